```python
import jax
import jax.numpy as jnp
from jax import lax
import numpy as np

D_MODEL = 1024
BATCH = 4
SEQ = 8192
DEPTH = 2

GRID_W = 64
CTX_LEN = 256
SSD_D_INNER = D_MODEL
SSD_HEAD_DIM = 64
SSD_HEADS = SSD_D_INNER // SSD_HEAD_DIM
SSD_GROUPS = 2
SSD_STATE = 128
SSD_CONV = 5
SSD_CHUNK = 128
SSD_XBC = SSD_D_INNER + 2 * SSD_GROUPS * SSD_STATE
SG_WIDTH = D_MODEL
SG_CHUNK = 128
SG_GROUP_DIM = 128
SG_GROUPS = SG_WIDTH // SG_GROUP_DIM
ROWS_PER_CHUNK = SG_CHUNK // GRID_W
MIX_WIDTH = SSD_D_INNER + SG_WIDTH
IN_PROJ = SSD_D_INNER + SSD_XBC + 2 * SSD_HEADS + 2 * SG_WIDTH
CONF_WIDTH = D_MODEL
CONF_CONV_WIDTH = 31
MLP_HIDDEN = 4 * D_MODEL
EPS = 1e-6

kernel_name = "hybrid_ssd_gmlp_conformer_dit"


def rmsnorm(x, g):
    xf = x.astype(jnp.float32)
    y = xf * lax.rsqrt(jnp.mean(xf * xf, axis=-1, keepdims=True) + EPS)
    return (y * g.astype(jnp.float32)).astype(x.dtype)


def layernorm(x, g, b):
    xf = x.astype(jnp.float32)
    mu = jnp.mean(xf, axis=-1, keepdims=True)
    xc = xf - mu
    y = xc * lax.rsqrt(jnp.mean(xc * xc, axis=-1, keepdims=True) + EPS)
    return (y * g.astype(jnp.float32) + b.astype(jnp.float32)).astype(x.dtype)


def modulate(x, g, shift, scale):
    return rmsnorm(x, g) * (1 + scale) + shift


def dwconv_centred(x, w, b):
    k = w.shape[0]
    y = lax.conv_general_dilated(
        x, w[:, None, :].astype(x.dtype), window_strides=(1,),
        padding=[(k // 2, k // 2)], dimension_numbers=('NWC', 'WIO', 'NWC'),
        feature_group_count=x.shape[-1])
    return y + b.astype(x.dtype)


def ssd_chunked(x, dt, a, bm, cm, h0):
    f32 = jnp.float32
    bsz, seq_len, n_heads, p_dim = x.shape
    n_groups, n_state = bm.shape[-2], bm.shape[-1]
    r = n_heads // n_groups
    nc = seq_len // SSD_CHUNK
    xc = x.astype(f32).reshape(bsz, nc, SSD_CHUNK, n_groups, r, p_dim)
    dtc = dt.reshape(bsz, nc, SSD_CHUNK, n_groups, r)
    bc = bm.astype(f32).reshape(bsz, nc, SSD_CHUNK, n_groups, n_state)
    cc = cm.astype(f32).reshape(bsz, nc, SSD_CHUNK, n_groups, n_state)
    acs = jnp.cumsum(dtc * a.astype(f32).reshape(n_groups, r), axis=2)
    lower = jnp.tril(jnp.ones((SSD_CHUNK, SSD_CHUNK), dtype=bool))[:, :, None, None]
    seg = acs[:, :, :, None] - acs[:, :, None, :]
    decay = jnp.exp(jnp.where(lower, seg, -jnp.inf))
    scores = jnp.einsum('bcign,bcjgn->bcijg', cc, bc)
    w = scores[..., None] * decay * dtc[:, :, None]
    y_diag = jnp.einsum('bcijgr,bcjgrp->bcigrp', w, xc)
    to_end = jnp.exp(acs[:, :, -1:] - acs) * dtc
    states = jnp.einsum('bcjgn,bcjgrp->bcgrpn', bc, xc * to_end[..., None])
    chunk_decay = jnp.exp(acs[:, :, -1])

    def step(h, inp):
        s, d = inp
        return h * d[..., None, None] + s, h

    h_last, h_in = lax.scan(step, h0, (jnp.moveaxis(states, 1, 0),
                                       jnp.moveaxis(chunk_decay, 1, 0)))
    h_in = jnp.moveaxis(h_in, 0, 1)
    y_off = jnp.einsum('bcign,bcgrpn->bcigrp', cc, h_in) * jnp.exp(acs)[..., None]
    y = (y_diag + y_off).reshape(bsz, seq_len, n_heads, p_dim).astype(x.dtype)
    return y, h_last


def ssd_bidirectional(xs, dt_f, dt_b, a_f, a_b, bm, cm, h0_f, h0_b):
    flip = lambda t: jnp.flip(t, axis=1)
    y_f, h_f = ssd_chunked(xs, dt_f, a_f, bm, cm, h0_f)
    y_b, h_b = ssd_chunked(flip(xs), flip(dt_b), a_b, flip(bm), flip(cm), h0_b)
    return y_f + flip(y_b), h_f, h_b


def hybrid_mixer(u, n_chunks, h0_f, h0_b, w_in, conv_w, conv_b, dt_bias, a_log, d_skip,
                 ssd_norm_g, sg_ln_g, sg_ln_b, sg_w, sg_b, w_out):
    bsz, seq_len, _ = u.shape
    proj = u @ w_in
    o1 = SSD_D_INNER
    o2 = o1 + SSD_XBC
    o3 = o2 + 2 * SSD_HEADS
    z, xbc, dt_raw, uv = jnp.split(proj, [o1, o2, o3], axis=-1)
    xbc = jax.nn.silu(dwconv_centred(xbc, conv_w, conv_b))
    xs, bm, cm = jnp.split(xbc, [SSD_D_INNER, SSD_D_INNER + SSD_GROUPS * SSD_STATE], axis=-1)
    xs = xs.reshape(bsz, seq_len, SSD_HEADS, SSD_HEAD_DIM)
    bm = bm.reshape(bsz, seq_len, SSD_GROUPS, SSD_STATE)
    cm = cm.reshape(bsz, seq_len, SSD_GROUPS, SSD_STATE)
    dt = jax.nn.softplus(dt_raw.astype(jnp.float32).reshape(bsz, seq_len, 2, SSD_HEADS)
                         + dt_bias.astype(jnp.float32))
    a = -jnp.exp(a_log.astype(jnp.float32))
    y, h_f, h_b = ssd_bidirectional(xs, dt[:, :, 0], dt[:, :, 1], a[0], a[1], bm, cm, h0_f, h0_b)
    y = y + d_skip[:, None].astype(y.dtype) * xs
    y = y.reshape(bsz, seq_len, SSD_D_INNER) * jax.nn.silu(z)
    y_ssd = rmsnorm(y.reshape(bsz, seq_len, SSD_GROUPS, -1),
                    ssd_norm_g.reshape(SSD_GROUPS, -1)).reshape(bsz, seq_len, SSD_D_INNER)
    gate_u, v = jnp.split(jax.nn.gelu(uv), 2, axis=-1)
    v = layernorm(v, sg_ln_g, sg_ln_b).reshape(bsz, n_chunks, SG_CHUNK, SG_GROUPS, SG_GROUP_DIM)
    s = jnp.einsum('gqk,bnkgc->bnqgc', sg_w, v) + sg_b.T[:, :, None]
    y_sg = gate_u * s.reshape(bsz, seq_len, SG_WIDTH)
    out = jnp.concatenate([y_ssd, y_sg], axis=-1) @ w_out
    return out, h_f, h_b


def conformer_conv(u, w_pw1, b_pw1, w_dw, b_dw, ln_g, ln_b, w_pw2, b_pw2):
    a = u @ w_pw1 + b_pw1
    a = a[..., :CONF_WIDTH] * jax.nn.sigmoid(a[..., CONF_WIDTH:])
    a = dwconv_centred(a, w_dw, b_dw)
    a = jax.nn.silu(layernorm(a, ln_g, ln_b))
    return a @ w_pw2 + b_pw2


def sq_relu_mlp(u, w1, w2):
    return jnp.square(jax.nn.relu(u @ w1)) @ w2


def setup_inputs(seed: int = 0) -> dict:
    key = jax.random.key(seed)
    ks = jax.random.split(key, 40)
    n_even = (DEPTH + 1) // 2
    n_odd = DEPTH // 2
    nrm = lambda k, shape, s: jax.random.normal(k, shape, jnp.float32) * s
    gain = lambda k, shape: 1.0 + 0.05 * jax.random.normal(k, shape, jnp.float32)
    dt0 = jnp.exp(jax.random.uniform(ks[13], (n_even, 2, SSD_HEADS), jnp.float32,
                                     np.log(1e-3), np.log(1e-1)))
    return {
        "x": nrm(ks[0], (BATCH, SEQ, D_MODEL), 1.0),
        "c": nrm(ks[1], (BATCH, D_MODEL), 1.0),
        "ctx": nrm(ks[2], (BATCH, CTX_LEN, D_MODEL), 1.0),
        "c_ctx": nrm(ks[3], (D_MODEL,), 1.0),
        "ada_w": nrm(ks[4], (DEPTH, D_MODEL, 6 * D_MODEL), 0.5 * D_MODEL ** -0.5),
        "ada_b": nrm(ks[5], (DEPTH, 6 * D_MODEL), 0.02),
        "norm_mix_g": gain(ks[6], (DEPTH, D_MODEL)),
        "norm_mlp_g": gain(ks[7], (DEPTH, D_MODEL)),
        "mlp_w1": nrm(ks[8], (DEPTH, D_MODEL, MLP_HIDDEN), D_MODEL ** -0.5),
        "mlp_w2": nrm(ks[9], (DEPTH, MLP_HIDDEN, D_MODEL), MLP_HIDDEN ** -0.5),
        "hy_w_in": nrm(ks[10], (n_even, D_MODEL, IN_PROJ), D_MODEL ** -0.5),
        "ssd_conv_w": nrm(ks[11], (n_even, SSD_CONV, SSD_XBC), SSD_CONV ** -0.5),
        "ssd_conv_b": nrm(ks[12], (n_even, SSD_XBC), 0.02),
        "ssd_dt_bias": dt0 + jnp.log(-jnp.expm1(-dt0)),
        "ssd_a_log": jnp.log(jax.random.uniform(ks[14], (n_even, 2, SSD_HEADS), jnp.float32, 1.0, 16.0)),
        "ssd_d": 1.0 + 0.1 * jax.random.normal(ks[15], (n_even, SSD_HEADS), jnp.float32),
        "ssd_norm_g": gain(ks[16], (n_even, SSD_D_INNER)),
        "sg_ln_g": gain(ks[17], (n_even, SG_WIDTH)),
        "sg_ln_b": nrm(ks[18], (n_even, SG_WIDTH), 0.02),
        "sg_w": nrm(ks[19], (n_even, SG_GROUPS, SG_CHUNK, SG_CHUNK), 0.5 * SG_CHUNK ** -0.5),
        "sg_b": gain(ks[20], (n_even, SG_GROUPS, SG_CHUNK)),
        "hy_w_out": nrm(ks[21], (n_even, MIX_WIDTH, D_MODEL), MIX_WIDTH ** -0.5),
        "cf_w_pw1": nrm(ks[22], (n_odd, D_MODEL, 2 * CONF_WIDTH), D_MODEL ** -0.5),
        "cf_b_pw1": nrm(ks[23], (n_odd, 2 * CONF_WIDTH), 0.02),
        "cf_w_dw": nrm(ks[24], (n_odd, CONF_CONV_WIDTH, CONF_WIDTH), CONF_CONV_WIDTH ** -0.5),
        "cf_b_dw": nrm(ks[25], (n_odd, CONF_WIDTH), 0.02),
        "cf_ln_g": gain(ks[26], (n_odd, CONF_WIDTH)),
        "cf_ln_b": nrm(ks[27], (n_odd, CONF_WIDTH), 0.02),
        "cf_w_pw2": nrm(ks[28], (n_odd, CONF_WIDTH, D_MODEL), CONF_WIDTH ** -0.5),
        "cf_b_pw2": nrm(ks[29], (n_odd, D_MODEL), 0.02),
        "final_norm_g": gain(ks[30], (D_MODEL,)),
    }


def reference(x, c, ctx, c_ctx, ada_w, ada_b, norm_mix_g, norm_mlp_g, mlp_w1, mlp_w2,
              hy_w_in, ssd_conv_w, ssd_conv_b, ssd_dt_bias, ssd_a_log, ssd_d, ssd_norm_g,
              sg_ln_g, sg_ln_b, sg_w, sg_b, hy_w_out, cf_w_pw1, cf_b_pw1, cf_w_dw, cf_b_dw,
              cf_ln_g, cf_ln_b, cf_w_pw2, cf_b_pw2, final_norm_g):
    bsz = x.shape[0]
    rows = x.shape[1] // GRID_W
    lat_chunks = rows // ROWS_PER_CHUNK
    ctx_chunks = ctx.shape[1] // SG_CHUNK
    silu_c = jax.nn.silu(c)[:, None, :]
    silu_cc = jax.nn.silu(c_ctx)[None, None, :]
    h, hc = x, ctx
    for i in range(DEPTH):
        ctx_needed = i < DEPTH - 1
        sh1, sc1, g1, sh2, sc2, g2 = jnp.split(silu_c @ ada_w[i] + ada_b[i], 6, axis=-1)
        csh1, csc1, cg1, csh2, csc2, cg2 = jnp.split(silu_cc @ ada_w[i] + ada_b[i], 6, axis=-1)
        j = i // 2
        if i % 2 == 0:
            p = (hy_w_in[j], ssd_conv_w[j], ssd_conv_b[j], ssd_dt_bias[j], ssd_a_log[j], ssd_d[j],
                 ssd_norm_g[j], sg_ln_g[j], sg_ln_b[j], sg_w[j], sg_b[j], hy_w_out[j])
            zeros = jnp.zeros((bsz, SSD_GROUPS, SSD_HEADS // SSD_GROUPS, SSD_HEAD_DIM, SSD_STATE),
                              jnp.float32)
            ctx_mix, h0_f, h0_b = hybrid_mixer(modulate(hc, norm_mix_g[i], csh1, csc1),
                                               ctx_chunks, zeros, zeros, *p)
            lat_mix, _, _ = hybrid_mixer(modulate(h, norm_mix_g[i], sh1, sc1),
                                         lat_chunks, h0_f, h0_b, *p)
            h = h + g1 * lat_mix
            if ctx_needed:
                hc = hc + cg1 * ctx_mix
        else:
            p = (cf_w_pw1[j], cf_b_pw1[j], cf_w_dw[j], cf_b_dw[j], cf_ln_g[j], cf_ln_b[j],
                 cf_w_pw2[j], cf_b_pw2[j])
            h = h + g1 * conformer_conv(modulate(h, norm_mix_g[i], sh1, sc1), *p)
            if ctx_needed:
                hc = hc + cg1 * conformer_conv(modulate(hc, norm_mix_g[i], csh1, csc1), *p)
        h = h + g2 * sq_relu_mlp(modulate(h, norm_mlp_g[i], sh2, sc2), mlp_w1[i], mlp_w2[i])
        if ctx_needed:
            hc = hc + cg2 * sq_relu_mlp(modulate(hc, norm_mlp_g[i], csh2, csc2), mlp_w1[i], mlp_w2[i])
    return rmsnorm(h, final_norm_g)
```

```python
import functools

import jax
import jax.numpy as jnp
from jax import lax
from jax.experimental import pallas as pl
from jax.experimental.pallas import tpu as pltpu

F32 = jnp.float32
BF16 = jnp.bfloat16

SSD_HEADS = 16
SSD_HEAD_DIM = 64
SSD_GROUPS = 2
SSD_STATE = 128
SSD_CONV = 5
CHUNK = 128
SG_GROUPS = 8
SG_GROUP_DIM = 128
CONF_CONV = 31
EPS = 1e-6
N_MOD = 6
MOD_ROWS = 8
HALO = 16
LANES = 128
VMEM_LIMIT = 56 * 1024 * 1024


def _cparams(*sem):
    return pltpu.CompilerParams(dimension_semantics=sem, vmem_limit_bytes=VMEM_LIMIT)


def _dot(a, b):
    return jnp.dot(a, b, preferred_element_type=F32)


def _sigmoid(x):
    return 1.0 / (1.0 + jnp.exp(-x))


def _silu(x):
    return x * _sigmoid(x)


def _gelu_tanh(x):
    c = 0.7978845608028654
    return x * (0.5 * (1.0 + jnp.tanh(c * (x + 0.044715 * (x * x * x)))))


def _softplus(x):
    return jnp.maximum(x, 0.0) + jnp.log1p(jnp.exp(-jnp.abs(x)))


def _modulate(hf, g, shift, scale):
    ms = jnp.mean(hf * hf, axis=-1, keepdims=True)
    return (hf * lax.rsqrt(ms + EPS) * g) * (1.0 + scale) + shift


def _layernorm(x, g, b):
    mu = jnp.mean(x, axis=-1, keepdims=True)
    xc = x - mu
    var = jnp.mean(xc * xc, axis=-1, keepdims=True)
    return xc * lax.rsqrt(var + EPS) * g + b


def _mod_vec(mod_ref, k):
    d = mod_ref.shape[-1] // N_MOD
    return mod_ref[:, k * d:(k + 1) * d]


def _cumsum_rows(x, reverse):
    n = x.shape[0]
    row = lax.broadcasted_iota(jnp.int32, x.shape, 0)
    s = 1
    while s < n:
        if reverse:
            x = x + jnp.where(row < n - s, pltpu.roll(x, n - s, axis=0), 0.0)
        else:
            x = x + jnp.where(row >= s, pltpu.roll(x, s, axis=0), 0.0)
        s *= 2
    return x


def _expand_heads(x, e, terms):
    out = None
    r = x
    for _ in range(terms):
        p = r.astype(BF16)
        t = _dot(p, e)
        out = t if out is None else out + t
        r = r - p.astype(F32)
    return out


def _ada_kernel(cs_ref, w_ref, b_ref, o_ref):
    s = _silu(cs_ref[...]).astype(BF16)
    o_ref[...] = _dot(s, w_ref[...].astype(BF16)) + b_ref[...]


def _ada_call(cs, ada_w, ada_b):
    depth, d, n = ada_w.shape
    tn = n // 4
    out = pl.pallas_call(
        _ada_kernel,
        out_shape=jax.ShapeDtypeStruct((depth, MOD_ROWS, n), F32),
        grid=(depth, n // tn),
        in_specs=[
            pl.BlockSpec((MOD_ROWS, d), lambda l, j: (0, 0)),
            pl.BlockSpec((None, d, tn), lambda l, j: (l, 0, j)),
            pl.BlockSpec((None, 1, tn), lambda l, j: (l, 0, j)),
        ],
        out_specs=pl.BlockSpec((None, MOD_ROWS, tn), lambda l, j: (l, 0, j)),
        compiler_params=_cparams("parallel", "parallel"),
        name="ada_mod",
    )(cs, ada_w, ada_b.reshape(depth, 1, n))
    return out.reshape(depth * MOD_ROWS, 1, n)


def _halo_specs(tm, n_tiles, d):
    per = tm // HALO
    last = n_tiles * per - 1
    prev = pl.BlockSpec((None, None, HALO, d),
                        lambda b, i: (b, jnp.maximum(i * per - 1, 0), 0, 0))
    nxt = pl.BlockSpec((None, None, HALO, d),
                       lambda b, i: (b, jnp.minimum((i + 1) * per, last), 0, 0))
    return prev, nxt


def _ext_rows(hp_ref, h_ref, hn_ref):
    return jnp.concatenate([hp_ref[...], h_ref[...], hn_ref[...]], axis=0)


def _ext_valid(tm):
    i = pl.program_id(1)
    n = pl.num_programs(1)
    row = lax.broadcasted_iota(jnp.int32, (tm + 2 * HALO, 1), 0)
    return jnp.logical_and(jnp.logical_or(row >= HALO, i > 0),
                           jnp.logical_or(row < HALO + tm, i < n - 1))


def _inproj_kernel(hp_ref, h_ref, hn_ref, mod_ref, g_ref, w_ref, cw_ref, cb_ref, dtb_ref,
                   lng_ref, lnb_ref, zs_ref, xbc_ref, dt_ref, gate_ref, v_ref):
    tm, d = h_ref.shape
    n_xbc = xbc_ref.shape[-1]
    o_xbc = d
    o_uv = o_xbc + n_xbc
    o_dt = o_uv + 2 * d
    ext = _ext_rows(hp_ref, h_ref, hn_ref)
    u = _modulate(ext, g_ref[...], _mod_vec(mod_ref, 0), _mod_vec(mod_ref, 1))
    u = jnp.where(_ext_valid(tm), u, 0.0).astype(BF16)
    um = u[HALO:HALO + tm]

    xe = _dot(u, w_ref[:, o_xbc:o_uv])
    off = HALO - SSD_CONV // 2
    acc = cb_ref[...] + cw_ref[0:1, :] * xe[off:off + tm]
    for k in range(1, SSD_CONV):
        acc = acc + cw_ref[k:k + 1, :] * xe[off + k:off + k + tm]
    xbc_ref[...] = _silu(acc).astype(BF16)

    zs_ref[...] = _silu(_dot(um, w_ref[:, 0:d])).astype(BF16)

    lane = lax.broadcasted_iota(jnp.int32, (1, LANES), 1)
    dt = _softplus(_dot(um, w_ref[:, o_dt:o_dt + LANES]) + dtb_ref[...])
    dt_ref[...] = jnp.where(lane < 2 * SSD_HEADS, dt, 0.0)

    gate_ref[...] = _gelu_tanh(_dot(um, w_ref[:, o_uv:o_uv + d])).astype(BF16)
    v = _gelu_tanh(_dot(um, w_ref[:, o_uv + d:o_uv + 2 * d]))
    v_ref[...] = _layernorm(v, lng_ref[...], lnb_ref[...]).astype(BF16)


def _inproj_call(h, mod, mod_row, norm_g, w_in_r, conv_w, conv_b, dt_bias_p, ln_g, ln_b, tm):
    bsz, seq, d = h.shape
    n_tiles = seq // tm
    n_xbc = conv_w.shape[-1]
    hv = h.reshape(bsz, seq // HALO, HALO, d)
    prev, nxt = _halo_specs(tm, n_tiles, d)
    const = lambda shape: pl.BlockSpec(shape, lambda b, i: (0,) * len(shape))
    tile = lambda n: pl.BlockSpec((None, tm, n), lambda b, i: (b, i, 0))
    out_shape = (
        jax.ShapeDtypeStruct((bsz, seq, d), BF16),
        jax.ShapeDtypeStruct((bsz, seq, n_xbc), BF16),
        jax.ShapeDtypeStruct((bsz, seq, LANES), F32),
        jax.ShapeDtypeStruct((bsz, seq, d), BF16),
        jax.ShapeDtypeStruct((bsz, seq, d), BF16),
    )
    return pl.pallas_call(
        _inproj_kernel,
        out_shape=out_shape,
        grid=(bsz, n_tiles),
        in_specs=[
            prev, tile(d), nxt,
            pl.BlockSpec((None, 1, mod.shape[-1]), lambda b, i: (mod_row(b), 0, 0)),
            const((1, d)),
            const(w_in_r.shape),
            const(conv_w.shape), const((1, n_xbc)), const((1, LANES)),
            const((1, d)), const((1, d)),
        ],
        out_specs=(tile(d), tile(n_xbc), tile(LANES), tile(d), tile(d)),
        compiler_params=_cparams("parallel", "parallel"),
        name="hy_inproj",
    )(hv, h, hv, mod, norm_g.reshape(1, d), w_in_r, conv_w, conv_b.reshape(1, n_xbc),
      dt_bias_p, ln_g.reshape(1, d), ln_b.reshape(1, d))


def _ssd_state_kernel(xbc_ref, dt_ref, alog_ref, e_ref, h0_ref, hin_ref, hfin_ref, hs_ref,
                      *, reverse):
    n_chunks = hin_ref.shape[0]
    d_inner = e_ref.shape[-1]
    gw = d_inner // SSD_GROUPS
    i = pl.program_id(1)

    @pl.when(i == 0)
    def _():
        hs_ref[...] = h0_ref[...]

    a = -jnp.exp(alog_ref[...])
    e = e_ref[...]
    for t in range(n_chunks):
        c = n_chunks - 1 - t if reverse else t
        rows = pl.ds(c * CHUNK, CHUNK)
        dt = dt_ref[rows, :]
        cs = _cumsum_rows(dt * a, reverse)
        edge = cs[0:1, :] if reverse else cs[CHUNK - 1:CHUNK, :]
        w = _expand_heads(jnp.exp(edge - cs) * dt, e, 2)
        decay = _expand_heads(jnp.broadcast_to(jnp.exp(edge), (8, LANES)), e, 3)[0:1, :]
        xw = (xbc_ref[rows, 0:d_inner].astype(F32) * w).astype(BF16)
        for g in range(SSD_GROUPS):
            bt = xbc_ref[rows, d_inner + g * SSD_STATE:d_inner + (g + 1) * SSD_STATE]
            bt = bt.astype(F32).T.astype(BF16)
            s = _dot(bt, xw[:, g * gw:(g + 1) * gw])
            hcur = hs_ref[g]
            hin_ref[c, g] = hcur.astype(BF16)
            hs_ref[g] = hcur * decay[:, g * gw:(g + 1) * gw] + s

    @pl.when(i == pl.num_programs(1) - 1)
    def _():
        hfin_ref[...] = hs_ref[...]


def _ssd_state_call(xbc, dt, alog_p, e, h0, reverse, tq):
    bsz, seq, _ = xbc.shape
    d_inner = e.shape[-1]
    gw = d_inner // SSD_GROUPS
    n_blk = seq // tq
    cpb = tq // CHUNK
    blk = (lambda i: n_blk - 1 - i) if reverse else (lambda i: i)
    n_xb = d_inner + SSD_GROUPS * SSD_STATE
    return pl.pallas_call(
        functools.partial(_ssd_state_kernel, reverse=reverse),
        out_shape=(
            jax.ShapeDtypeStruct((bsz, seq // CHUNK, SSD_GROUPS, SSD_STATE, gw), BF16),
            jax.ShapeDtypeStruct((bsz, SSD_GROUPS, SSD_STATE, gw), F32),
        ),
        grid=(bsz, n_blk),
        in_specs=[
            pl.BlockSpec((None, tq, n_xb), lambda b, i: (b, blk(i), 0)),
            pl.BlockSpec((None, tq, LANES), lambda b, i: (b, blk(i), 0)),
            pl.BlockSpec((1, LANES), lambda b, i: (0, 0)),
            pl.BlockSpec(e.shape, lambda b, i: (0, 0)),
            pl.BlockSpec((None, SSD_GROUPS, SSD_STATE, gw), lambda b, i: (b, 0, 0, 0)),
        ],
        out_specs=(
            pl.BlockSpec((None, cpb, SSD_GROUPS, SSD_STATE, gw), lambda b, i: (b, blk(i), 0, 0, 0)),
            pl.BlockSpec((None, SSD_GROUPS, SSD_STATE, gw), lambda b, i: (b, 0, 0, 0)),
        ),
        scratch_shapes=[pltpu.VMEM((SSD_GROUPS, SSD_STATE, gw), F32)],
        compiler_params=_cparams("parallel", "arbitrary"),
        name="ssd_state_bwd" if reverse else "ssd_state_fwd",
    )(xbc, dt, alog_p, e, h0)


def _mix_out_kernel(h_ref, xbc_ref, dt_ref, zs_ref, gate_ref, v_ref, hinf_ref, hinb_ref,
                    alog_ref, ef_ref, eb_ref, dskip_ref, ng_ref, sgw_ref, sgb_ref, wout_ref,
                    mod_ref, o_ref, ycat_ref):
    tq, d = h_ref.shape
    n_chunks = tq // CHUNK
    d_inner = ef_ref.shape[-1]
    gw = d_inner // SSD_GROUPS
    hpg = SSD_HEADS // SSD_GROUPS
    a = -jnp.exp(alog_ref[...])
    lane = lax.broadcasted_iota(jnp.int32, (CHUNK, LANES), 1)
    ii = lax.broadcasted_iota(jnp.int32, (CHUNK, CHUNK), 0)
    jj = lax.broadcasted_iota(jnp.int32, (CHUNK, CHUNK), 1)
    low = ii >= jj
    diag = ii == jj
    lane_lo = lane < SSD_HEAD_DIM

    def chunk(c, carry):
        rows = pl.ds(pl.multiple_of(c * CHUNK, CHUNK), CHUNK)
        dt = dt_ref[rows, :]
        da = dt * a
        cs = jnp.where(lane < SSD_HEADS, _cumsum_rows(da, False), _cumsum_rows(da, True))
        cs_t = cs.T
        dt_t = dt.T
        ecs = jnp.exp(cs)
        scale_f = _expand_heads(ecs, ef_ref[...], 2)
        scale_b = _expand_heads(ecs, eb_ref[...], 2)
        xs = xbc_ref[rows, 0:d_inner]
        for g in range(SSD_GROUPS):
            bm = xbc_ref[rows, d_inner + g * SSD_STATE:d_inner + (g + 1) * SSD_STATE]
            o_c = d_inner + (SSD_GROUPS + g) * SSD_STATE
            cm = xbc_ref[rows, o_c:o_c + SSD_STATE]
            scores = lax.dot_general(cm, bm, (((1,), (1,)), ((), ())),
                                     preferred_element_type=F32)
            cols = slice(g * gw, (g + 1) * gw)
            y_g = (_dot(cm, hinf_ref[c, g]) * scale_f[:, cols]
                   + _dot(cm, hinb_ref[c, g]) * scale_b[:, cols])
            ys = []
            for pair in range(hpg // 2):
                ws = []
                for hh in range(2):
                    hd = g * hpg + pair * 2 + hh
                    hb = SSD_HEADS + hd
                    seg = jnp.where(low, cs[:, hd:hd + 1] - cs_t[hd:hd + 1, :],
                                    cs[:, hb:hb + 1] - cs_t[hb:hb + 1, :])
                    wdt = (jnp.where(low, dt_t[hd:hd + 1, :], dt_t[hb:hb + 1, :])
                           + jnp.where(diag, dt_t[hb:hb + 1, :], 0.0))
                    ws.append((scores * jnp.exp(seg) * wdt).astype(BF16))
                p0 = (g * hpg + pair * 2) * SSD_HEAD_DIM
                xp = xs[:, p0:p0 + 2 * SSD_HEAD_DIM]
                zero = jnp.zeros_like(xp)
                rhs = jnp.concatenate([jnp.where(lane_lo, xp, zero),
                                       jnp.where(lane_lo, zero, xp)], axis=0)
                ys.append(_dot(jnp.concatenate(ws, axis=1), rhs))
            y_g = y_g + jnp.concatenate(ys, axis=1)
            y_g = y_g + dskip_ref[:, cols] * xs[:, cols].astype(F32)
            y_g = y_g * zs_ref[rows, cols].astype(F32)
            ms = jnp.mean(y_g * y_g, axis=-1, keepdims=True)
            y_g = y_g * lax.rsqrt(ms + EPS) * ng_ref[:, cols]
            ycat_ref[rows, cols] = y_g.astype(BF16)
        for sg in range(SG_GROUPS):
            cols = slice(sg * SG_GROUP_DIM, (sg + 1) * SG_GROUP_DIM)
            s = _dot(sgw_ref[sg], v_ref[rows, cols]) + sgb_ref[:, cols]
            ycat_ref[rows, d_inner + sg * SG_GROUP_DIM:d_inner + (sg + 1) * SG_GROUP_DIM] = (
                gate_ref[rows, cols].astype(F32) * s).astype(BF16)
        return carry

    lax.fori_loop(0, n_chunks, chunk, 0)
    out = _dot(ycat_ref[...], wout_ref[...])
    o_ref[...] = h_ref[...] + _mod_vec(mod_ref, 2) * out


def _mix_out_call(h, xbc, dt, zs, gate, v, hin_f, hin_b, alog_p, e_f, e_b, dskip, norm_g,
                  sg_w, sg_b_exp, w_out, mod, mod_row, tq):
    bsz, seq, d = h.shape
    d_inner = e_f.shape[-1]
    gw = d_inner // SSD_GROUPS
    cpb = tq // CHUNK
    const = lambda shape: pl.BlockSpec(shape, lambda b, i: (0,) * len(shape))
    tile = lambda n: pl.BlockSpec((None, tq, n), lambda b, i: (b, i, 0))
    hin_spec = pl.BlockSpec((None, cpb, SSD_GROUPS, SSD_STATE, gw), lambda b, i: (b, i, 0, 0, 0))
    return pl.pallas_call(
        _mix_out_kernel,
        out_shape=jax.ShapeDtypeStruct((bsz, seq, d), F32),
        grid=(bsz, seq // tq),
        in_specs=[
            tile(d), tile(xbc.shape[-1]), tile(LANES), tile(d_inner), tile(d), tile(d),
            hin_spec, hin_spec,
            const((1, LANES)), const(e_f.shape), const(e_b.shape),
            const((1, d_inner)), const((1, d_inner)),
            const(sg_w.shape), const(sg_b_exp.shape), const(w_out.shape),
            pl.BlockSpec((None, 1, mod.shape[-1]), lambda b, i: (mod_row(b), 0, 0)),
        ],
        out_specs=tile(d),
        scratch_shapes=[pltpu.VMEM((tq, w_out.shape[0]), BF16)],
        compiler_params=_cparams("parallel", "parallel"),
        name="hy_mix_out",
    )(h, xbc, dt, zs, gate, v, hin_f, hin_b, alog_p, e_f, e_b, dskip, norm_g.reshape(1, d_inner),
      sg_w, sg_b_exp, w_out, mod)


def _conformer_kernel(hp_ref, h_ref, hn_ref, mod_ref, g_ref, w1_ref, b1_ref, wd_ref, bd_ref,
                      lng_ref, lnb_ref, w2_ref, b2_ref, o_ref, a_ref):
    tm, d = h_ref.shape
    cw = w2_ref.shape[0]
    ext = _ext_rows(hp_ref, h_ref, hn_ref)
    u = _modulate(ext, g_ref[...], _mod_vec(mod_ref, 0), _mod_vec(mod_ref, 1)).astype(BF16)
    a = _dot(u, w1_ref[...]) + b1_ref[...]
    a = a[:, 0:cw] * _sigmoid(a[:, cw:2 * cw])
    a_ref[...] = jnp.where(_ext_valid(tm), a, 0.0)
    off = HALO - CONF_CONV // 2
    acc = bd_ref[...] + wd_ref[0:1, :] * a_ref[off:off + tm, :]
    for k in range(1, CONF_CONV):
        acc = acc + wd_ref[k:k + 1, :] * a_ref[off + k:off + k + tm, :]
    y = _silu(_layernorm(acc, lng_ref[...], lnb_ref[...])).astype(BF16)
    out = _dot(y, w2_ref[...]) + b2_ref[...]
    o_ref[...] = h_ref[...] + _mod_vec(mod_ref, 2) * out


def _conformer_call(h, mod, mod_row, norm_g, w1, b1, wd, bd, ln_g, ln_b, w2, b2, tm):
    bsz, seq, d = h.shape
    n_tiles = seq // tm
    cw = w2.shape[0]
    hv = h.reshape(bsz, seq // HALO, HALO, d)
    prev, nxt = _halo_specs(tm, n_tiles, d)
    const = lambda shape: pl.BlockSpec(shape, lambda b, i: (0,) * len(shape))
    tile = pl.BlockSpec((None, tm, d), lambda b, i: (b, i, 0))
    return pl.pallas_call(
        _conformer_kernel,
        out_shape=jax.ShapeDtypeStruct((bsz, seq, d), F32),
        grid=(bsz, n_tiles),
        in_specs=[
            prev, tile, nxt,
            pl.BlockSpec((None, 1, mod.shape[-1]), lambda b, i: (mod_row(b), 0, 0)),
            const((1, d)), const(w1.shape), const((1, 2 * cw)), const(wd.shape), const((1, cw)),
            const((1, cw)), const((1, cw)), const(w2.shape), const((1, d)),
        ],
        out_specs=tile,
        scratch_shapes=[pltpu.VMEM((tm + 2 * HALO, cw), F32)],
        compiler_params=_cparams("parallel", "parallel"),
        name="conformer",
    )(hv, h, hv, mod, norm_g.reshape(1, d), w1, b1.reshape(1, 2 * cw), wd, bd.reshape(1, cw),
      ln_g.reshape(1, cw), ln_b.reshape(1, cw), w2, b2.reshape(1, d))


def _mlp_kernel(h_ref, mod_ref, g_ref, w1_ref, w2_ref, fg_ref, o_ref, a_ref, *, final_norm):
    hf = h_ref[...]
    u = _modulate(hf, g_ref[...], _mod_vec(mod_ref, 3), _mod_vec(mod_ref, 4)).astype(BF16)
    hid = w1_ref.shape[1]
    step = min(hid, 1024)
    for k in range(0, hid, step):
        r = jnp.maximum(_dot(u, w1_ref[:, k:k + step]), 0.0)
        a_ref[:, k:k + step] = (r * r).astype(BF16)
    out = hf + _mod_vec(mod_ref, 5) * _dot(a_ref[...], w2_ref[...])
    if final_norm:
        ms = jnp.mean(out * out, axis=-1, keepdims=True)
        out = out * lax.rsqrt(ms + EPS) * fg_ref[...]
    o_ref[...] = out


def _mlp_call(h, mod, mod_row, norm_g, w1, w2, final_g, final_norm, tm):
    bsz, seq, d = h.shape
    const = lambda shape: pl.BlockSpec(shape, lambda b, i: (0,) * len(shape))
    tile = pl.BlockSpec((None, tm, d), lambda b, i: (b, i, 0))
    return pl.pallas_call(
        functools.partial(_mlp_kernel, final_norm=final_norm),
        out_shape=jax.ShapeDtypeStruct((bsz, seq, d), F32),
        grid=(bsz, seq // tm),
        in_specs=[
            tile,
            pl.BlockSpec((None, 1, mod.shape[-1]), lambda b, i: (mod_row(b), 0, 0)),
            const((1, d)), const(w1.shape), const(w2.shape), const((1, d)),
        ],
        out_specs=tile,
        scratch_shapes=[pltpu.VMEM((tm, w1.shape[1]), BF16)],
        compiler_params=_cparams("parallel", "parallel"),
        name="mlp_final" if final_norm else "mlp",
    )(h, mod, norm_g.reshape(1, d), w1, w2, final_g.reshape(1, d))


def _tile(seq, want):
    return min(seq, want)


def _hybrid_mixer(h, mod, mod_row, norm_g, p, h0_f, h0_b, need_out):
    seq = h.shape[1]
    zs, xbc, dt, gate, v = _inproj_call(
        h, mod, mod_row, norm_g, p["w_in"], p["conv_w"], p["conv_b"], p["dt_bias"],
        p["ln_g"], p["ln_b"], _tile(seq, 256))
    tq = _tile(seq, 512)
    hin_f, hfin_f = _ssd_state_call(xbc, dt, p["alog"], p["e_f"], h0_f, False, tq)
    hin_b, hfin_b = _ssd_state_call(xbc, dt, p["alog"], p["e_b"], h0_b, True, tq)
    out = None
    if need_out:
        out = _mix_out_call(h, xbc, dt, zs, gate, v, hin_f, hin_b, p["alog"], p["e_f"], p["e_b"],
                            p["dskip"], p["norm_g"], p["sg_w"], p["sg_b"], p["w_out"], mod,
                            mod_row, tq)
    return out, hfin_f, hfin_b


def _hybrid_params(j, hy_w_in, ssd_conv_w, ssd_conv_b, ssd_dt_bias, ssd_a_log, ssd_d, ssd_norm_g,
                   sg_ln_g, sg_ln_b, sg_w, sg_b, hy_w_out):
    d_inner = SSD_HEADS * SSD_HEAD_DIM
    n_xbc = d_inner + 2 * SSD_GROUPS * SSD_STATE
    w = hy_w_in[j]
    o2 = d_inner + n_xbc
    o3 = o2 + 2 * SSD_HEADS
    pad = jnp.zeros((w.shape[0], LANES - 2 * SSD_HEADS), w.dtype)
    w_in_r = jnp.concatenate([w[:, :o2], w[:, o3:], w[:, o2:o3], pad], axis=1).astype(BF16)
    lane_pad = lambda t: jnp.pad(t.reshape(1, -1), ((0, 0), (0, LANES - t.size)))
    head = jnp.arange(d_inner, dtype=jnp.int32) // SSD_HEAD_DIM
    row = jnp.arange(LANES, dtype=jnp.int32)[:, None]
    sg_width = SG_GROUPS * SG_GROUP_DIM
    sg_b_exp = jnp.broadcast_to(sg_b[j].T[:, :, None],
                                (CHUNK, SG_GROUPS, SG_GROUP_DIM)).reshape(CHUNK, sg_width)
    return dict(
        w_in=w_in_r, conv_w=ssd_conv_w[j], conv_b=ssd_conv_b[j],
        dt_bias=lane_pad(ssd_dt_bias[j]), alog=lane_pad(ssd_a_log[j]),
        e_f=(row == head[None, :]).astype(BF16),
        e_b=(row == head[None, :] + SSD_HEADS).astype(BF16),
        dskip=jnp.repeat(ssd_d[j], SSD_HEAD_DIM).reshape(1, d_inner),
        norm_g=ssd_norm_g[j], ln_g=sg_ln_g[j], ln_b=sg_ln_b[j],
        sg_w=sg_w[j].astype(BF16), sg_b=sg_b_exp, w_out=hy_w_out[j].astype(BF16))


def kernel(x, c, ctx, c_ctx, ada_w, ada_b, norm_mix_g, norm_mlp_g, mlp_w1, mlp_w2, hy_w_in,
           ssd_conv_w, ssd_conv_b, ssd_dt_bias, ssd_a_log, ssd_d, ssd_norm_g, sg_ln_g, sg_ln_b,
           sg_w, sg_b, hy_w_out, cf_w_pw1, cf_b_pw1, cf_w_dw, cf_b_dw, cf_ln_g, cf_ln_b, cf_w_pw2,
           cf_b_pw2, final_norm_g):
    bsz, seq, d = x.shape
    depth = ada_w.shape[0]
    assert bsz < MOD_ROWS and seq % 512 == 0 and ctx.shape[1] % (2 * CHUNK) == 0
    cs = jnp.concatenate([c, c_ctx[None, :], jnp.zeros((MOD_ROWS - bsz - 1, d), c.dtype)], axis=0)
    mod = _ada_call(cs, ada_w, ada_b)

    gw = SSD_HEADS * SSD_HEAD_DIM // SSD_GROUPS
    zeros = jnp.zeros((bsz, SSD_GROUPS, SSD_STATE, gw), F32)
    h, hc = x, ctx
    for i in range(depth):
        ctx_needed = i < depth - 1
        last = i == depth - 1
        lat_row = lambda b, i=i: i * MOD_ROWS + b
        ctx_row = lambda b, i=i: i * MOD_ROWS + bsz
        j = i // 2
        if i % 2 == 0:
            p = _hybrid_params(j, hy_w_in, ssd_conv_w, ssd_conv_b, ssd_dt_bias, ssd_a_log, ssd_d,
                               ssd_norm_g, sg_ln_g, sg_ln_b, sg_w, sg_b, hy_w_out)
            hc_new, h0_f, h0_b = _hybrid_mixer(hc, mod, ctx_row, norm_mix_g[i], p, zeros, zeros,
                                               ctx_needed)
            h, _, _ = _hybrid_mixer(h, mod, lat_row, norm_mix_g[i], p, h0_f, h0_b, True)
            if ctx_needed:
                hc = hc_new
        else:
            w1 = cf_w_pw1[j].astype(BF16)
            w2 = cf_w_pw2[j].astype(BF16)
            args = (w1, cf_b_pw1[j], cf_w_dw[j], cf_b_dw[j], cf_ln_g[j], cf_ln_b[j], w2,
                    cf_b_pw2[j])
            h = _conformer_call(h, mod, lat_row, norm_mix_g[i], *args, _tile(seq, 256))
            if ctx_needed:
                hc = _conformer_call(hc, mod, ctx_row, norm_mix_g[i], *args, _tile(hc.shape[1], 256))
        w1 = mlp_w1[i].astype(BF16)
        w2 = mlp_w2[i].astype(BF16)
        h = _mlp_call(h, mod, lat_row, norm_mlp_g[i], w1, w2, final_norm_g, last, _tile(seq, 512))
        if ctx_needed:
            hc = _mlp_call(hc, mod, ctx_row, norm_mlp_g[i], w1, w2, final_norm_g, False,
                           _tile(hc.shape[1], 512))
    return h
```

```python
import functools

import jax
import jax.numpy as jnp
from jax import lax
from jax.experimental import pallas as pl
from jax.experimental.pallas import tpu as pltpu

F32 = jnp.float32
BF16 = jnp.bfloat16

SSD_HEADS = 16
SSD_HEAD_DIM = 64
SSD_GROUPS = 2
SSD_STATE = 128
SSD_CONV = 5
CHUNK = 128
SG_GROUPS = 8
SG_GROUP_DIM = 128
CONF_CONV = 31
EPS = 1e-6
LOG2E = 1.4426950408889634
N_MOD = 6
MOD_ROWS = 8
HALO = 16
LANES = 128
VMEM_LIMIT = 56 * 1024 * 1024
TOKEN_TILE = 512
SCAN_TILE = 1024


def _cparams(*sem):
    return pltpu.CompilerParams(dimension_semantics=sem, vmem_limit_bytes=VMEM_LIMIT)


def _dot(a, b):
    return jnp.dot(a, b, preferred_element_type=F32)


def _sigmoid(x):
    return 1.0 / (1.0 + jnp.exp(-x))


def _silu(x):
    return x * _sigmoid(x)


def _gelu_tanh(x):
    c = 0.7978845608028654
    return x * (0.5 + 0.5 * jnp.tanh(x * (c + (c * 0.044715) * (x * x))))


def _softplus(x):
    return jnp.maximum(x, 0.0) + jnp.log1p(jnp.exp(-jnp.abs(x)))


def _modulate(hf, g, shift, scale):
    ms = jnp.mean(hf * hf, axis=-1, keepdims=True)
    return (hf * lax.rsqrt(ms + EPS)) * (g * (1.0 + scale)) + shift


def _layernorm(x, g, b):
    mu = jnp.mean(x, axis=-1, keepdims=True)
    xc = x - mu
    var = jnp.mean(xc * xc, axis=-1, keepdims=True)
    return xc * lax.rsqrt(var + EPS) * g + b


def _mod_vec(mod_ref, k):
    d = mod_ref.shape[-1] // N_MOD
    return mod_ref[:, k * d:(k + 1) * d]


def _cumsum_rows(x, reverse):
    n = x.shape[0]
    row = lax.broadcasted_iota(jnp.int32, x.shape, 0)
    s = 1
    while s < n:
        if reverse:
            x = x + jnp.where(row < n - s, pltpu.roll(x, n - s, axis=0), 0.0)
        else:
            x = x + jnp.where(row >= s, pltpu.roll(x, s, axis=0), 0.0)
        s *= 2
    return x


def _expand_heads(x, e, terms):
    out = None
    r = x
    for _ in range(terms):
        p = r.astype(BF16)
        t = _dot(p, e)
        out = t if out is None else out + t
        r = r - p.astype(F32)
    return out


def _dwconv_rows(ext, w_ref, bias, start, taps, tm):
    n = ext.shape[0]
    acc = None
    for s in range(8):
        ks = [k for k in range(taps) if (start + k) % 8 == s]
        if not ks:
            continue
        rolled = ext if s == 0 else pltpu.roll(ext, n - s, axis=0)
        for k in ks:
            base = start + k - s
            term = w_ref[k:k + 1, :] * rolled[base:base + tm]
            acc = term if acc is None else acc + term
    return acc + bias


def _resident(shape):
    return pl.BlockSpec(shape, lambda *_: (0,) * len(shape), pipeline_mode=pl.Buffered(1))


def _ada_kernel(cs_ref, w_ref, b_ref, o_ref):
    s = _silu(cs_ref[...]).astype(BF16)
    o_ref[...] = _dot(s, w_ref[...].astype(BF16)) + b_ref[...]


def _ada_call(cs, ada_w, ada_b):
    depth, d, n = ada_w.shape
    tn = n // 4
    out = pl.pallas_call(
        _ada_kernel,
        out_shape=jax.ShapeDtypeStruct((depth, MOD_ROWS, n), F32),
        grid=(depth, n // tn),
        in_specs=[
            pl.BlockSpec((MOD_ROWS, d), lambda l, j: (0, 0)),
            pl.BlockSpec((None, d, tn), lambda l, j: (l, 0, j)),
            pl.BlockSpec((None, 1, tn), lambda l, j: (l, 0, j)),
        ],
        out_specs=pl.BlockSpec((None, MOD_ROWS, tn), lambda l, j: (l, 0, j)),
        compiler_params=_cparams("parallel", "parallel"),
        name="ada_mod",
    )(cs, ada_w, ada_b.reshape(depth, 1, n))
    return out.reshape(depth * MOD_ROWS, 1, n)


def _halo_specs(tm, n_tiles, d):
    per = tm // HALO
    last = n_tiles * per - 1
    prev = pl.BlockSpec((None, None, HALO, d),
                        lambda b, i: (b, jnp.maximum(i * per - 1, 0), 0, 0))
    nxt = pl.BlockSpec((None, None, HALO, d),
                       lambda b, i: (b, jnp.minimum((i + 1) * per, last), 0, 0))
    return prev, nxt


def _ext_rows(hp_ref, h_ref, hn_ref):
    return jnp.concatenate([hp_ref[...], h_ref[...], hn_ref[...]], axis=0)


def _ext_valid(tm):
    i = pl.program_id(1)
    n = pl.num_programs(1)
    row = lax.broadcasted_iota(jnp.int32, (tm + 2 * HALO, 1), 0)
    return jnp.logical_and(jnp.logical_or(row >= HALO, i > 0),
                           jnp.logical_or(row < HALO + tm, i < n - 1))


def _inproj_kernel(hp_ref, h_ref, hn_ref, mod_ref, g_ref, w_ref, cw_ref, cb_ref, dtb_ref,
                   lng_ref, lnb_ref, zs_ref, xbc_ref, dt_ref, gate_ref, v_ref):
    tm, d = h_ref.shape
    n_xbc = xbc_ref.shape[-1]
    o_xbc = d
    o_uv = o_xbc + n_xbc
    o_dt = o_uv + 2 * d
    ext = _ext_rows(hp_ref, h_ref, hn_ref)
    u = _modulate(ext, g_ref[...], _mod_vec(mod_ref, 0), _mod_vec(mod_ref, 1))
    u = jnp.where(_ext_valid(tm), u, 0.0).astype(BF16)
    um = u[HALO:HALO + tm]

    xe = _dot(u, w_ref[:, o_xbc:o_uv])
    acc = _dwconv_rows(xe, cw_ref, cb_ref[...], HALO - SSD_CONV // 2, SSD_CONV, tm)
    xbc_ref[...] = _silu(acc).astype(BF16)

    zs_ref[...] = _silu(_dot(um, w_ref[:, 0:d])).astype(BF16)

    lane = lax.broadcasted_iota(jnp.int32, (1, LANES), 1)
    dt = _softplus(_dot(um, w_ref[:, o_dt:o_dt + LANES]) + dtb_ref[...])
    dt_ref[...] = jnp.where(lane < 2 * SSD_HEADS, dt, 0.0)

    gate_ref[...] = _gelu_tanh(_dot(um, w_ref[:, o_uv:o_uv + d])).astype(BF16)
    v = _gelu_tanh(_dot(um, w_ref[:, o_uv + d:o_uv + 2 * d]))
    v_ref[...] = _layernorm(v, lng_ref[...], lnb_ref[...]).astype(BF16)


def _inproj_call(h, mod, mod_row, norm_g, w_in_r, conv_w, conv_b, dt_bias_p, ln_g, ln_b, tm):
    bsz, seq, d = h.shape
    n_tiles = seq // tm
    n_xbc = conv_w.shape[-1]
    hv = h.reshape(bsz, seq // HALO, HALO, d)
    prev, nxt = _halo_specs(tm, n_tiles, d)
    const = lambda shape: pl.BlockSpec(shape, lambda b, i: (0,) * len(shape))
    tile = lambda n: pl.BlockSpec((None, tm, n), lambda b, i: (b, i, 0))
    out_shape = (
        jax.ShapeDtypeStruct((bsz, seq, d), BF16),
        jax.ShapeDtypeStruct((bsz, seq, n_xbc), BF16),
        jax.ShapeDtypeStruct((bsz, seq, LANES), F32),
        jax.ShapeDtypeStruct((bsz, seq, d), BF16),
        jax.ShapeDtypeStruct((bsz, seq, d), BF16),
    )
    return pl.pallas_call(
        _inproj_kernel,
        out_shape=out_shape,
        grid=(bsz, n_tiles),
        in_specs=[
            prev, tile(d), nxt,
            pl.BlockSpec((None, 1, mod.shape[-1]), lambda b, i: (mod_row(b), 0, 0)),
            const((1, d)),
            _resident(w_in_r.shape),
            const(conv_w.shape), const((1, n_xbc)), const((1, LANES)),
            const((1, d)), const((1, d)),
        ],
        out_specs=(tile(d), tile(n_xbc), tile(LANES), tile(d), tile(d)),
        compiler_params=_cparams("parallel", "parallel"),
        name="hy_inproj",
    )(hv, h, hv, mod, norm_g.reshape(1, d), w_in_r, conv_w, conv_b.reshape(1, n_xbc),
      dt_bias_p, ln_g.reshape(1, d), ln_b.reshape(1, d))


def _ssd_state_kernel(xbc_ref, dt_ref, alog_ref, e_ref, h0_ref, hin_ref, hfin_ref, hs_ref,
                      xw_ref, *, reverse):
    n_chunks = hin_ref.shape[0]
    d_inner = e_ref.shape[-1]
    gw = d_inner // SSD_GROUPS
    i = pl.program_id(1)

    @pl.when(i == 0)
    def _():
        hs_ref[...] = h0_ref[...]

    a = -jnp.exp(alog_ref[...])
    e = e_ref[...]
    row8 = lax.broadcasted_iota(jnp.int32, (8, LANES), 0)
    ws = []
    edges = jnp.zeros((8, LANES), F32)
    for c in range(n_chunks):
        dt = dt_ref[c * CHUNK:(c + 1) * CHUNK, :]
        cs = _cumsum_rows(dt * a, reverse)
        edge = cs[0:1, :] if reverse else cs[CHUNK - 1:CHUNK, :]
        ws.append(jnp.exp(edge - cs) * dt)
        edges = jnp.where(row8 == c, edge, edges)
    w = _expand_heads(jnp.concatenate(ws, axis=0), e, 2)
    decay = _expand_heads(jnp.exp(edges), e, 3)
    xw_ref[...] = (xbc_ref[:, 0:d_inner].astype(F32) * w).astype(BF16)
    for t in range(n_chunks):
        c = n_chunks - 1 - t if reverse else t
        rows = slice(c * CHUNK, (c + 1) * CHUNK)
        for g in range(SSD_GROUPS):
            bm = xbc_ref[rows, d_inner + g * SSD_STATE:d_inner + (g + 1) * SSD_STATE]
            s = lax.dot_general(bm, xw_ref[rows, g * gw:(g + 1) * gw], (((0,), (0,)), ((), ())),
                                preferred_element_type=F32)
            hcur = hs_ref[g]
            hin_ref[c, g] = hcur.astype(BF16)
            hs_ref[g] = hcur * decay[c:c + 1, g * gw:(g + 1) * gw] + s

    @pl.when(i == pl.num_programs(1) - 1)
    def _():
        hfin_ref[...] = hs_ref[...]


def _ssd_state_call(xbc, dt, alog_p, e, h0, reverse, tq):
    bsz, seq, _ = xbc.shape
    d_inner = e.shape[-1]
    gw = d_inner // SSD_GROUPS
    n_blk = seq // tq
    cpb = tq // CHUNK
    blk = (lambda i: n_blk - 1 - i) if reverse else (lambda i: i)
    n_xb = d_inner + SSD_GROUPS * SSD_STATE
    return pl.pallas_call(
        functools.partial(_ssd_state_kernel, reverse=reverse),
        out_shape=(
            jax.ShapeDtypeStruct((bsz, seq // CHUNK, SSD_GROUPS, SSD_STATE, gw), BF16),
            jax.ShapeDtypeStruct((bsz, SSD_GROUPS, SSD_STATE, gw), F32),
        ),
        grid=(bsz, n_blk),
        in_specs=[
            pl.BlockSpec((None, tq, n_xb), lambda b, i: (b, blk(i), 0)),
            pl.BlockSpec((None, tq, LANES), lambda b, i: (b, blk(i), 0)),
            pl.BlockSpec((1, LANES), lambda b, i: (0, 0)),
            pl.BlockSpec(e.shape, lambda b, i: (0, 0)),
            pl.BlockSpec((None, SSD_GROUPS, SSD_STATE, gw), lambda b, i: (b, 0, 0, 0)),
        ],
        out_specs=(
            pl.BlockSpec((None, cpb, SSD_GROUPS, SSD_STATE, gw), lambda b, i: (b, blk(i), 0, 0, 0)),
            pl.BlockSpec((None, SSD_GROUPS, SSD_STATE, gw), lambda b, i: (b, 0, 0, 0)),
        ),
        scratch_shapes=[pltpu.VMEM((SSD_GROUPS, SSD_STATE, gw), F32),
                        pltpu.VMEM((tq, d_inner), BF16)],
        compiler_params=_cparams("parallel", "arbitrary"),
        name="ssd_state_bwd" if reverse else "ssd_state_fwd",
    )(xbc, dt, alog_p, e, h0)


def _mix_out_kernel(h_ref, xbc_ref, dt_ref, zs_ref, gate_ref, v_ref, hinf_ref, hinb_ref,
                    alog_ref, ef_ref, eb_ref, dskip_ref, ng_ref, sgw_ref, sgb_ref, wout_ref,
                    mod_ref, o_ref, ycat_ref, cs_ref, sf_ref, sb_ref):
    tq, d = h_ref.shape
    n_chunks = tq // CHUNK
    d_inner = ef_ref.shape[-1]
    gw = d_inner // SSD_GROUPS
    hpg = SSD_HEADS // SSD_GROUPS
    a2 = -jnp.exp(alog_ref[...]) * LOG2E
    lane = lax.broadcasted_iota(jnp.int32, (CHUNK, LANES), 1)
    ii = lax.broadcasted_iota(jnp.int32, (CHUNK, CHUNK), 0)
    jj = lax.broadcasted_iota(jnp.int32, (CHUNK, CHUNK), 1)
    low = ii >= jj
    diag = ii == jj
    lane_lo = lane < SSD_HEAD_DIM
    lane2 = lax.broadcasted_iota(jnp.int32, (CHUNK, 2 * LANES), 1)
    lane2_lo = lane2 < LANES

    for c in range(n_chunks):
        da = dt_ref[c * CHUNK:(c + 1) * CHUNK, :] * a2
        cs_ref[c * CHUNK:(c + 1) * CHUNK, :] = jnp.where(
            lane < SSD_HEADS, _cumsum_rows(da, False), _cumsum_rows(da, True))
    ecs = jnp.exp2(cs_ref[...])
    sf_ref[...] = _expand_heads(ecs, ef_ref[...], 2)
    sb_ref[...] = _expand_heads(ecs, eb_ref[...], 2)

    def chunk(c, carry):
        rows = pl.ds(pl.multiple_of(c * CHUNK, CHUNK), CHUNK)
        dt = dt_ref[rows, :]
        cs = cs_ref[rows, :]
        rw_t = (cs - jnp.log(dt) * LOG2E).T
        xs = xbc_ref[rows, 0:d_inner]
        cms, scs, dsel = [], [], jnp.zeros((CHUNK, LANES), F32)
        for g in range(SSD_GROUPS):
            bm = xbc_ref[rows, d_inner + g * SSD_STATE:d_inner + (g + 1) * SSD_STATE]
            o_c = d_inner + (SSD_GROUPS + g) * SSD_STATE
            cm = xbc_ref[rows, o_c:o_c + SSD_STATE]
            scores = lax.dot_general(cm, bm, (((1,), (1,)), ((), ())),
                                     preferred_element_type=F32)
            dg = jnp.sum(jnp.where(diag, scores, 0.0), axis=1, keepdims=True)
            lo = SSD_HEADS + g * hpg
            dsel = jnp.where(jnp.logical_and(lane >= lo, lane < lo + hpg), dg, dsel)
            cms.append(cm)
            scs.append(scores)
        skip = dskip_ref[...] + _expand_heads(dsel * dt, eb_ref[...], 1)
        for g in range(SSD_GROUPS):
            cm, scores = cms[g], scs[g]
            cols = slice(g * gw, (g + 1) * gw)
            y_g = (_dot(cm, hinf_ref[c, g]) * sf_ref[rows, cols]
                   + _dot(cm, hinb_ref[c, g]) * sb_ref[rows, cols])
            ys = []
            for pair in range(hpg // 2):
                ws = []
                for hh in range(2):
                    hd = g * hpg + pair * 2 + hh
                    hb = SSD_HEADS + hd
                    seg = jnp.where(low, cs[:, hd:hd + 1] - rw_t[hd:hd + 1, :],
                                    cs[:, hb:hb + 1] - rw_t[hb:hb + 1, :])
                    ws.append((scores * jnp.exp2(seg)).astype(BF16))
                p0 = (g * hpg + pair * 2) * SSD_HEAD_DIM
                xp = xs[:, p0:p0 + 2 * SSD_HEAD_DIM]
                zero = jnp.zeros_like(xp)
                rhs = jnp.concatenate([jnp.where(lane_lo, xp, zero),
                                       jnp.where(lane_lo, zero, xp)], axis=0)
                ys.append(_dot(jnp.concatenate(ws, axis=1), rhs))
            y_g = y_g + jnp.concatenate(ys, axis=1)
            y_g = y_g + skip[:, cols] * xs[:, cols].astype(F32)
            y_g = y_g * zs_ref[rows, cols].astype(F32)
            ms = jnp.mean(y_g * y_g, axis=-1, keepdims=True)
            y_g = y_g * lax.rsqrt(ms + EPS) * ng_ref[:, cols]
            ycat_ref[rows, cols] = y_g.astype(BF16)
        for sp in range(SG_GROUPS // 2):
            cols = slice(sp * 2 * SG_GROUP_DIM, (sp + 1) * 2 * SG_GROUP_DIM)
            vp = v_ref[rows, cols]
            zero = jnp.zeros_like(vp)
            rhs = jnp.concatenate([jnp.where(lane2_lo, vp, zero),
                                   jnp.where(lane2_lo, zero, vp)], axis=0)
            s = _dot(sgw_ref[sp], rhs) + sgb_ref[:, cols]
            ycat_ref[rows, d_inner + sp * 2 * SG_GROUP_DIM:d_inner + (sp + 1) * 2 * SG_GROUP_DIM] = (
                gate_ref[rows, cols].astype(F32) * s).astype(BF16)
        return carry

    lax.fori_loop(0, n_chunks, chunk, 0)
    out = _dot(ycat_ref[...], wout_ref[...])
    o_ref[...] = h_ref[...] + _mod_vec(mod_ref, 2) * out


def _mix_out_call(h, xbc, dt, zs, gate, v, hin_f, hin_b, alog_p, e_f, e_b, dskip, norm_g,
                  sg_w, sg_b_exp, w_out, mod, mod_row, tq):
    bsz, seq, d = h.shape
    d_inner = e_f.shape[-1]
    gw = d_inner // SSD_GROUPS
    cpb = tq // CHUNK
    const = lambda shape: pl.BlockSpec(shape, lambda b, i: (0,) * len(shape))
    tile = lambda n: pl.BlockSpec((None, tq, n), lambda b, i: (b, i, 0))
    hin_spec = pl.BlockSpec((None, cpb, SSD_GROUPS, SSD_STATE, gw), lambda b, i: (b, i, 0, 0, 0))
    return pl.pallas_call(
        _mix_out_kernel,
        out_shape=jax.ShapeDtypeStruct((bsz, seq, d), F32),
        grid=(bsz, seq // tq),
        in_specs=[
            tile(d), tile(xbc.shape[-1]), tile(LANES), tile(d_inner), tile(d), tile(d),
            hin_spec, hin_spec,
            const((1, LANES)), const(e_f.shape), const(e_b.shape),
            const((1, d_inner)), const((1, d_inner)),
            const(sg_w.shape), const(sg_b_exp.shape), _resident(w_out.shape),
            pl.BlockSpec((None, 1, mod.shape[-1]), lambda b, i: (mod_row(b), 0, 0)),
        ],
        out_specs=tile(d),
        scratch_shapes=[pltpu.VMEM((tq, w_out.shape[0]), BF16),
                        pltpu.VMEM((tq, LANES), F32),
                        pltpu.VMEM((tq, d_inner), F32),
                        pltpu.VMEM((tq, d_inner), F32)],
        compiler_params=_cparams("parallel", "parallel"),
        name="hy_mix_out",
    )(h, xbc, dt, zs, gate, v, hin_f, hin_b, alog_p, e_f, e_b, dskip, norm_g.reshape(1, d_inner),
      sg_w, sg_b_exp, w_out, mod)


def _conformer_kernel(hp_ref, h_ref, hn_ref, mod_ref, g_ref, w1_ref, b1_ref, wd_ref, bd_ref,
                      lng_ref, lnb_ref, w2_ref, b2_ref, o_ref):
    tm, d = h_ref.shape
    cw = w2_ref.shape[0]
    ext = _ext_rows(hp_ref, h_ref, hn_ref)
    u = _modulate(ext, g_ref[...], _mod_vec(mod_ref, 0), _mod_vec(mod_ref, 1)).astype(BF16)
    a = (_dot(u, w1_ref[:, 0:cw]) + b1_ref[:, 0:cw]) * _sigmoid(
        _dot(u, w1_ref[:, cw:2 * cw]) + b1_ref[:, cw:2 * cw])
    a = jnp.where(_ext_valid(tm), a, 0.0)
    acc = _dwconv_rows(a, wd_ref, bd_ref[...], HALO - CONF_CONV // 2, CONF_CONV, tm)
    y = _silu(_layernorm(acc, lng_ref[...], lnb_ref[...])).astype(BF16)
    out = _dot(y, w2_ref[...]) + b2_ref[...]
    o_ref[...] = h_ref[...] + _mod_vec(mod_ref, 2) * out


def _conformer_call(h, mod, mod_row, norm_g, w1, b1, wd, bd, ln_g, ln_b, w2, b2, tm):
    bsz, seq, d = h.shape
    n_tiles = seq // tm
    cw = w2.shape[0]
    hv = h.reshape(bsz, seq // HALO, HALO, d)
    prev, nxt = _halo_specs(tm, n_tiles, d)
    const = lambda shape: pl.BlockSpec(shape, lambda b, i: (0,) * len(shape))
    tile = pl.BlockSpec((None, tm, d), lambda b, i: (b, i, 0))
    return pl.pallas_call(
        _conformer_kernel,
        out_shape=jax.ShapeDtypeStruct((bsz, seq, d), F32),
        grid=(bsz, n_tiles),
        in_specs=[
            prev, tile, nxt,
            pl.BlockSpec((None, 1, mod.shape[-1]), lambda b, i: (mod_row(b), 0, 0)),
            const((1, d)), _resident(w1.shape), const((1, 2 * cw)), const(wd.shape),
            const((1, cw)), const((1, cw)), const((1, cw)), _resident(w2.shape), const((1, d)),
        ],
        out_specs=tile,
        compiler_params=_cparams("parallel", "parallel"),
        name="conformer",
    )(hv, h, hv, mod, norm_g.reshape(1, d), w1, b1.reshape(1, 2 * cw), wd, bd.reshape(1, cw),
      ln_g.reshape(1, cw), ln_b.reshape(1, cw), w2, b2.reshape(1, d))


def _mlp_kernel(h_ref, mod_ref, g_ref, w1_ref, w2_ref, fg_ref, o_ref, a_ref, *, final_norm):
    hf = h_ref[...]
    u = _modulate(hf, g_ref[...], _mod_vec(mod_ref, 3), _mod_vec(mod_ref, 4)).astype(BF16)
    hid = w1_ref.shape[1]
    step = min(hid, 1024)
    for k in range(0, hid, step):
        r = jnp.maximum(_dot(u, w1_ref[:, k:k + step]), 0.0)
        a_ref[:, k:k + step] = (r * r).astype(BF16)
    out = hf + _mod_vec(mod_ref, 5) * _dot(a_ref[...], w2_ref[...])
    if final_norm:
        ms = jnp.mean(out * out, axis=-1, keepdims=True)
        out = out * lax.rsqrt(ms + EPS) * fg_ref[...]
    o_ref[...] = out


def _mlp_call(h, mod, mod_row, norm_g, w1, w2, final_g, final_norm, tm):
    bsz, seq, d = h.shape
    const = lambda shape: pl.BlockSpec(shape, lambda b, i: (0,) * len(shape))
    tile = pl.BlockSpec((None, tm, d), lambda b, i: (b, i, 0))
    return pl.pallas_call(
        functools.partial(_mlp_kernel, final_norm=final_norm),
        out_shape=jax.ShapeDtypeStruct((bsz, seq, d), F32),
        grid=(bsz, seq // tm),
        in_specs=[
            tile,
            pl.BlockSpec((None, 1, mod.shape[-1]), lambda b, i: (mod_row(b), 0, 0)),
            const((1, d)), _resident(w1.shape), _resident(w2.shape), const((1, d)),
        ],
        out_specs=tile,
        scratch_shapes=[pltpu.VMEM((tm, w1.shape[1]), BF16)],
        compiler_params=_cparams("parallel", "parallel"),
        name="mlp_final" if final_norm else "mlp",
    )(h, mod, norm_g.reshape(1, d), w1, w2, final_g.reshape(1, d))


def _tile(seq, want):
    return min(seq, want)


def _hybrid_mixer(h, mod, mod_row, norm_g, p, h0_f, h0_b, need_out):
    seq = h.shape[1]
    zs, xbc, dt, gate, v = _inproj_call(
        h, mod, mod_row, norm_g, p["w_in"], p["conv_w"], p["conv_b"], p["dt_bias"],
        p["ln_g"], p["ln_b"], _tile(seq, TOKEN_TILE))
    ts = _tile(seq, SCAN_TILE)
    tq = _tile(seq, TOKEN_TILE)
    hin_f, hfin_f = _ssd_state_call(xbc, dt, p["alog"], p["e_f"], h0_f, False, ts)
    hin_b, hfin_b = _ssd_state_call(xbc, dt, p["alog"], p["e_b"], h0_b, True, ts)
    out = None
    if need_out:
        out = _mix_out_call(h, xbc, dt, zs, gate, v, hin_f, hin_b, p["alog"], p["e_f"], p["e_b"],
                            p["dskip"], p["norm_g"], p["sg_w"], p["sg_b"], p["w_out"], mod,
                            mod_row, tq)
    return out, hfin_f, hfin_b


def _hybrid_params(j, hy_w_in, ssd_conv_w, ssd_conv_b, ssd_dt_bias, ssd_a_log, ssd_d, ssd_norm_g,
                   sg_ln_g, sg_ln_b, sg_w, sg_b, hy_w_out):
    d_inner = SSD_HEADS * SSD_HEAD_DIM
    n_xbc = d_inner + 2 * SSD_GROUPS * SSD_STATE
    w = hy_w_in[j]
    o2 = d_inner + n_xbc
    o3 = o2 + 2 * SSD_HEADS
    pad = jnp.zeros((w.shape[0], LANES - 2 * SSD_HEADS), w.dtype)
    w_in_r = jnp.concatenate([w[:, :o2], w[:, o3:], w[:, o2:o3], pad], axis=1).astype(BF16)
    lane_pad = lambda t: jnp.pad(t.reshape(1, -1), ((0, 0), (0, LANES - t.size)))
    head = jnp.arange(d_inner, dtype=jnp.int32) // SSD_HEAD_DIM
    row = jnp.arange(LANES, dtype=jnp.int32)[:, None]
    sg_width = SG_GROUPS * SG_GROUP_DIM
    sg_b_exp = jnp.broadcast_to(sg_b[j].T[:, :, None],
                                (CHUNK, SG_GROUPS, SG_GROUP_DIM)).reshape(CHUNK, sg_width)
    sg_pairs = sg_w[j].reshape(SG_GROUPS // 2, 2, CHUNK, CHUNK).transpose(0, 2, 1, 3).reshape(
        SG_GROUPS // 2, CHUNK, 2 * CHUNK)
    return dict(
        w_in=w_in_r, conv_w=ssd_conv_w[j], conv_b=ssd_conv_b[j],
        dt_bias=lane_pad(ssd_dt_bias[j]), alog=lane_pad(ssd_a_log[j]),
        e_f=(row == head[None, :]).astype(BF16),
        e_b=(row == head[None, :] + SSD_HEADS).astype(BF16),
        dskip=jnp.repeat(ssd_d[j], SSD_HEAD_DIM).reshape(1, d_inner),
        norm_g=ssd_norm_g[j], ln_g=sg_ln_g[j], ln_b=sg_ln_b[j],
        sg_w=sg_pairs.astype(BF16), sg_b=sg_b_exp, w_out=hy_w_out[j].astype(BF16))


def kernel(x, c, ctx, c_ctx, ada_w, ada_b, norm_mix_g, norm_mlp_g, mlp_w1, mlp_w2, hy_w_in,
           ssd_conv_w, ssd_conv_b, ssd_dt_bias, ssd_a_log, ssd_d, ssd_norm_g, sg_ln_g, sg_ln_b,
           sg_w, sg_b, hy_w_out, cf_w_pw1, cf_b_pw1, cf_w_dw, cf_b_dw, cf_ln_g, cf_ln_b, cf_w_pw2,
           cf_b_pw2, final_norm_g):
    bsz, seq, d = x.shape
    depth = ada_w.shape[0]
    assert bsz < MOD_ROWS and seq % 512 == 0 and ctx.shape[1] % (2 * CHUNK) == 0
    cs = jnp.concatenate([c, c_ctx[None, :], jnp.zeros((MOD_ROWS - bsz - 1, d), c.dtype)], axis=0)
    mod = _ada_call(cs, ada_w, ada_b)

    gw = SSD_HEADS * SSD_HEAD_DIM // SSD_GROUPS
    zeros = jnp.zeros((bsz, SSD_GROUPS, SSD_STATE, gw), F32)
    h, hc = x, ctx
    for i in range(depth):
        ctx_needed = i < depth - 1
        last = i == depth - 1
        lat_row = lambda b, i=i: i * MOD_ROWS + b
        ctx_row = lambda b, i=i: i * MOD_ROWS + bsz
        j = i // 2
        if i % 2 == 0:
            p = _hybrid_params(j, hy_w_in, ssd_conv_w, ssd_conv_b, ssd_dt_bias, ssd_a_log, ssd_d,
                               ssd_norm_g, sg_ln_g, sg_ln_b, sg_w, sg_b, hy_w_out)
            hc_new, h0_f, h0_b = _hybrid_mixer(hc, mod, ctx_row, norm_mix_g[i], p, zeros, zeros,
                                               ctx_needed)
            h, _, _ = _hybrid_mixer(h, mod, lat_row, norm_mix_g[i], p, h0_f, h0_b, True)
            if ctx_needed:
                hc = hc_new
        else:
            w1 = cf_w_pw1[j].astype(BF16)
            w2 = cf_w_pw2[j].astype(BF16)
            args = (w1, cf_b_pw1[j], cf_w_dw[j], cf_b_dw[j], cf_ln_g[j], cf_ln_b[j], w2,
                    cf_b_pw2[j])
            h = _conformer_call(h, mod, lat_row, norm_mix_g[i], *args, _tile(seq, TOKEN_TILE))
            if ctx_needed:
                hc = _conformer_call(hc, mod, ctx_row, norm_mix_g[i], *args,
                                     _tile(hc.shape[1], TOKEN_TILE))
        w1 = mlp_w1[i].astype(BF16)
        w2 = mlp_w2[i].astype(BF16)
        h = _mlp_call(h, mod, lat_row, norm_mlp_g[i], w1, w2, final_norm_g, last,
                      _tile(seq, TOKEN_TILE))
        if ctx_needed:
            hc = _mlp_call(hc, mod, ctx_row, norm_mlp_g[i], w1, w2, final_norm_g, False,
                           _tile(hc.shape[1], TOKEN_TILE))
    return h
```

```python
import functools

import jax
import jax.numpy as jnp
from jax import lax
from jax.experimental import pallas as pl
from jax.experimental.pallas import tpu as pltpu

F32 = jnp.float32
BF16 = jnp.bfloat16

SSD_HEADS = 16
SSD_HEAD_DIM = 64
SSD_GROUPS = 2
SSD_STATE = 128
SSD_CONV = 5
CHUNK = 128
SG_GROUPS = 8
SG_GROUP_DIM = 128
CONF_CONV = 31
EPS = 1e-6
LOG2E = 1.4426950408889634
N_MOD = 6
MOD_ROWS = 8
HALO = 16
LANES = 128
VMEM_LIMIT = 60 * 1024 * 1024
TOKEN_TILE = 512
SCAN_TILE = 1024
MLP_BLOCK = 512

def _cparams(*sem, flags=None):
    return pltpu.CompilerParams(dimension_semantics=sem, vmem_limit_bytes=VMEM_LIMIT, flags=flags)


def _dot(a, b):
    return jnp.dot(a, b, preferred_element_type=F32)


def _sigmoid(x):
    return 1.0 / (1.0 + jnp.exp(-x))


def _silu(x):
    return x * _sigmoid(x)


def _gelu_tanh(x):
    c = 0.7978845608028654
    return x * (0.5 + 0.5 * jnp.tanh(x * (c + (c * 0.044715) * (x * x))))


def _softplus(x):
    return jnp.maximum(x, 0.0) + jnp.log1p(jnp.exp(-jnp.abs(x)))


def _modulate(hf, g, shift, scale):
    ms = jnp.mean(hf * hf, axis=-1, keepdims=True)
    return (hf * lax.rsqrt(ms + EPS)) * (g * (1.0 + scale)) + shift


def _layernorm(x, g, b):
    mu = jnp.mean(x, axis=-1, keepdims=True)
    xc = x - mu
    var = jnp.mean(xc * xc, axis=-1, keepdims=True)
    return xc * lax.rsqrt(var + EPS) * g + b


def _mod_vec(mod_ref, k):
    d = mod_ref.shape[-1] // N_MOD
    return mod_ref[:, k * d:(k + 1) * d]


def _cumsum_rows(x, reverse):
    n = x.shape[0]
    row = lax.broadcasted_iota(jnp.int32, x.shape, 0)
    s = 1
    while s < n:
        if reverse:
            x = x + jnp.where(row < n - s, pltpu.roll(x, n - s, axis=0), 0.0)
        else:
            x = x + jnp.where(row >= s, pltpu.roll(x, s, axis=0), 0.0)
        s *= 2
    return x


def _expand_heads(x, e, terms):
    out = None
    r = x
    for _ in range(terms):
        p = r.astype(BF16)
        t = _dot(p, e)
        out = t if out is None else out + t
        r = r - p.astype(F32)
    return out


def _dwconv_block(a_ref, w_ref, b_ref, lt, rb, start, taps):
    span = CHUNK + 2 * HALO
    assert start + taps - 1 + CHUNK <= span
    r0 = rb * CHUNK if isinstance(rb, int) else pl.multiple_of(rb * CHUNK, CHUNK)
    blk = a_ref[lt, pl.ds(r0, span), :]
    acc = b_ref[lt]
    for s in range(8):
        ks = [k for k in range(taps) if (start + k) % 8 == s]
        if not ks:
            continue
        rolled = blk if s == 0 else pltpu.roll(blk, span - s, axis=0)
        for k in ks:
            base = start + k - s
            acc = acc + w_ref[lt, pl.ds(k, 1), :] * rolled[base:base + CHUNK]
    return acc


def _lane_tile_major(w):
    rows, c = w.shape
    w = jnp.pad(w, ((0, -rows % 8), (0, 0)))
    return w.reshape(w.shape[0], c // LANES, LANES).transpose(1, 0, 2)


def _resident(shape):
    return pl.BlockSpec(shape, lambda *_: (0,) * len(shape), pipeline_mode=pl.Buffered(1))


def _ada_kernel(cs_ref, w_ref, b_ref, o_ref):
    s = _silu(cs_ref[...]).astype(BF16)
    o_ref[...] = _dot(s, w_ref[...].astype(BF16)) + b_ref[...]


def _ada_call(cs, ada_w, ada_b):
    depth, d, n = ada_w.shape
    tn = n // 4
    out = pl.pallas_call(
        _ada_kernel,
        out_shape=jax.ShapeDtypeStruct((depth, MOD_ROWS, n), F32),
        grid=(depth, n // tn),
        in_specs=[
            pl.BlockSpec((MOD_ROWS, d), lambda l, j: (0, 0)),
            pl.BlockSpec((None, d, tn), lambda l, j: (l, 0, j)),
            pl.BlockSpec((None, 1, tn), lambda l, j: (l, 0, j)),
        ],
        out_specs=pl.BlockSpec((None, MOD_ROWS, tn), lambda l, j: (l, 0, j)),
        compiler_params=_cparams("parallel", "parallel"),
        name="ada_mod",
    )(cs, ada_w, ada_b.reshape(depth, 1, n))
    return out.reshape(depth * MOD_ROWS, 1, n)


def _halo_specs(tm, n_tiles, d):
    per = tm // HALO
    last = n_tiles * per - 1
    prev = pl.BlockSpec((None, None, HALO, d),
                        lambda b, i: (b, jnp.maximum(i * per - 1, 0), 0, 0))
    nxt = pl.BlockSpec((None, None, HALO, d),
                       lambda b, i: (b, jnp.minimum((i + 1) * per, last), 0, 0))
    return prev, nxt


def _ext_rows(hp_ref, h_ref, hn_ref):
    return jnp.concatenate([hp_ref[...], h_ref[...], hn_ref[...]], axis=0)


def _ext_valid(tm, i, n):
    row = lax.broadcasted_iota(jnp.int32, (tm + 2 * HALO, 1), 0)
    return jnp.logical_and(jnp.logical_or(row >= HALO, i > 0),
                           jnp.logical_or(row < HALO + tm, i < n - 1))


def _inproj_kernel(hp_ref, h_ref, hn_ref, mod_ref, g_ref, w_ref, cw_ref, cb_ref, dtb_ref,
                   lng_ref, lnb_ref, zs_ref, xbc_ref, dt_ref, gate_ref, v_ref, xe_ref):
    tm, d = h_ref.shape
    n_xbc = xbc_ref.shape[-1]
    o_xbc = d
    o_uv = o_xbc + n_xbc
    o_dt = o_uv + 2 * d
    ext = _ext_rows(hp_ref, h_ref, hn_ref)
    u = _modulate(ext, g_ref[...], _mod_vec(mod_ref, 0), _mod_vec(mod_ref, 1))
    u = jnp.where(_ext_valid(tm, pl.program_id(1), pl.num_programs(1)), u, 0.0).astype(BF16)
    um = u[HALO:HALO + tm]

    n_lt = n_xbc // LANES
    xe = _dot(u, w_ref[:, o_xbc:o_uv])
    for lt in range(n_lt):
        xe_ref[lt] = xe[:, lt * LANES:(lt + 1) * LANES]

    def conv(first, last):
        for lt in range(first, last):
            for rb in range(tm // CHUNK):
                acc = _dwconv_block(xe_ref, cw_ref, cb_ref, lt, rb, HALO - SSD_CONV // 2, SSD_CONV)
                xbc_ref[rb * CHUNK:(rb + 1) * CHUNK, lt * LANES:(lt + 1) * LANES] = (
                    _silu(acc).astype(BF16))

    zs_ref[...] = _silu(_dot(um, w_ref[:, 0:d])).astype(BF16)
    conv(0, n_lt // 3)

    lane = lax.broadcasted_iota(jnp.int32, (1, LANES), 1)
    dt = _softplus(_dot(um, w_ref[:, o_dt:o_dt + LANES]) + dtb_ref[...])
    dt_ref[...] = jnp.where(lane < 2 * SSD_HEADS, dt, 0.0)

    gate_ref[...] = _gelu_tanh(_dot(um, w_ref[:, o_uv:o_uv + d])).astype(BF16)
    conv(n_lt // 3, 2 * n_lt // 3)
    v = _gelu_tanh(_dot(um, w_ref[:, o_uv + d:o_uv + 2 * d]))
    v_ref[...] = _layernorm(v, lng_ref[...], lnb_ref[...]).astype(BF16)
    conv(2 * n_lt // 3, n_lt)


def _inproj_call(h, mod, mod_row, norm_g, w_in_r, conv_w, conv_b, dt_bias_p, ln_g, ln_b, tm):
    bsz, seq, d = h.shape
    n_tiles = seq // tm
    n_xbc = conv_w.shape[-1]
    n_lt = n_xbc // LANES
    cw3 = _lane_tile_major(conv_w)
    hv = h.reshape(bsz, seq // HALO, HALO, d)
    prev, nxt = _halo_specs(tm, n_tiles, d)
    const = lambda shape: pl.BlockSpec(shape, lambda b, i: (0,) * len(shape))
    tile = lambda n: pl.BlockSpec((None, tm, n), lambda b, i: (b, i, 0))
    out_shape = (
        jax.ShapeDtypeStruct((bsz, seq, d), BF16),
        jax.ShapeDtypeStruct((bsz, seq, n_xbc), BF16),
        jax.ShapeDtypeStruct((bsz, seq, LANES), F32),
        jax.ShapeDtypeStruct((bsz, seq, d), BF16),
        jax.ShapeDtypeStruct((bsz, seq, d), BF16),
    )
    return pl.pallas_call(
        _inproj_kernel,
        out_shape=out_shape,
        grid=(bsz, n_tiles),
        in_specs=[
            prev, tile(d), nxt,
            pl.BlockSpec((None, 1, mod.shape[-1]), lambda b, i: (mod_row(b), 0, 0)),
            const((1, d)),
            _resident(w_in_r.shape),
            const(cw3.shape), const((n_lt, 1, LANES)), const((1, LANES)),
            const((1, d)), const((1, d)),
        ],
        out_specs=(tile(d), tile(n_xbc), tile(LANES), tile(d), tile(d)),
        scratch_shapes=[pltpu.VMEM((n_lt, tm + 2 * HALO, LANES), F32)],
        compiler_params=_cparams("parallel", "parallel"),
        name="hy_inproj",
    )(hv, h, hv, mod, norm_g.reshape(1, d), w_in_r, cw3, conv_b.reshape(n_lt, 1, LANES),
      dt_bias_p, ln_g.reshape(1, d), ln_b.reshape(1, d))


def _ssd_state_kernel(xbc_ref, dt_ref, alog_ref, e_ref, h0_ref, hin_ref, hfin_ref, hs_ref,
                      xw_ref, *, reverse):
    n_chunks = hin_ref.shape[0]
    d_inner = e_ref.shape[-1]
    gw = d_inner // SSD_GROUPS
    i = pl.program_id(1)

    @pl.when(i == 0)
    def _():
        hs_ref[...] = h0_ref[...]

    a = -jnp.exp(alog_ref[...])
    e = e_ref[...]
    row8 = lax.broadcasted_iota(jnp.int32, (8, LANES), 0)
    ws = []
    edges = jnp.zeros((8, LANES), F32)
    for c in range(n_chunks):
        dt = dt_ref[c * CHUNK:(c + 1) * CHUNK, :]
        cs = _cumsum_rows(dt * a, reverse)
        edge = cs[0:1, :] if reverse else cs[CHUNK - 1:CHUNK, :]
        ws.append(jnp.exp(edge - cs) * dt)
        edges = jnp.where(row8 == c, edge, edges)
    w = _expand_heads(jnp.concatenate(ws, axis=0), e, 2)
    decay = _expand_heads(jnp.exp(edges), e, 3)
    xw_ref[...] = (xbc_ref[:, 0:d_inner].astype(F32) * w).astype(BF16)
    for t in range(n_chunks):
        c = n_chunks - 1 - t if reverse else t
        rows = slice(c * CHUNK, (c + 1) * CHUNK)
        for g in range(SSD_GROUPS):
            bm = xbc_ref[rows, d_inner + g * SSD_STATE:d_inner + (g + 1) * SSD_STATE]
            s = lax.dot_general(bm, xw_ref[rows, g * gw:(g + 1) * gw], (((0,), (0,)), ((), ())),
                                preferred_element_type=F32)
            hcur = hs_ref[g]
            hin_ref[c, g] = hcur.astype(BF16)
            hs_ref[g] = hcur * decay[c:c + 1, g * gw:(g + 1) * gw] + s

    @pl.when(i == pl.num_programs(1) - 1)
    def _():
        hfin_ref[...] = hs_ref[...]


def _ssd_state_call(xbc, dt, alog_p, e, h0, reverse, tq):
    bsz, seq, _ = xbc.shape
    d_inner = e.shape[-1]
    gw = d_inner // SSD_GROUPS
    n_blk = seq // tq
    cpb = tq // CHUNK
    blk = (lambda i: n_blk - 1 - i) if reverse else (lambda i: i)
    n_xb = d_inner + SSD_GROUPS * SSD_STATE
    return pl.pallas_call(
        functools.partial(_ssd_state_kernel, reverse=reverse),
        out_shape=(
            jax.ShapeDtypeStruct((bsz, seq // CHUNK, SSD_GROUPS, SSD_STATE, gw), BF16),
            jax.ShapeDtypeStruct((bsz, SSD_GROUPS, SSD_STATE, gw), F32),
        ),
        grid=(bsz, n_blk),
        in_specs=[
            pl.BlockSpec((None, tq, n_xb), lambda b, i: (b, blk(i), 0)),
            pl.BlockSpec((None, tq, LANES), lambda b, i: (b, blk(i), 0)),
            pl.BlockSpec((1, LANES), lambda b, i: (0, 0)),
            pl.BlockSpec(e.shape, lambda b, i: (0, 0)),
            pl.BlockSpec((None, SSD_GROUPS, SSD_STATE, gw), lambda b, i: (b, 0, 0, 0)),
        ],
        out_specs=(
            pl.BlockSpec((None, cpb, SSD_GROUPS, SSD_STATE, gw), lambda b, i: (b, blk(i), 0, 0, 0)),
            pl.BlockSpec((None, SSD_GROUPS, SSD_STATE, gw), lambda b, i: (b, 0, 0, 0)),
        ),
        scratch_shapes=[pltpu.VMEM((SSD_GROUPS, SSD_STATE, gw), F32),
                        pltpu.VMEM((tq, d_inner), BF16)],
        compiler_params=_cparams("parallel", "arbitrary"),
        name="ssd_state_bwd" if reverse else "ssd_state_fwd",
    )(xbc, dt, alog_p, e, h0)


def _mix_tile(xbc_ref, dt_ref, zs_ref, gate_ref, v_ref, hinf_ref, hinb_ref, alog_ref, ef_ref,
              eb_ref, dskip_ref, ng_ref, sgw_ref, sgb_ref, wout_ref, ycat_ref, cs_ref, sf_ref,
              sb_ref, per_chunk=None):
    tq = dt_ref.shape[0]
    n_chunks = tq // CHUNK
    d_inner = ef_ref.shape[-1]
    gw = d_inner // SSD_GROUPS
    hpg = SSD_HEADS // SSD_GROUPS
    a2 = -jnp.exp(alog_ref[...]) * LOG2E
    lane = lax.broadcasted_iota(jnp.int32, (CHUNK, LANES), 1)
    ii = lax.broadcasted_iota(jnp.int32, (CHUNK, CHUNK), 0)
    jj = lax.broadcasted_iota(jnp.int32, (CHUNK, CHUNK), 1)
    low = ii >= jj
    diag = ii == jj
    lane_lo = lane < SSD_HEAD_DIM
    lane2 = lax.broadcasted_iota(jnp.int32, (CHUNK, 2 * LANES), 1)
    lane2_lo = lane2 < LANES

    for c in range(n_chunks):
        da = dt_ref[c * CHUNK:(c + 1) * CHUNK, :] * a2
        cs_ref[c * CHUNK:(c + 1) * CHUNK, :] = jnp.where(
            lane < SSD_HEADS, _cumsum_rows(da, False), _cumsum_rows(da, True))
    ecs = jnp.exp2(cs_ref[...])
    sf_ref[...] = _expand_heads(ecs, ef_ref[...], 2)
    sb_ref[...] = _expand_heads(ecs, eb_ref[...], 2)

    def chunk(c, carry):
        if per_chunk is not None:
            per_chunk(c)
        rows = pl.ds(pl.multiple_of(c * CHUNK, CHUNK), CHUNK)
        dt = dt_ref[rows, :]
        cs = cs_ref[rows, :]
        rw_t = (cs - jnp.log(dt) * LOG2E).T
        xs = xbc_ref[rows, 0:d_inner]
        cms, scs, dsel = [], [], jnp.zeros((CHUNK, LANES), F32)
        for g in range(SSD_GROUPS):
            bm = xbc_ref[rows, d_inner + g * SSD_STATE:d_inner + (g + 1) * SSD_STATE]
            o_c = d_inner + (SSD_GROUPS + g) * SSD_STATE
            cm = xbc_ref[rows, o_c:o_c + SSD_STATE]
            scores = lax.dot_general(cm, bm, (((1,), (1,)), ((), ())),
                                     preferred_element_type=F32)
            dg = jnp.sum(jnp.where(diag, scores, 0.0), axis=1, keepdims=True)
            lo = SSD_HEADS + g * hpg
            dsel = jnp.where(jnp.logical_and(lane >= lo, lane < lo + hpg), dg, dsel)
            cms.append(cm)
            scs.append(scores)
        skip = dskip_ref[...] + _expand_heads(dsel * dt, eb_ref[...], 1)
        for g in range(SSD_GROUPS):
            cm, scores = cms[g], scs[g]
            cols = slice(g * gw, (g + 1) * gw)
            y_g = (_dot(cm, hinf_ref[c, g]) * sf_ref[rows, cols]
                   + _dot(cm, hinb_ref[c, g]) * sb_ref[rows, cols])
            ys = []
            for pair in range(hpg // 2):
                ws = []
                for hh in range(2):
                    hd = g * hpg + pair * 2 + hh
                    hb = SSD_HEADS + hd
                    seg = jnp.where(low, cs[:, hd:hd + 1] - rw_t[hd:hd + 1, :],
                                    cs[:, hb:hb + 1] - rw_t[hb:hb + 1, :])
                    ws.append((scores * jnp.exp2(seg)).astype(BF16))
                p0 = (g * hpg + pair * 2) * SSD_HEAD_DIM
                xp = xs[:, p0:p0 + 2 * SSD_HEAD_DIM]
                zero = jnp.zeros_like(xp)
                rhs = jnp.concatenate([jnp.where(lane_lo, xp, zero),
                                       jnp.where(lane_lo, zero, xp)], axis=0)
                ys.append(_dot(jnp.concatenate(ws, axis=1), rhs))
            y_g = y_g + jnp.concatenate(ys, axis=1)
            y_g = y_g + skip[:, cols] * xs[:, cols].astype(F32)
            y_g = y_g * zs_ref[rows, cols].astype(F32)
            ms = jnp.mean(y_g * y_g, axis=-1, keepdims=True)
            y_g = y_g * lax.rsqrt(ms + EPS) * ng_ref[:, cols]
            ycat_ref[rows, cols] = y_g.astype(BF16)
        for sp in range(SG_GROUPS // 2):
            cols = slice(sp * 2 * SG_GROUP_DIM, (sp + 1) * 2 * SG_GROUP_DIM)
            vp = v_ref[rows, cols]
            zero = jnp.zeros_like(vp)
            rhs = jnp.concatenate([jnp.where(lane2_lo, vp, zero),
                                   jnp.where(lane2_lo, zero, vp)], axis=0)
            s = _dot(sgw_ref[sp], rhs) + sgb_ref[:, cols]
            ycat_ref[rows, d_inner + sp * 2 * SG_GROUP_DIM:d_inner + (sp + 1) * 2 * SG_GROUP_DIM] = (
                gate_ref[rows, cols].astype(F32) * s).astype(BF16)
        return carry

    lax.fori_loop(0, n_chunks, chunk, 0)
    return _dot(ycat_ref[...], wout_ref[...])


def _mix_out_kernel(h_ref, xbc_ref, dt_ref, zs_ref, gate_ref, v_ref, hinf_ref, hinb_ref,
                    alog_ref, ef_ref, eb_ref, dskip_ref, ng_ref, sgw_ref, sgb_ref, wout_ref,
                    mod_ref, o_ref, ycat_ref, cs_ref, sf_ref, sb_ref):
    out = _mix_tile(xbc_ref, dt_ref, zs_ref, gate_ref, v_ref, hinf_ref, hinb_ref, alog_ref, ef_ref,
                    eb_ref, dskip_ref, ng_ref, sgw_ref, sgb_ref, wout_ref, ycat_ref, cs_ref,
                    sf_ref, sb_ref)
    o_ref[...] = h_ref[...] + _mod_vec(mod_ref, 2) * out


def _mix_mlp_kernel(h_ref, xbc_ref, dt_ref, zs_ref, gate_ref, v_ref, hinf_ref, hinb_ref,
                    alog_ref, ef_ref, eb_ref, dskip_ref, ng_ref, sgw_ref, sgb_ref, wout_ref,
                    moda_ref, modb_ref, mg_ref, mw1_ref, mw2_ref, fg_ref, o_ref, ycat_ref, cs_ref,
                    sf_ref, sb_ref, hmid_ref, um_ref, acc_ref, *, final_norm):
    s = pl.program_id(0)
    slot = lax.rem(s + 1, 2)

    @pl.when(s == 0)
    def _():
        hmid_ref[...] = jnp.zeros_like(hmid_ref)

    um_ref[...] = _modulate(hmid_ref[slot], mg_ref[...], _mod_vec(modb_ref, 3),
                            _mod_vec(modb_ref, 4)).astype(BF16)
    acc_ref[...] = jnp.zeros_like(acc_ref)

    def mlp_block(c):
        r = jnp.maximum(_dot(um_ref[...], mw1_ref[c]), 0.0)
        acc_ref[...] += _dot((r * r).astype(BF16), mw2_ref[c])

    out = _mix_tile(xbc_ref, dt_ref, zs_ref, gate_ref, v_ref, hinf_ref, hinb_ref, alog_ref, ef_ref,
                    eb_ref, dskip_ref, ng_ref, sgw_ref, sgb_ref, wout_ref, ycat_ref, cs_ref,
                    sf_ref, sb_ref, per_chunk=mlp_block)

    res = hmid_ref[slot] + _mod_vec(modb_ref, 5) * acc_ref[...]
    if final_norm:
        ms = jnp.mean(res * res, axis=-1, keepdims=True)
        res = res * lax.rsqrt(ms + EPS) * fg_ref[...]
    o_ref[...] = res
    hmid_ref[lax.rem(s, 2)] = h_ref[...] + _mod_vec(moda_ref, 2) * out


def _mix_out_call(h, xbc, dt, zs, gate, v, hin_f, hin_b, alog_p, e_f, e_b, dskip, norm_g,
                  sg_w, sg_b_exp, w_out, mod, mod_row, tq):
    bsz, seq, d = h.shape
    d_inner = e_f.shape[-1]
    gw = d_inner // SSD_GROUPS
    cpb = tq // CHUNK
    const = lambda shape: pl.BlockSpec(shape, lambda b, i: (0,) * len(shape))
    tile = lambda n: pl.BlockSpec((None, tq, n), lambda b, i: (b, i, 0))
    hin_spec = pl.BlockSpec((None, cpb, SSD_GROUPS, SSD_STATE, gw), lambda b, i: (b, i, 0, 0, 0))
    return pl.pallas_call(
        _mix_out_kernel,
        out_shape=jax.ShapeDtypeStruct((bsz, seq, d), F32),
        grid=(bsz, seq // tq),
        in_specs=[
            tile(d), tile(xbc.shape[-1]), tile(LANES), tile(d_inner), tile(d), tile(d),
            hin_spec, hin_spec,
            const((1, LANES)), const(e_f.shape), const(e_b.shape),
            const((1, d_inner)), const((1, d_inner)),
            const(sg_w.shape), const(sg_b_exp.shape), _resident(w_out.shape),
            pl.BlockSpec((None, 1, mod.shape[-1]), lambda b, i: (mod_row(b), 0, 0)),
        ],
        out_specs=tile(d),
        scratch_shapes=[pltpu.VMEM((tq, w_out.shape[0]), BF16),
                        pltpu.VMEM((tq, LANES), F32),
                        pltpu.VMEM((tq, d_inner), F32),
                        pltpu.VMEM((tq, d_inner), F32)],
        compiler_params=_cparams("parallel", "parallel"),
        name="hy_mix_out",
    )(h, xbc, dt, zs, gate, v, hin_f, hin_b, alog_p, e_f, e_b, dskip, norm_g.reshape(1, d_inner),
      sg_w, sg_b_exp, w_out, mod)


def _mix_mlp_call(h, xbc, dt, zs, gate, v, hin_f, hin_b, alog_p, e_f, e_b, dskip, norm_g,
                  sg_w, sg_b_exp, w_out, mod, mod_row, mg, mw1, mw2, final_g, final_norm, tq):
    bsz, seq, d = h.shape
    d_inner = e_f.shape[-1]
    gw = d_inner // SSD_GROUPS
    cpb = tq // CHUNK
    n_tiles = seq // tq
    total = bsz * n_tiles
    hid = mw1.shape[1]
    mw1b = mw1.reshape(d, cpb, hid // cpb).transpose(1, 0, 2)
    mw2b = mw2.reshape(cpb, hid // cpb, d)
    cur = lambda s: jnp.minimum(s, total - 1)
    done = lambda s: jnp.maximum(s - 1, 0)
    const = lambda shape: pl.BlockSpec(shape, lambda s: (0,) * len(shape))
    tile = lambda n: pl.BlockSpec((None, tq, n), lambda s: (cur(s) // n_tiles, cur(s) % n_tiles, 0))
    hin_spec = pl.BlockSpec((None, cpb, SSD_GROUPS, SSD_STATE, gw),
                            lambda s: (cur(s) // n_tiles, cur(s) % n_tiles, 0, 0, 0))
    mod_spec = lambda t: pl.BlockSpec((None, 1, mod.shape[-1]),
                                      lambda s: (mod_row(t(s) // n_tiles), 0, 0))
    return pl.pallas_call(
        functools.partial(_mix_mlp_kernel, final_norm=final_norm),
        out_shape=jax.ShapeDtypeStruct((bsz, seq, d), F32),
        grid=(total + 1,),
        in_specs=[
            tile(d), tile(xbc.shape[-1]), tile(LANES), tile(d_inner), tile(d), tile(d),
            hin_spec, hin_spec,
            const((1, LANES)), const(e_f.shape), const(e_b.shape),
            const((1, d_inner)), const((1, d_inner)),
            const(sg_w.shape), const(sg_b_exp.shape), _resident(w_out.shape),
            mod_spec(cur), mod_spec(done),
            const((1, d)), _resident(mw1b.shape), _resident(mw2b.shape), const((1, d)),
        ],
        out_specs=pl.BlockSpec((None, tq, d), lambda s: (done(s) // n_tiles, done(s) % n_tiles, 0)),
        scratch_shapes=[pltpu.VMEM((tq, w_out.shape[0]), BF16),
                        pltpu.VMEM((tq, LANES), F32),
                        pltpu.VMEM((tq, d_inner), F32),
                        pltpu.VMEM((tq, d_inner), F32),
                        pltpu.VMEM((2, tq, d), F32),
                        pltpu.VMEM((tq, d), BF16),
                        pltpu.VMEM((tq, d), F32)],
        compiler_params=_cparams("arbitrary"),
        name="mix_mlp_final" if final_norm else "mix_mlp",
    )(h, xbc, dt, zs, gate, v, hin_f, hin_b, alog_p, e_f, e_b, dskip, norm_g.reshape(1, d_inner),
      sg_w, sg_b_exp, w_out, mod, mod, mg.reshape(1, d), mw1b, mw2b, final_g.reshape(1, d))


def _conformer_glu(ext, valid, mod_ref, g_ref, w1_ref, b1_ref, a_ref):
    n_lt = a_ref.shape[0]
    cw = n_lt * LANES
    u = _modulate(ext, g_ref[...], _mod_vec(mod_ref, 0), _mod_vec(mod_ref, 1)).astype(BF16)
    a = (_dot(u, w1_ref[:, 0:cw]) + b1_ref[:, 0:cw]) * _sigmoid(
        _dot(u, w1_ref[:, cw:2 * cw]) + b1_ref[:, cw:2 * cw])
    a = jnp.where(valid, a, 0.0)
    for lt in range(n_lt):
        a_ref[lt] = a[:, lt * LANES:(lt + 1) * LANES]


def _conformer_dwconv(a_ref, wd_ref, bd_ref, c_ref, blk):
    n_rb = c_ref.shape[1] // CHUNK
    lt = blk // n_rb
    rb = blk % n_rb
    r0 = rb * CHUNK if isinstance(rb, int) else pl.multiple_of(rb * CHUNK, CHUNK)
    c_ref[lt, pl.ds(r0, CHUNK), :] = _dwconv_block(a_ref, wd_ref, bd_ref, lt, rb,
                                                   HALO - CONF_CONV // 2, CONF_CONV)


def _conformer_out(h_mid, c_ref, mod_ref, lng_ref, lnb_ref, w2_ref, b2_ref):
    c = jnp.concatenate([c_ref[lt] for lt in range(c_ref.shape[0])], axis=1)
    y = _silu(_layernorm(c, lng_ref[...], lnb_ref[...])).astype(BF16)
    return h_mid + _mod_vec(mod_ref, 2) * (_dot(y, w2_ref[...]) + b2_ref[...])


def _conformer_kernel(hp_ref, h_ref, hn_ref, mod_ref, g_ref, w1_ref, b1_ref, wd_ref, bd_ref,
                      lng_ref, lnb_ref, w2_ref, b2_ref, o_ref, a_ref, c_ref):
    valid = _ext_valid(h_ref.shape[0], pl.program_id(1), pl.num_programs(1))
    _conformer_glu(_ext_rows(hp_ref, h_ref, hn_ref), valid, mod_ref, g_ref, w1_ref, b1_ref, a_ref)

    def body(blk, carry):
        _conformer_dwconv(a_ref, wd_ref, bd_ref, c_ref, blk)
        return carry

    lax.fori_loop(0, c_ref.shape[0] * (c_ref.shape[1] // CHUNK), body, 0)
    o_ref[...] = _conformer_out(h_ref[...], c_ref, mod_ref, lng_ref, lnb_ref, w2_ref, b2_ref)


def _conformer_call(h, mod, mod_row, norm_g, w1, b1, wd, bd, ln_g, ln_b, w2, b2, tm):
    bsz, seq, d = h.shape
    n_tiles = seq // tm
    cw = w2.shape[0]
    n_lt = cw // LANES
    wd3 = _lane_tile_major(wd)
    hv = h.reshape(bsz, seq // HALO, HALO, d)
    prev, nxt = _halo_specs(tm, n_tiles, d)
    const = lambda shape: pl.BlockSpec(shape, lambda b, i: (0,) * len(shape))
    tile = pl.BlockSpec((None, tm, d), lambda b, i: (b, i, 0))
    return pl.pallas_call(
        _conformer_kernel,
        out_shape=jax.ShapeDtypeStruct((bsz, seq, d), F32),
        grid=(bsz, n_tiles),
        in_specs=[
            prev, tile, nxt,
            pl.BlockSpec((None, 1, mod.shape[-1]), lambda b, i: (mod_row(b), 0, 0)),
            const((1, d)), _resident(w1.shape), const((1, 2 * cw)), const(wd3.shape),
            const((n_lt, 1, LANES)), const((1, cw)), const((1, cw)), _resident(w2.shape),
            const((1, d)),
        ],
        out_specs=tile,
        scratch_shapes=[pltpu.VMEM((n_lt, tm + 2 * HALO, LANES), F32),
                        pltpu.VMEM((n_lt, tm, LANES), F32)],
        compiler_params=_cparams("parallel", "parallel"),
        name="conformer",
    )(hv, h, hv, mod, norm_g.reshape(1, d), w1, b1.reshape(1, 2 * cw), wd3,
      bd.reshape(n_lt, 1, LANES), ln_g.reshape(1, cw), ln_b.reshape(1, cw), w2, b2.reshape(1, d))


def _mlp_tile(hf, mod_ref, g_ref, w1_ref, w2_ref, fg_ref, a_ref, final_norm):
    u = _modulate(hf, g_ref[...], _mod_vec(mod_ref, 3), _mod_vec(mod_ref, 4)).astype(BF16)
    hid = w1_ref.shape[1]
    step = min(hid, 1024)
    for k in range(0, hid, step):
        r = jnp.maximum(_dot(u, w1_ref[:, k:k + step]), 0.0)
        a_ref[:, k:k + step] = (r * r).astype(BF16)
    out = hf + _mod_vec(mod_ref, 5) * _dot(a_ref[...], w2_ref[...])
    if final_norm:
        ms = jnp.mean(out * out, axis=-1, keepdims=True)
        out = out * lax.rsqrt(ms + EPS) * fg_ref[...]
    return out


def _mlp_kernel(h_ref, mod_ref, g_ref, w1_ref, w2_ref, fg_ref, o_ref, a_ref, *, final_norm):
    o_ref[...] = _mlp_tile(h_ref[...], mod_ref, g_ref, w1_ref, w2_ref, fg_ref, a_ref, final_norm)


def _conf_mlp_kernel(hp_ref, h_ref, hn_ref, moda_ref, modb_ref, g_ref, w1_ref, b1_ref, wd_ref,
                     bd_ref, lng_ref, lnb_ref, w2_ref, b2_ref, mg_ref, mw1_ref, mw2_ref, fg_ref,
                     o_ref, hmid_ref, um_ref, acc_ref, a_ref, c_ref, *, n_tiles, final_norm):
    s = pl.program_id(0)
    tm = h_ref.shape[0]
    slot = lax.rem(s + 1, 2)

    @pl.when(s == 0)
    def _():
        hmid_ref[...] = jnp.zeros_like(hmid_ref)

    um_ref[...] = _modulate(hmid_ref[slot], mg_ref[...], _mod_vec(modb_ref, 3),
                            _mod_vec(modb_ref, 4)).astype(BF16)
    acc_ref[...] = jnp.zeros_like(acc_ref)

    i = lax.rem(jnp.minimum(s, pl.num_programs(0) - 2), n_tiles)
    _conformer_glu(_ext_rows(hp_ref, h_ref, hn_ref), _ext_valid(tm, i, n_tiles), moda_ref, g_ref,
                   w1_ref, b1_ref, a_ref)

    n_blk = mw1_ref.shape[0]
    conv_per = c_ref.shape[0] * (tm // CHUNK) // n_blk

    def body(q, carry):
        r = jnp.maximum(_dot(um_ref[...], mw1_ref[q]), 0.0)
        acc_ref[...] += _dot((r * r).astype(BF16), mw2_ref[q])
        for j in range(conv_per):
            _conformer_dwconv(a_ref, wd_ref, bd_ref, c_ref, q * conv_per + j)
        return carry

    lax.fori_loop(0, n_blk, body, 0)

    out = hmid_ref[slot] + _mod_vec(modb_ref, 5) * acc_ref[...]
    if final_norm:
        ms = jnp.mean(out * out, axis=-1, keepdims=True)
        out = out * lax.rsqrt(ms + EPS) * fg_ref[...]
    o_ref[...] = out

    hmid_ref[lax.rem(s, 2)] = _conformer_out(h_ref[...], c_ref, moda_ref, lng_ref, lnb_ref, w2_ref,
                                            b2_ref)


def _conf_mlp_call(h, mod, mod_row, cg, w1, b1, wd, bd, ln_g, ln_b, w2, b2, mg, mw1, mw2, final_g,
                   final_norm, tm):
    bsz, seq, d = h.shape
    n_tiles = seq // tm
    total = bsz * n_tiles
    cw = w2.shape[0]
    n_lt = cw // LANES
    per = tm // HALO
    wd3 = _lane_tile_major(wd)
    hid = mw1.shape[1]
    mw1b = mw1.reshape(d, hid // MLP_BLOCK, MLP_BLOCK).transpose(1, 0, 2)
    mw2b = mw2.reshape(hid // MLP_BLOCK, MLP_BLOCK, d)
    hv = h.reshape(bsz, seq // HALO, HALO, d)
    cur = lambda s: jnp.minimum(s, total - 1)
    done = lambda s: jnp.maximum(s - 1, 0)
    const = lambda shape: pl.BlockSpec(shape, lambda s: (0,) * len(shape))
    mod_spec = lambda t: pl.BlockSpec((None, 1, mod.shape[-1]),
                                      lambda s: (mod_row(t(s) // n_tiles), 0, 0))
    return pl.pallas_call(
        functools.partial(_conf_mlp_kernel, n_tiles=n_tiles, final_norm=final_norm),
        out_shape=jax.ShapeDtypeStruct((bsz, seq, d), F32),
        grid=(total + 1,),
        in_specs=[
            pl.BlockSpec((None, None, HALO, d), lambda s: (
                cur(s) // n_tiles, jnp.maximum(cur(s) % n_tiles * per - 1, 0), 0, 0)),
            pl.BlockSpec((None, tm, d), lambda s: (cur(s) // n_tiles, cur(s) % n_tiles, 0)),
            pl.BlockSpec((None, None, HALO, d), lambda s: (
                cur(s) // n_tiles, jnp.minimum((cur(s) % n_tiles + 1) * per, n_tiles * per - 1),
                0, 0)),
            mod_spec(cur), mod_spec(done),
            const((1, d)), _resident(w1.shape), const((1, 2 * cw)), const(wd3.shape),
            const((n_lt, 1, LANES)), const((1, cw)), const((1, cw)), _resident(w2.shape),
            const((1, d)),
            const((1, d)), _resident(mw1b.shape), _resident(mw2b.shape), const((1, d)),
        ],
        out_specs=pl.BlockSpec((None, tm, d), lambda s: (done(s) // n_tiles, done(s) % n_tiles, 0)),
        scratch_shapes=[pltpu.VMEM((2, tm, d), F32), pltpu.VMEM((tm, d), BF16),
                        pltpu.VMEM((tm, d), F32), pltpu.VMEM((n_lt, tm + 2 * HALO, LANES), F32),
                        pltpu.VMEM((n_lt, tm, LANES), F32)],
        compiler_params=_cparams("arbitrary"),
        name="conf_mlp_final" if final_norm else "conf_mlp",
    )(hv, h, hv, mod, mod, cg.reshape(1, d), w1, b1.reshape(1, 2 * cw), wd3,
      bd.reshape(n_lt, 1, LANES), ln_g.reshape(1, cw), ln_b.reshape(1, cw), w2, b2.reshape(1, d),
      mg.reshape(1, d), mw1b, mw2b, final_g.reshape(1, d))


def _mlp_call(h, mod, mod_row, norm_g, w1, w2, final_g, final_norm, tm):
    bsz, seq, d = h.shape
    const = lambda shape: pl.BlockSpec(shape, lambda b, i: (0,) * len(shape))
    tile = pl.BlockSpec((None, tm, d), lambda b, i: (b, i, 0))
    return pl.pallas_call(
        functools.partial(_mlp_kernel, final_norm=final_norm),
        out_shape=jax.ShapeDtypeStruct((bsz, seq, d), F32),
        grid=(bsz, seq // tm),
        in_specs=[
            tile,
            pl.BlockSpec((None, 1, mod.shape[-1]), lambda b, i: (mod_row(b), 0, 0)),
            const((1, d)), _resident(w1.shape), _resident(w2.shape), const((1, d)),
        ],
        out_specs=tile,
        scratch_shapes=[pltpu.VMEM((tm, w1.shape[1]), BF16)],
        compiler_params=_cparams("parallel", "parallel"),
        name="mlp_final" if final_norm else "mlp",
    )(h, mod, norm_g.reshape(1, d), w1, w2, final_g.reshape(1, d))


def _tile(seq, want):
    return min(seq, want)


def _hybrid_mixer(h, mod, mod_row, norm_g, p, h0_f, h0_b, need_out, mlp=None):
    seq = h.shape[1]
    zs, xbc, dt, gate, v = _inproj_call(
        h, mod, mod_row, norm_g, p["w_in"], p["conv_w"], p["conv_b"], p["dt_bias"],
        p["ln_g"], p["ln_b"], _tile(seq, TOKEN_TILE))
    ts = _tile(seq, SCAN_TILE)
    tq = _tile(seq, TOKEN_TILE)
    hin_f, hfin_f = _ssd_state_call(xbc, dt, p["alog"], p["e_f"], h0_f, False, ts)
    hin_b, hfin_b = _ssd_state_call(xbc, dt, p["alog"], p["e_b"], h0_b, True, ts)
    out = None
    args = (h, xbc, dt, zs, gate, v, hin_f, hin_b, p["alog"], p["e_f"], p["e_b"], p["dskip"],
            p["norm_g"], p["sg_w"], p["sg_b"], p["w_out"], mod, mod_row)
    if mlp is not None:
        out = _mix_mlp_call(*args, *mlp, tq)
    elif need_out:
        out = _mix_out_call(*args, tq)
    return out, hfin_f, hfin_b


def _hybrid_params(j, hy_w_in, ssd_conv_w, ssd_conv_b, ssd_dt_bias, ssd_a_log, ssd_d, ssd_norm_g,
                   sg_ln_g, sg_ln_b, sg_w, sg_b, hy_w_out):
    d_inner = SSD_HEADS * SSD_HEAD_DIM
    n_xbc = d_inner + 2 * SSD_GROUPS * SSD_STATE
    w = hy_w_in[j]
    o2 = d_inner + n_xbc
    o3 = o2 + 2 * SSD_HEADS
    pad = jnp.zeros((w.shape[0], LANES - 2 * SSD_HEADS), w.dtype)
    w_in_r = jnp.concatenate([w[:, :o2], w[:, o3:], w[:, o2:o3], pad], axis=1).astype(BF16)
    lane_pad = lambda t: jnp.pad(t.reshape(1, -1), ((0, 0), (0, LANES - t.size)))
    head = jnp.arange(d_inner, dtype=jnp.int32) // SSD_HEAD_DIM
    row = jnp.arange(LANES, dtype=jnp.int32)[:, None]
    sg_width = SG_GROUPS * SG_GROUP_DIM
    sg_b_exp = jnp.broadcast_to(sg_b[j].T[:, :, None],
                                (CHUNK, SG_GROUPS, SG_GROUP_DIM)).reshape(CHUNK, sg_width)
    sg_pairs = sg_w[j].reshape(SG_GROUPS // 2, 2, CHUNK, CHUNK).transpose(0, 2, 1, 3).reshape(
        SG_GROUPS // 2, CHUNK, 2 * CHUNK)
    return dict(
        w_in=w_in_r, conv_w=ssd_conv_w[j], conv_b=ssd_conv_b[j],
        dt_bias=lane_pad(ssd_dt_bias[j]), alog=lane_pad(ssd_a_log[j]),
        e_f=(row == head[None, :]).astype(BF16),
        e_b=(row == head[None, :] + SSD_HEADS).astype(BF16),
        dskip=jnp.repeat(ssd_d[j], SSD_HEAD_DIM).reshape(1, d_inner),
        norm_g=ssd_norm_g[j], ln_g=sg_ln_g[j], ln_b=sg_ln_b[j],
        sg_w=sg_pairs.astype(BF16), sg_b=sg_b_exp, w_out=hy_w_out[j].astype(BF16))


def kernel(x, c, ctx, c_ctx, ada_w, ada_b, norm_mix_g, norm_mlp_g, mlp_w1, mlp_w2, hy_w_in,
           ssd_conv_w, ssd_conv_b, ssd_dt_bias, ssd_a_log, ssd_d, ssd_norm_g, sg_ln_g, sg_ln_b,
           sg_w, sg_b, hy_w_out, cf_w_pw1, cf_b_pw1, cf_w_dw, cf_b_dw, cf_ln_g, cf_ln_b, cf_w_pw2,
           cf_b_pw2, final_norm_g):
    bsz, seq, d = x.shape
    depth = ada_w.shape[0]
    assert bsz < MOD_ROWS and seq % 512 == 0 and ctx.shape[1] % (2 * CHUNK) == 0
    cs = jnp.concatenate([c, c_ctx[None, :], jnp.zeros((MOD_ROWS - bsz - 1, d), c.dtype)], axis=0)
    mod = _ada_call(cs, ada_w, ada_b)

    gw = SSD_HEADS * SSD_HEAD_DIM // SSD_GROUPS
    zeros = jnp.zeros((bsz, SSD_GROUPS, SSD_STATE, gw), F32)
    h, hc = x, ctx
    for i in range(depth):
        ctx_needed = i < depth - 1
        last = i == depth - 1
        lat_row = lambda b, i=i: i * MOD_ROWS + b
        ctx_row = lambda b, i=i: i * MOD_ROWS + bsz
        j = i // 2
        w1 = mlp_w1[i].astype(BF16)
        w2 = mlp_w2[i].astype(BF16)
        if i % 2 == 0:
            p = _hybrid_params(j, hy_w_in, ssd_conv_w, ssd_conv_b, ssd_dt_bias, ssd_a_log, ssd_d,
                               ssd_norm_g, sg_ln_g, sg_ln_b, sg_w, sg_b, hy_w_out)
            hc_new, h0_f, h0_b = _hybrid_mixer(hc, mod, ctx_row, norm_mix_g[i], p, zeros, zeros,
                                               ctx_needed)
            h, _, _ = _hybrid_mixer(h, mod, lat_row, norm_mix_g[i], p, h0_f, h0_b, True,
                                    mlp=(norm_mlp_g[i], w1, w2, final_norm_g, last))
            if ctx_needed:
                hc = hc_new
        else:
            args = (cf_w_pw1[j].astype(BF16), cf_b_pw1[j], cf_w_dw[j], cf_b_dw[j], cf_ln_g[j],
                    cf_ln_b[j], cf_w_pw2[j].astype(BF16), cf_b_pw2[j])
            h = _conf_mlp_call(h, mod, lat_row, norm_mix_g[i], *args, norm_mlp_g[i], w1, w2,
                               final_norm_g, last, _tile(seq, TOKEN_TILE))
            if ctx_needed:
                hc = _conformer_call(hc, mod, ctx_row, norm_mix_g[i], *args,
                                     _tile(hc.shape[1], TOKEN_TILE))
        if ctx_needed:
            hc = _mlp_call(hc, mod, ctx_row, norm_mlp_g[i], w1, w2, final_norm_g, False,
                           _tile(hc.shape[1], TOKEN_TILE))
    return h
```

```python
import functools

import jax
import jax.numpy as jnp
from jax import lax
from jax.experimental import pallas as pl
from jax.experimental.pallas import tpu as pltpu

F32 = jnp.float32
BF16 = jnp.bfloat16

SSD_HEADS = 16
SSD_HEAD_DIM = 64
SSD_GROUPS = 2
SSD_STATE = 128
SSD_CONV = 5
CHUNK = 128
SG_GROUPS = 8
SG_GROUP_DIM = 128
CONF_CONV = 31
EPS = 1e-6
LOG2E = 1.4426950408889634
N_MOD = 6
MOD_ROWS = 8
HALO = 16
LANES = 128
VMEM_LIMIT = 56 * 1024 * 1024
TOKEN_TILE = 512
MLP_TILE = 1024
SCAN_TILE = 1024


def _cparams(*sem):
    return pltpu.CompilerParams(dimension_semantics=sem, vmem_limit_bytes=VMEM_LIMIT)


def _dot(a, b):
    return jnp.dot(a, b, preferred_element_type=F32)


def _sigmoid(x):
    return 1.0 / (1.0 + jnp.exp(-x))


def _silu(x):
    return x * _sigmoid(x)


def _gelu_tanh(x):
    c = 0.7978845608028654
    return x * (0.5 + 0.5 * jnp.tanh(x * (c + (c * 0.044715) * (x * x))))


def _softplus(x):
    return jnp.maximum(x, 0.0) + jnp.log1p(jnp.exp(-jnp.abs(x)))


def _modulate(hf, g, shift, scale):
    ms = jnp.mean(hf * hf, axis=-1, keepdims=True)
    return (hf * lax.rsqrt(ms + EPS)) * (g * (1.0 + scale)) + shift


def _layernorm(x, g, b):
    mu = jnp.mean(x, axis=-1, keepdims=True)
    xc = x - mu
    var = jnp.mean(xc * xc, axis=-1, keepdims=True)
    return xc * lax.rsqrt(var + EPS) * g + b


def _mod_vec(mod_ref, k):
    d = mod_ref.shape[-1] // N_MOD
    return mod_ref[:, k * d:(k + 1) * d]


def _cumsum_rows(x, reverse):
    n = x.shape[0]
    row = lax.broadcasted_iota(jnp.int32, x.shape, 0)
    s = 1
    while s < n:
        if reverse:
            x = x + jnp.where(row < n - s, pltpu.roll(x, n - s, axis=0), 0.0)
        else:
            x = x + jnp.where(row >= s, pltpu.roll(x, s, axis=0), 0.0)
        s *= 2
    return x


def _expand_heads(x, e, terms):
    out = None
    r = x
    for _ in range(terms):
        p = r.astype(BF16)
        t = _dot(p, e)
        out = t if out is None else out + t
        r = r - p.astype(F32)
    return out


def _dwconv_rows(ext, w_ref, bias, start, taps, tm):
    n = ext.shape[0]
    acc = None
    for s in range(8):
        ks = [k for k in range(taps) if (start + k) % 8 == s]
        if not ks:
            continue
        rolled = ext if s == 0 else pltpu.roll(ext, n - s, axis=0)
        for k in ks:
            base = start + k - s
            term = w_ref[k:k + 1, :] * rolled[base:base + tm]
            acc = term if acc is None else acc + term
    return acc + bias


def _resident(shape):
    return pl.BlockSpec(shape, lambda *_: (0,) * len(shape), pipeline_mode=pl.Buffered(1))


def _ada_kernel(cs_ref, w_ref, b_ref, o_ref):
    s = _silu(cs_ref[...]).astype(BF16)
    o_ref[...] = _dot(s, w_ref[...].astype(BF16)) + b_ref[...]


def _ada_call(cs, ada_w, ada_b):
    depth, d, n = ada_w.shape
    tn = n // 4
    out = pl.pallas_call(
        _ada_kernel,
        out_shape=jax.ShapeDtypeStruct((depth, MOD_ROWS, n), F32),
        grid=(depth, n // tn),
        in_specs=[
            pl.BlockSpec((MOD_ROWS, d), lambda l, j: (0, 0)),
            pl.BlockSpec((None, d, tn), lambda l, j: (l, 0, j)),
            pl.BlockSpec((None, 1, tn), lambda l, j: (l, 0, j)),
        ],
        out_specs=pl.BlockSpec((None, MOD_ROWS, tn), lambda l, j: (l, 0, j)),
        compiler_params=_cparams("parallel", "parallel"),
        name="ada_mod",
    )(cs, ada_w, ada_b.reshape(depth, 1, n))
    return out.reshape(depth * MOD_ROWS, 1, n)


def _halo_specs(tm, n_tiles, d):
    per = tm // HALO
    last = n_tiles * per - 1
    prev = pl.BlockSpec((None, None, HALO, d),
                        lambda b, i: (b, jnp.maximum(i * per - 1, 0), 0, 0))
    nxt = pl.BlockSpec((None, None, HALO, d),
                       lambda b, i: (b, jnp.minimum((i + 1) * per, last), 0, 0))
    return prev, nxt


def _ext_rows(hp_ref, h_ref, hn_ref):
    return jnp.concatenate([hp_ref[...], h_ref[...], hn_ref[...]], axis=0)


def _ext_valid(tm):
    i = pl.program_id(1)
    n = pl.num_programs(1)
    row = lax.broadcasted_iota(jnp.int32, (tm + 2 * HALO, 1), 0)
    return jnp.logical_and(jnp.logical_or(row >= HALO, i > 0),
                           jnp.logical_or(row < HALO + tm, i < n - 1))


def _inproj_kernel(hp_ref, h_ref, hn_ref, mod_ref, g_ref, w_ref, cw_ref, cb_ref, dtb_ref,
                   lng_ref, lnb_ref, zs_ref, xbc_ref, dt_ref, gate_ref, v_ref):
    tm, d = h_ref.shape
    n_xbc = xbc_ref.shape[-1]
    o_xbc = d
    o_uv = o_xbc + n_xbc
    o_dt = o_uv + 2 * d
    ext = _ext_rows(hp_ref, h_ref, hn_ref)
    u = _modulate(ext, g_ref[...], _mod_vec(mod_ref, 0), _mod_vec(mod_ref, 1))
    u = jnp.where(_ext_valid(tm), u, 0.0).astype(BF16)
    um = u[HALO:HALO + tm]

    xe = _dot(u, w_ref[:, o_xbc:o_uv])
    acc = _dwconv_rows(xe, cw_ref, cb_ref[...], HALO - SSD_CONV // 2, SSD_CONV, tm)
    xbc_ref[...] = _silu(acc).astype(BF16)

    zs_ref[...] = _silu(_dot(um, w_ref[:, 0:d])).astype(BF16)

    lane = lax.broadcasted_iota(jnp.int32, (1, LANES), 1)
    dt = _softplus(_dot(um, w_ref[:, o_dt:o_dt + LANES]) + dtb_ref[...])
    dt_ref[...] = jnp.where(lane < 2 * SSD_HEADS, dt, 0.0)

    gate_ref[...] = _gelu_tanh(_dot(um, w_ref[:, o_uv:o_uv + d])).astype(BF16)
    v = _gelu_tanh(_dot(um, w_ref[:, o_uv + d:o_uv + 2 * d]))
    v_ref[...] = _layernorm(v, lng_ref[...], lnb_ref[...]).astype(BF16)


def _inproj_call(h, mod, mod_row, norm_g, w_in_r, conv_w, conv_b, dt_bias_p, ln_g, ln_b, tm):
    bsz, seq, d = h.shape
    n_tiles = seq // tm
    n_xbc = conv_w.shape[-1]
    hv = h.reshape(bsz, seq // HALO, HALO, d)
    prev, nxt = _halo_specs(tm, n_tiles, d)
    const = lambda shape: pl.BlockSpec(shape, lambda b, i: (0,) * len(shape))
    tile = lambda n: pl.BlockSpec((None, tm, n), lambda b, i: (b, i, 0))
    out_shape = (
        jax.ShapeDtypeStruct((bsz, seq, d), BF16),
        jax.ShapeDtypeStruct((bsz, seq, n_xbc), BF16),
        jax.ShapeDtypeStruct((bsz, seq, LANES), F32),
        jax.ShapeDtypeStruct((bsz, seq, d), BF16),
        jax.ShapeDtypeStruct((bsz, seq, d), BF16),
    )
    return pl.pallas_call(
        _inproj_kernel,
        out_shape=out_shape,
        grid=(bsz, n_tiles),
        in_specs=[
            prev, tile(d), nxt,
            pl.BlockSpec((None, 1, mod.shape[-1]), lambda b, i: (mod_row(b), 0, 0)),
            const((1, d)),
            _resident(w_in_r.shape),
            const(conv_w.shape), const((1, n_xbc)), const((1, LANES)),
            const((1, d)), const((1, d)),
        ],
        out_specs=(tile(d), tile(n_xbc), tile(LANES), tile(d), tile(d)),
        compiler_params=_cparams("parallel", "parallel"),
        name="hy_inproj",
    )(hv, h, hv, mod, norm_g.reshape(1, d), w_in_r, conv_w, conv_b.reshape(1, n_xbc),
      dt_bias_p, ln_g.reshape(1, d), ln_b.reshape(1, d))


def _ssd_state_kernel(xbc_ref, dt_ref, alog_ref, e_ref, h0_ref, hin_ref, hfin_ref, hs_ref,
                      xw_ref, *, reverse):
    n_chunks = hin_ref.shape[0]
    d_inner = e_ref.shape[-1]
    gw = d_inner // SSD_GROUPS
    i = pl.program_id(1)

    @pl.when(i == 0)
    def _():
        hs_ref[...] = h0_ref[...]

    a = -jnp.exp(alog_ref[...])
    e = e_ref[...]
    row8 = lax.broadcasted_iota(jnp.int32, (8, LANES), 0)
    ws = []
    edges = jnp.zeros((8, LANES), F32)
    for c in range(n_chunks):
        dt = dt_ref[c * CHUNK:(c + 1) * CHUNK, :]
        cs = _cumsum_rows(dt * a, reverse)
        edge = cs[0:1, :] if reverse else cs[CHUNK - 1:CHUNK, :]
        ws.append(jnp.exp(edge - cs) * dt)
        edges = jnp.where(row8 == c, edge, edges)
    w = _expand_heads(jnp.concatenate(ws, axis=0), e, 2)
    decay = _expand_heads(jnp.exp(edges), e, 3)
    xw_ref[...] = (xbc_ref[:, 0:d_inner].astype(F32) * w).astype(BF16)
    for t in range(n_chunks):
        c = n_chunks - 1 - t if reverse else t
        rows = slice(c * CHUNK, (c + 1) * CHUNK)
        for g in range(SSD_GROUPS):
            bm = xbc_ref[rows, d_inner + g * SSD_STATE:d_inner + (g + 1) * SSD_STATE]
            s = lax.dot_general(bm, xw_ref[rows, g * gw:(g + 1) * gw], (((0,), (0,)), ((), ())),
                                preferred_element_type=F32)
            hcur = hs_ref[g]
            hin_ref[c, g] = hcur.astype(BF16)
            hs_ref[g] = hcur * decay[c:c + 1, g * gw:(g + 1) * gw] + s

    @pl.when(i == pl.num_programs(1) - 1)
    def _():
        hfin_ref[...] = hs_ref[...]


def _ssd_state_call(xbc, dt, alog_p, e, h0, reverse, tq):
    bsz, seq, _ = xbc.shape
    d_inner = e.shape[-1]
    gw = d_inner // SSD_GROUPS
    n_blk = seq // tq
    cpb = tq // CHUNK
    blk = (lambda i: n_blk - 1 - i) if reverse else (lambda i: i)
    n_xb = d_inner + SSD_GROUPS * SSD_STATE
    return pl.pallas_call(
        functools.partial(_ssd_state_kernel, reverse=reverse),
        out_shape=(
            jax.ShapeDtypeStruct((bsz, seq // CHUNK, SSD_GROUPS, SSD_STATE, gw), BF16),
            jax.ShapeDtypeStruct((bsz, SSD_GROUPS, SSD_STATE, gw), F32),
        ),
        grid=(bsz, n_blk),
        in_specs=[
            pl.BlockSpec((None, tq, n_xb), lambda b, i: (b, blk(i), 0)),
            pl.BlockSpec((None, tq, LANES), lambda b, i: (b, blk(i), 0)),
            pl.BlockSpec((1, LANES), lambda b, i: (0, 0)),
            pl.BlockSpec(e.shape, lambda b, i: (0, 0)),
            pl.BlockSpec((None, SSD_GROUPS, SSD_STATE, gw), lambda b, i: (b, 0, 0, 0)),
        ],
        out_specs=(
            pl.BlockSpec((None, cpb, SSD_GROUPS, SSD_STATE, gw), lambda b, i: (b, blk(i), 0, 0, 0)),
            pl.BlockSpec((None, SSD_GROUPS, SSD_STATE, gw), lambda b, i: (b, 0, 0, 0)),
        ),
        scratch_shapes=[pltpu.VMEM((SSD_GROUPS, SSD_STATE, gw), F32),
                        pltpu.VMEM((tq, d_inner), BF16)],
        compiler_params=_cparams("parallel", "arbitrary"),
        name="ssd_state_bwd" if reverse else "ssd_state_fwd",
    )(xbc, dt, alog_p, e, h0)


def _mix_out_kernel(h_ref, xbc_ref, dt_ref, zs_ref, gate_ref, v_ref, hinf_ref, hinb_ref,
                    alog_ref, ef_ref, eb_ref, dskip_ref, ng_ref, sgw_ref, sgb_ref, wout_ref,
                    mod_ref, o_ref, ycat_ref):
    tq, d = h_ref.shape
    n_chunks = tq // CHUNK
    d_inner = ef_ref.shape[-1]
    gw = d_inner // SSD_GROUPS
    hpg = SSD_HEADS // SSD_GROUPS
    a2 = -jnp.exp(alog_ref[...]) * LOG2E
    lane = lax.broadcasted_iota(jnp.int32, (CHUNK, LANES), 1)
    ii = lax.broadcasted_iota(jnp.int32, (CHUNK, CHUNK), 0)
    jj = lax.broadcasted_iota(jnp.int32, (CHUNK, CHUNK), 1)
    low = ii >= jj
    diag = ii == jj
    lane_lo = lane < SSD_HEAD_DIM
    lane2 = lax.broadcasted_iota(jnp.int32, (CHUNK, 2 * LANES), 1)
    lane2_lo = lane2 < LANES

    def chunk(c, carry):
        rows = pl.ds(pl.multiple_of(c * CHUNK, CHUNK), CHUNK)
        dt = dt_ref[rows, :]
        da = dt * a2
        cs = jnp.where(lane < SSD_HEADS, _cumsum_rows(da, False), _cumsum_rows(da, True))
        ecs = jnp.exp2(cs)
        scale_f = _expand_heads(ecs, ef_ref[...], 2)
        scale_b = _expand_heads(ecs, eb_ref[...], 2)
        rw_t = (cs - jnp.log(dt) * LOG2E).T
        xs = xbc_ref[rows, 0:d_inner]
        cms, scs, dsel = [], [], jnp.zeros((CHUNK, LANES), F32)
        for g in range(SSD_GROUPS):
            bm = xbc_ref[rows, d_inner + g * SSD_STATE:d_inner + (g + 1) * SSD_STATE]
            o_c = d_inner + (SSD_GROUPS + g) * SSD_STATE
            cm = xbc_ref[rows, o_c:o_c + SSD_STATE]
            scores = lax.dot_general(cm, bm, (((1,), (1,)), ((), ())),
                                     preferred_element_type=F32)
            dg = jnp.sum(jnp.where(diag, scores, 0.0), axis=1, keepdims=True)
            lo = SSD_HEADS + g * hpg
            dsel = jnp.where(jnp.logical_and(lane >= lo, lane < lo + hpg), dg, dsel)
            cms.append(cm)
            scs.append(scores)
        skip = dskip_ref[...] + _expand_heads(dsel * dt, eb_ref[...], 1)
        for g in range(SSD_GROUPS):
            cm, scores = cms[g], scs[g]
            cols = slice(g * gw, (g + 1) * gw)
            y_g = (_dot(cm, hinf_ref[c, g]) * scale_f[:, cols]
                   + _dot(cm, hinb_ref[c, g]) * scale_b[:, cols])
            ys = []
            for pair in range(hpg // 2):
                ws = []
                for hh in range(2):
                    hd = g * hpg + pair * 2 + hh
                    hb = SSD_HEADS + hd
                    seg = jnp.where(low, cs[:, hd:hd + 1] - rw_t[hd:hd + 1, :],
                                    cs[:, hb:hb + 1] - rw_t[hb:hb + 1, :])
                    ws.append((scores * jnp.exp2(seg)).astype(BF16))
                p0 = (g * hpg + pair * 2) * SSD_HEAD_DIM
                xp = xs[:, p0:p0 + 2 * SSD_HEAD_DIM]
                zero = jnp.zeros_like(xp)
                rhs = jnp.concatenate([jnp.where(lane_lo, xp, zero),
                                       jnp.where(lane_lo, zero, xp)], axis=0)
                ys.append(_dot(jnp.concatenate(ws, axis=1), rhs))
            y_g = y_g + jnp.concatenate(ys, axis=1)
            y_g = y_g + skip[:, cols] * xs[:, cols].astype(F32)
            y_g = y_g * zs_ref[rows, cols].astype(F32)
            ms = jnp.mean(y_g * y_g, axis=-1, keepdims=True)
            y_g = y_g * lax.rsqrt(ms + EPS) * ng_ref[:, cols]
            ycat_ref[rows, cols] = y_g.astype(BF16)
        for sp in range(SG_GROUPS // 2):
            cols = slice(sp * 2 * SG_GROUP_DIM, (sp + 1) * 2 * SG_GROUP_DIM)
            vp = v_ref[rows, cols]
            zero = jnp.zeros_like(vp)
            rhs = jnp.concatenate([jnp.where(lane2_lo, vp, zero),
                                   jnp.where(lane2_lo, zero, vp)], axis=0)
            s = _dot(sgw_ref[sp], rhs) + sgb_ref[:, cols]
            ycat_ref[rows, d_inner + sp * 2 * SG_GROUP_DIM:d_inner + (sp + 1) * 2 * SG_GROUP_DIM] = (
                gate_ref[rows, cols].astype(F32) * s).astype(BF16)
        return carry

    lax.fori_loop(0, n_chunks, chunk, 0)
    out = _dot(ycat_ref[...], wout_ref[...])
    o_ref[...] = h_ref[...] + _mod_vec(mod_ref, 2) * out


def _mix_out_call(h, xbc, dt, zs, gate, v, hin_f, hin_b, alog_p, e_f, e_b, dskip, norm_g,
                  sg_w, sg_b_exp, w_out, mod, mod_row, tq):
    bsz, seq, d = h.shape
    d_inner = e_f.shape[-1]
    gw = d_inner // SSD_GROUPS
    cpb = tq // CHUNK
    const = lambda shape: pl.BlockSpec(shape, lambda b, i: (0,) * len(shape))
    tile = lambda n: pl.BlockSpec((None, tq, n), lambda b, i: (b, i, 0))
    hin_spec = pl.BlockSpec((None, cpb, SSD_GROUPS, SSD_STATE, gw), lambda b, i: (b, i, 0, 0, 0))
    return pl.pallas_call(
        _mix_out_kernel,
        out_shape=jax.ShapeDtypeStruct((bsz, seq, d), F32),
        grid=(bsz, seq // tq),
        in_specs=[
            tile(d), tile(xbc.shape[-1]), tile(LANES), tile(d_inner), tile(d), tile(d),
            hin_spec, hin_spec,
            const((1, LANES)), const(e_f.shape), const(e_b.shape),
            const((1, d_inner)), const((1, d_inner)),
            const(sg_w.shape), const(sg_b_exp.shape), _resident(w_out.shape),
            pl.BlockSpec((None, 1, mod.shape[-1]), lambda b, i: (mod_row(b), 0, 0)),
        ],
        out_specs=tile(d),
        scratch_shapes=[pltpu.VMEM((tq, w_out.shape[0]), BF16)],
        compiler_params=_cparams("parallel", "parallel"),
        name="hy_mix_out",
    )(h, xbc, dt, zs, gate, v, hin_f, hin_b, alog_p, e_f, e_b, dskip, norm_g.reshape(1, d_inner),
      sg_w, sg_b_exp, w_out, mod)


def _conformer_kernel(hp_ref, h_ref, hn_ref, mod_ref, g_ref, w1_ref, b1_ref, wd_ref, bd_ref,
                      lng_ref, lnb_ref, w2_ref, b2_ref, o_ref):
    tm, d = h_ref.shape
    cw = w2_ref.shape[0]
    ext = _ext_rows(hp_ref, h_ref, hn_ref)
    u = _modulate(ext, g_ref[...], _mod_vec(mod_ref, 0), _mod_vec(mod_ref, 1)).astype(BF16)
    a = (_dot(u, w1_ref[:, 0:cw]) + b1_ref[:, 0:cw]) * _sigmoid(
        _dot(u, w1_ref[:, cw:2 * cw]) + b1_ref[:, cw:2 * cw])
    a = jnp.where(_ext_valid(tm), a, 0.0)
    acc = _dwconv_rows(a, wd_ref, bd_ref[...], HALO - CONF_CONV // 2, CONF_CONV, tm)
    y = _silu(_layernorm(acc, lng_ref[...], lnb_ref[...])).astype(BF16)
    out = _dot(y, w2_ref[...]) + b2_ref[...]
    o_ref[...] = h_ref[...] + _mod_vec(mod_ref, 2) * out


def _conformer_call(h, mod, mod_row, norm_g, w1, b1, wd, bd, ln_g, ln_b, w2, b2, tm):
    bsz, seq, d = h.shape
    n_tiles = seq // tm
    cw = w2.shape[0]
    hv = h.reshape(bsz, seq // HALO, HALO, d)
    prev, nxt = _halo_specs(tm, n_tiles, d)
    const = lambda shape: pl.BlockSpec(shape, lambda b, i: (0,) * len(shape))
    tile = pl.BlockSpec((None, tm, d), lambda b, i: (b, i, 0))
    return pl.pallas_call(
        _conformer_kernel,
        out_shape=jax.ShapeDtypeStruct((bsz, seq, d), F32),
        grid=(bsz, n_tiles),
        in_specs=[
            prev, tile, nxt,
            pl.BlockSpec((None, 1, mod.shape[-1]), lambda b, i: (mod_row(b), 0, 0)),
            const((1, d)), _resident(w1.shape), const((1, 2 * cw)), const(wd.shape),
            const((1, cw)), const((1, cw)), const((1, cw)), _resident(w2.shape), const((1, d)),
        ],
        out_specs=tile,
        compiler_params=_cparams("parallel", "parallel"),
        name="conformer",
    )(hv, h, hv, mod, norm_g.reshape(1, d), w1, b1.reshape(1, 2 * cw), wd, bd.reshape(1, cw),
      ln_g.reshape(1, cw), ln_b.reshape(1, cw), w2, b2.reshape(1, d))


def _mlp_kernel(h_ref, mod_ref, g_ref, w1_ref, w2_ref, fg_ref, o_ref, a_ref, *, final_norm):
    hf = h_ref[...]
    u = _modulate(hf, g_ref[...], _mod_vec(mod_ref, 3), _mod_vec(mod_ref, 4)).astype(BF16)
    hid = w1_ref.shape[1]
    step = min(hid, 1024)
    for k in range(0, hid, step):
        r = jnp.maximum(_dot(u, w1_ref[:, k:k + step]), 0.0)
        a_ref[:, k:k + step] = (r * r).astype(BF16)
    out = hf + _mod_vec(mod_ref, 5) * _dot(a_ref[...], w2_ref[...])
    if final_norm:
        ms = jnp.mean(out * out, axis=-1, keepdims=True)
        out = out * lax.rsqrt(ms + EPS) * fg_ref[...]
    o_ref[...] = out


def _mlp_call(h, mod, mod_row, norm_g, w1, w2, final_g, final_norm, tm):
    bsz, seq, d = h.shape
    const = lambda shape: pl.BlockSpec(shape, lambda b, i: (0,) * len(shape))
    tile = pl.BlockSpec((None, tm, d), lambda b, i: (b, i, 0))
    return pl.pallas_call(
        functools.partial(_mlp_kernel, final_norm=final_norm),
        out_shape=jax.ShapeDtypeStruct((bsz, seq, d), F32),
        grid=(bsz, seq // tm),
        in_specs=[
            tile,
            pl.BlockSpec((None, 1, mod.shape[-1]), lambda b, i: (mod_row(b), 0, 0)),
            const((1, d)), _resident(w1.shape), _resident(w2.shape), const((1, d)),
        ],
        out_specs=tile,
        scratch_shapes=[pltpu.VMEM((tm, w1.shape[1]), BF16)],
        compiler_params=_cparams("parallel", "parallel"),
        name="mlp_final" if final_norm else "mlp",
    )(h, mod, norm_g.reshape(1, d), w1, w2, final_g.reshape(1, d))


def _tile(seq, want):
    return min(seq, want)


def _hybrid_mixer(h, mod, mod_row, norm_g, p, h0_f, h0_b, need_out):
    seq = h.shape[1]
    zs, xbc, dt, gate, v = _inproj_call(
        h, mod, mod_row, norm_g, p["w_in"], p["conv_w"], p["conv_b"], p["dt_bias"],
        p["ln_g"], p["ln_b"], _tile(seq, TOKEN_TILE))
    ts = _tile(seq, SCAN_TILE)
    tq = _tile(seq, TOKEN_TILE)
    hin_f, hfin_f = _ssd_state_call(xbc, dt, p["alog"], p["e_f"], h0_f, False, ts)
    hin_b, hfin_b = _ssd_state_call(xbc, dt, p["alog"], p["e_b"], h0_b, True, ts)
    out = None
    if need_out:
        out = _mix_out_call(h, xbc, dt, zs, gate, v, hin_f, hin_b, p["alog"], p["e_f"], p["e_b"],
                            p["dskip"], p["norm_g"], p["sg_w"], p["sg_b"], p["w_out"], mod,
                            mod_row, tq)
    return out, hfin_f, hfin_b


def _hybrid_params(j, hy_w_in, ssd_conv_w, ssd_conv_b, ssd_dt_bias, ssd_a_log, ssd_d, ssd_norm_g,
                   sg_ln_g, sg_ln_b, sg_w, sg_b, hy_w_out):
    d_inner = SSD_HEADS * SSD_HEAD_DIM
    n_xbc = d_inner + 2 * SSD_GROUPS * SSD_STATE
    w = hy_w_in[j]
    o2 = d_inner + n_xbc
    o3 = o2 + 2 * SSD_HEADS
    pad = jnp.zeros((w.shape[0], LANES - 2 * SSD_HEADS), w.dtype)
    w_in_r = jnp.concatenate([w[:, :o2], w[:, o3:], w[:, o2:o3], pad], axis=1).astype(BF16)
    lane_pad = lambda t: jnp.pad(t.reshape(1, -1), ((0, 0), (0, LANES - t.size)))
    head = jnp.arange(d_inner, dtype=jnp.int32) // SSD_HEAD_DIM
    row = jnp.arange(LANES, dtype=jnp.int32)[:, None]
    sg_width = SG_GROUPS * SG_GROUP_DIM
    sg_b_exp = jnp.broadcast_to(sg_b[j].T[:, :, None],
                                (CHUNK, SG_GROUPS, SG_GROUP_DIM)).reshape(CHUNK, sg_width)
    sg_pairs = sg_w[j].reshape(SG_GROUPS // 2, 2, CHUNK, CHUNK).transpose(0, 2, 1, 3).reshape(
        SG_GROUPS // 2, CHUNK, 2 * CHUNK)
    return dict(
        w_in=w_in_r, conv_w=ssd_conv_w[j], conv_b=ssd_conv_b[j],
        dt_bias=lane_pad(ssd_dt_bias[j]), alog=lane_pad(ssd_a_log[j]),
        e_f=(row == head[None, :]).astype(BF16),
        e_b=(row == head[None, :] + SSD_HEADS).astype(BF16),
        dskip=jnp.repeat(ssd_d[j], SSD_HEAD_DIM).reshape(1, d_inner),
        norm_g=ssd_norm_g[j], ln_g=sg_ln_g[j], ln_b=sg_ln_b[j],
        sg_w=sg_pairs.astype(BF16), sg_b=sg_b_exp, w_out=hy_w_out[j].astype(BF16))


def kernel(x, c, ctx, c_ctx, ada_w, ada_b, norm_mix_g, norm_mlp_g, mlp_w1, mlp_w2, hy_w_in,
           ssd_conv_w, ssd_conv_b, ssd_dt_bias, ssd_a_log, ssd_d, ssd_norm_g, sg_ln_g, sg_ln_b,
           sg_w, sg_b, hy_w_out, cf_w_pw1, cf_b_pw1, cf_w_dw, cf_b_dw, cf_ln_g, cf_ln_b, cf_w_pw2,
           cf_b_pw2, final_norm_g):
    bsz, seq, d = x.shape
    depth = ada_w.shape[0]
    assert bsz < MOD_ROWS and seq % 512 == 0 and ctx.shape[1] % (2 * CHUNK) == 0
    cs = jnp.concatenate([c, c_ctx[None, :], jnp.zeros((MOD_ROWS - bsz - 1, d), c.dtype)], axis=0)
    mod = _ada_call(cs, ada_w, ada_b)

    gw = SSD_HEADS * SSD_HEAD_DIM // SSD_GROUPS
    zeros = jnp.zeros((bsz, SSD_GROUPS, SSD_STATE, gw), F32)
    h, hc = x, ctx
    for i in range(depth):
        ctx_needed = i < depth - 1
        last = i == depth - 1
        lat_row = lambda b, i=i: i * MOD_ROWS + b
        ctx_row = lambda b, i=i: i * MOD_ROWS + bsz
        j = i // 2
        if i % 2 == 0:
            p = _hybrid_params(j, hy_w_in, ssd_conv_w, ssd_conv_b, ssd_dt_bias, ssd_a_log, ssd_d,
                               ssd_norm_g, sg_ln_g, sg_ln_b, sg_w, sg_b, hy_w_out)
            hc_new, h0_f, h0_b = _hybrid_mixer(hc, mod, ctx_row, norm_mix_g[i], p, zeros, zeros,
                                               ctx_needed)
            h, _, _ = _hybrid_mixer(h, mod, lat_row, norm_mix_g[i], p, h0_f, h0_b, True)
            if ctx_needed:
                hc = hc_new
        else:
            w1 = cf_w_pw1[j].astype(BF16)
            w2 = cf_w_pw2[j].astype(BF16)
            args = (w1, cf_b_pw1[j], cf_w_dw[j], cf_b_dw[j], cf_ln_g[j], cf_ln_b[j], w2,
                    cf_b_pw2[j])
            h = _conformer_call(h, mod, lat_row, norm_mix_g[i], *args, _tile(seq, TOKEN_TILE))
            if ctx_needed:
                hc = _conformer_call(hc, mod, ctx_row, norm_mix_g[i], *args,
                                     _tile(hc.shape[1], TOKEN_TILE))
        w1 = mlp_w1[i].astype(BF16)
        w2 = mlp_w2[i].astype(BF16)
        h = _mlp_call(h, mod, lat_row, norm_mlp_g[i], w1, w2, final_norm_g, last,
                      _tile(seq, MLP_TILE))
        if ctx_needed:
            hc = _mlp_call(hc, mod, ctx_row, norm_mlp_g[i], w1, w2, final_norm_g, False,
                           _tile(hc.shape[1], MLP_TILE))
    return h
```

```python
import functools

import jax
import jax.numpy as jnp
from jax import lax
from jax.experimental import pallas as pl
from jax.experimental.pallas import tpu as pltpu

F32 = jnp.float32
BF16 = jnp.bfloat16

SSD_HEADS = 16
SSD_HEAD_DIM = 64
SSD_GROUPS = 2
SSD_STATE = 128
SSD_CONV = 5
CHUNK = 128
SG_GROUPS = 8
SG_GROUP_DIM = 128
CONF_CONV = 31
EPS = 1e-6
LOG2E = 1.4426950408889634
N_MOD = 6
MOD_ROWS = 8
HALO = 16
LANES = 128
VMEM_LIMIT = 56 * 1024 * 1024
TOKEN_TILE = 512
MLP_TILE = 1024
SCAN_TILE = 1024


def _cparams(*sem):
    return pltpu.CompilerParams(dimension_semantics=sem, vmem_limit_bytes=VMEM_LIMIT)


def _dot(a, b):
    return jnp.dot(a, b, preferred_element_type=F32)


def _sigmoid(x):
    return 1.0 / (1.0 + jnp.exp(-x))


def _silu(x):
    return x * _sigmoid(x)


def _gelu_tanh(x):
    c = 0.7978845608028654
    return x * (0.5 + 0.5 * jnp.tanh(x * (c + (c * 0.044715) * (x * x))))


def _softplus(x):
    return jnp.maximum(x, 0.0) + jnp.log1p(jnp.exp(-jnp.abs(x)))


def _modulate(hf, g, shift, scale):
    ms = jnp.mean(hf * hf, axis=-1, keepdims=True)
    return (hf * lax.rsqrt(ms + EPS)) * (g * (1.0 + scale)) + shift


def _layernorm(x, g, b):
    mu = jnp.mean(x, axis=-1, keepdims=True)
    xc = x - mu
    var = jnp.mean(xc * xc, axis=-1, keepdims=True)
    return xc * lax.rsqrt(var + EPS) * g + b


def _mod_vec(mod_ref, k):
    d = mod_ref.shape[-1] // N_MOD
    return mod_ref[:, k * d:(k + 1) * d]


def _cumsum_rows(x, reverse):
    n = x.shape[0]
    row = lax.broadcasted_iota(jnp.int32, x.shape, 0)
    s = 1
    while s < n:
        if reverse:
            x = x + jnp.where(row < n - s, pltpu.roll(x, n - s, axis=0), 0.0)
        else:
            x = x + jnp.where(row >= s, pltpu.roll(x, s, axis=0), 0.0)
        s *= 2
    return x


def _expand_heads(x, e, terms):
    out = None
    r = x
    for _ in range(terms):
        p = r.astype(BF16)
        t = _dot(p, e)
        out = t if out is None else out + t
        r = r - p.astype(F32)
    return out


def _dwconv_rows(ext, w_ref, bias, start, taps, tm):
    n = ext.shape[0]
    acc = None
    for s in range(8):
        ks = [k for k in range(taps) if (start + k) % 8 == s]
        if not ks:
            continue
        rolled = ext if s == 0 else pltpu.roll(ext, n - s, axis=0)
        for k in ks:
            base = start + k - s
            term = w_ref[k:k + 1, :] * rolled[base:base + tm]
            acc = term if acc is None else acc + term
    return acc + bias


def _resident(shape):
    return pl.BlockSpec(shape, lambda *_: (0,) * len(shape), pipeline_mode=pl.Buffered(1))


def _ada_kernel(cs_ref, w_ref, b_ref, o_ref):
    s = _silu(cs_ref[...]).astype(BF16)
    o_ref[...] = _dot(s, w_ref[...].astype(BF16)) + b_ref[...]


def _ada_call(cs, ada_w, ada_b):
    depth, d, n = ada_w.shape
    tn = n // 4
    out = pl.pallas_call(
        _ada_kernel,
        out_shape=jax.ShapeDtypeStruct((depth, MOD_ROWS, n), F32),
        grid=(depth, n // tn),
        in_specs=[
            pl.BlockSpec((MOD_ROWS, d), lambda l, j: (0, 0)),
            pl.BlockSpec((None, d, tn), lambda l, j: (l, 0, j)),
            pl.BlockSpec((None, 1, tn), lambda l, j: (l, 0, j)),
        ],
        out_specs=pl.BlockSpec((None, MOD_ROWS, tn), lambda l, j: (l, 0, j)),
        compiler_params=_cparams("parallel", "parallel"),
        name="ada_mod",
    )(cs, ada_w, ada_b.reshape(depth, 1, n))
    return out.reshape(depth * MOD_ROWS, 1, n)


def _halo_specs(tm, n_tiles, d):
    per = tm // HALO
    last = n_tiles * per - 1
    prev = pl.BlockSpec((None, None, HALO, d),
                        lambda b, i: (b, jnp.maximum(i * per - 1, 0), 0, 0))
    nxt = pl.BlockSpec((None, None, HALO, d),
                       lambda b, i: (b, jnp.minimum((i + 1) * per, last), 0, 0))
    return prev, nxt


def _ext_rows(hp_ref, h_ref, hn_ref):
    return jnp.concatenate([hp_ref[...], h_ref[...], hn_ref[...]], axis=0)


def _ext_valid(tm):
    i = pl.program_id(1)
    n = pl.num_programs(1)
    row = lax.broadcasted_iota(jnp.int32, (tm + 2 * HALO, 1), 0)
    return jnp.logical_and(jnp.logical_or(row >= HALO, i > 0),
                           jnp.logical_or(row < HALO + tm, i < n - 1))


def _inproj_kernel(hp_ref, h_ref, hn_ref, mod_ref, g_ref, w_ref, cw_ref, cb_ref, dtb_ref,
                   lng_ref, lnb_ref, zs_ref, xbc_ref, dt_ref, gate_ref, v_ref):
    tm, d = h_ref.shape
    n_xbc = xbc_ref.shape[-1]
    o_xbc = d
    o_uv = o_xbc + n_xbc
    o_dt = o_uv + 2 * d
    ext = _ext_rows(hp_ref, h_ref, hn_ref)
    u = _modulate(ext, g_ref[...], _mod_vec(mod_ref, 0), _mod_vec(mod_ref, 1))
    u = jnp.where(_ext_valid(tm), u, 0.0).astype(BF16)
    um = u[HALO:HALO + tm]

    xe = _dot(u, w_ref[:, o_xbc:o_uv])
    acc = _dwconv_rows(xe, cw_ref, cb_ref[...], HALO - SSD_CONV // 2, SSD_CONV, tm)
    xbc_ref[...] = _silu(acc).astype(BF16)

    zs_ref[...] = _silu(_dot(um, w_ref[:, 0:d])).astype(BF16)

    lane = lax.broadcasted_iota(jnp.int32, (1, LANES), 1)
    dt = _softplus(_dot(um, w_ref[:, o_dt:o_dt + LANES]) + dtb_ref[...])
    dt_ref[...] = jnp.where(lane < 2 * SSD_HEADS, dt, 0.0)

    gate_ref[...] = _gelu_tanh(_dot(um, w_ref[:, o_uv:o_uv + d])).astype(BF16)
    v = _gelu_tanh(_dot(um, w_ref[:, o_uv + d:o_uv + 2 * d]))
    v_ref[...] = _layernorm(v, lng_ref[...], lnb_ref[...]).astype(BF16)


def _inproj_call(h, mod, mod_row, norm_g, w_in_r, conv_w, conv_b, dt_bias_p, ln_g, ln_b, tm):
    bsz, seq, d = h.shape
    n_tiles = seq // tm
    n_xbc = conv_w.shape[-1]
    hv = h.reshape(bsz, seq // HALO, HALO, d)
    prev, nxt = _halo_specs(tm, n_tiles, d)
    const = lambda shape: pl.BlockSpec(shape, lambda b, i: (0,) * len(shape))
    tile = lambda n: pl.BlockSpec((None, tm, n), lambda b, i: (b, i, 0))
    out_shape = (
        jax.ShapeDtypeStruct((bsz, seq, d), BF16),
        jax.ShapeDtypeStruct((bsz, seq, n_xbc), BF16),
        jax.ShapeDtypeStruct((bsz, seq, LANES), F32),
        jax.ShapeDtypeStruct((bsz, seq, d), BF16),
        jax.ShapeDtypeStruct((bsz, seq, d), BF16),
    )
    return pl.pallas_call(
        _inproj_kernel,
        out_shape=out_shape,
        grid=(bsz, n_tiles),
        in_specs=[
            prev, tile(d), nxt,
            pl.BlockSpec((None, 1, mod.shape[-1]), lambda b, i: (mod_row(b), 0, 0)),
            const((1, d)),
            _resident(w_in_r.shape),
            const(conv_w.shape), const((1, n_xbc)), const((1, LANES)),
            const((1, d)), const((1, d)),
        ],
        out_specs=(tile(d), tile(n_xbc), tile(LANES), tile(d), tile(d)),
        compiler_params=_cparams("parallel", "parallel"),
        name="hy_inproj",
    )(hv, h, hv, mod, norm_g.reshape(1, d), w_in_r, conv_w, conv_b.reshape(1, n_xbc),
      dt_bias_p, ln_g.reshape(1, d), ln_b.reshape(1, d))


def _ssd_state_kernel(xbc_ref, dt_ref, alog_ref, e_ref, h0_ref, hin_ref, hfin_ref, hs_ref,
                      xw_ref, *, reverse):
    n_chunks = hin_ref.shape[0]
    d_inner = e_ref.shape[-1]
    gw = d_inner // SSD_GROUPS
    i = pl.program_id(1)

    @pl.when(i == 0)
    def _():
        hs_ref[...] = h0_ref[...]

    a = -jnp.exp(alog_ref[...])
    e = e_ref[...]
    row8 = lax.broadcasted_iota(jnp.int32, (8, LANES), 0)
    ws = []
    edges = jnp.zeros((8, LANES), F32)
    for c in range(n_chunks):
        dt = dt_ref[c * CHUNK:(c + 1) * CHUNK, :]
        cs = _cumsum_rows(dt * a, reverse)
        edge = cs[0:1, :] if reverse else cs[CHUNK - 1:CHUNK, :]
        ws.append(jnp.exp(edge - cs) * dt)
        edges = jnp.where(row8 == c, edge, edges)
    w = _expand_heads(jnp.concatenate(ws, axis=0), e, 1)
    decay = _expand_heads(jnp.exp(edges), e, 3)
    xw_ref[...] = (xbc_ref[:, 0:d_inner].astype(F32) * w).astype(BF16)
    for t in range(n_chunks):
        c = n_chunks - 1 - t if reverse else t
        rows = slice(c * CHUNK, (c + 1) * CHUNK)
        for g in range(SSD_GROUPS):
            bm = xbc_ref[rows, d_inner + g * SSD_STATE:d_inner + (g + 1) * SSD_STATE]
            s = lax.dot_general(bm, xw_ref[rows, g * gw:(g + 1) * gw], (((0,), (0,)), ((), ())),
                                preferred_element_type=F32)
            hcur = hs_ref[g]
            hin_ref[c, g] = hcur.astype(BF16)
            hs_ref[g] = hcur * decay[c:c + 1, g * gw:(g + 1) * gw] + s

    @pl.when(i == pl.num_programs(1) - 1)
    def _():
        hfin_ref[...] = hs_ref[...]


def _ssd_state_call(xbc, dt, alog_p, e, h0, reverse, tq):
    bsz, seq, _ = xbc.shape
    d_inner = e.shape[-1]
    gw = d_inner // SSD_GROUPS
    n_blk = seq // tq
    cpb = tq // CHUNK
    blk = (lambda i: n_blk - 1 - i) if reverse else (lambda i: i)
    n_xb = d_inner + SSD_GROUPS * SSD_STATE
    return pl.pallas_call(
        functools.partial(_ssd_state_kernel, reverse=reverse),
        out_shape=(
            jax.ShapeDtypeStruct((bsz, seq // CHUNK, SSD_GROUPS, SSD_STATE, gw), BF16),
            jax.ShapeDtypeStruct((bsz, SSD_GROUPS, SSD_STATE, gw), F32),
        ),
        grid=(bsz, n_blk),
        in_specs=[
            pl.BlockSpec((None, tq, n_xb), lambda b, i: (b, blk(i), 0)),
            pl.BlockSpec((None, tq, LANES), lambda b, i: (b, blk(i), 0)),
            pl.BlockSpec((1, LANES), lambda b, i: (0, 0)),
            pl.BlockSpec(e.shape, lambda b, i: (0, 0)),
            pl.BlockSpec((None, SSD_GROUPS, SSD_STATE, gw), lambda b, i: (b, 0, 0, 0)),
        ],
        out_specs=(
            pl.BlockSpec((None, cpb, SSD_GROUPS, SSD_STATE, gw), lambda b, i: (b, blk(i), 0, 0, 0)),
            pl.BlockSpec((None, SSD_GROUPS, SSD_STATE, gw), lambda b, i: (b, 0, 0, 0)),
        ),
        scratch_shapes=[pltpu.VMEM((SSD_GROUPS, SSD_STATE, gw), F32),
                        pltpu.VMEM((tq, d_inner), BF16)],
        compiler_params=_cparams("parallel", "arbitrary"),
        name="ssd_state_bwd" if reverse else "ssd_state_fwd",
    )(xbc, dt, alog_p, e, h0)


def _mix_out_kernel(h_ref, xbc_ref, dt_ref, zs_ref, gate_ref, v_ref, hinf_ref, hinb_ref,
                    alog_ref, ef_ref, eb_ref, dskip_ref, ng_ref, sgw_ref, sgb_ref, wout_ref,
                    mod_ref, o_ref, ycat_ref):
    tq, d = h_ref.shape
    n_chunks = tq // CHUNK
    d_inner = ef_ref.shape[-1]
    gw = d_inner // SSD_GROUPS
    hpg = SSD_HEADS // SSD_GROUPS
    a2 = -jnp.exp(alog_ref[...]) * LOG2E
    lane = lax.broadcasted_iota(jnp.int32, (CHUNK, LANES), 1)
    ii = lax.broadcasted_iota(jnp.int32, (CHUNK, CHUNK), 0)
    jj = lax.broadcasted_iota(jnp.int32, (CHUNK, CHUNK), 1)
    low = ii >= jj
    diag = ii == jj
    lane_lo = lane < SSD_HEAD_DIM
    lane2 = lax.broadcasted_iota(jnp.int32, (CHUNK, 2 * LANES), 1)
    lane2_lo = lane2 < LANES

    def chunk(c, carry):
        rows = pl.ds(pl.multiple_of(c * CHUNK, CHUNK), CHUNK)
        dt = dt_ref[rows, :]
        da = dt * a2
        cs = jnp.where(lane < SSD_HEADS, _cumsum_rows(da, False), _cumsum_rows(da, True))
        ecs = jnp.exp2(cs)
        scale_f = _expand_heads(ecs, ef_ref[...], 1)
        scale_b = _expand_heads(ecs, eb_ref[...], 1)
        rw_t = (cs - jnp.log(dt) * LOG2E).T
        xs = xbc_ref[rows, 0:d_inner]
        cms, scs, dsel = [], [], jnp.zeros((CHUNK, LANES), F32)
        for g in range(SSD_GROUPS):
            bm = xbc_ref[rows, d_inner + g * SSD_STATE:d_inner + (g + 1) * SSD_STATE]
            o_c = d_inner + (SSD_GROUPS + g) * SSD_STATE
            cm = xbc_ref[rows, o_c:o_c + SSD_STATE]
            scores = lax.dot_general(cm, bm, (((1,), (1,)), ((), ())),
                                     preferred_element_type=F32)
            dg = jnp.sum(jnp.where(diag, scores, 0.0), axis=1, keepdims=True)
            lo = SSD_HEADS + g * hpg
            dsel = jnp.where(jnp.logical_and(lane >= lo, lane < lo + hpg), dg, dsel)
            cms.append(cm)
            scs.append(scores)
        skip = dskip_ref[...] + _expand_heads(dsel * dt, eb_ref[...], 1)
        for g in range(SSD_GROUPS):
            cm, scores = cms[g], scs[g]
            cols = slice(g * gw, (g + 1) * gw)
            y_g = (_dot(cm, hinf_ref[c, g]) * scale_f[:, cols]
                   + _dot(cm, hinb_ref[c, g]) * scale_b[:, cols])
            ys = []
            for pair in range(hpg // 2):
                ws = []
                for hh in range(2):
                    hd = g * hpg + pair * 2 + hh
                    hb = SSD_HEADS + hd
                    seg = jnp.where(low, cs[:, hd:hd + 1] - rw_t[hd:hd + 1, :],
                                    cs[:, hb:hb + 1] - rw_t[hb:hb + 1, :])
                    ws.append((scores * jnp.exp2(seg)).astype(BF16))
                p0 = (g * hpg + pair * 2) * SSD_HEAD_DIM
                xp = xs[:, p0:p0 + 2 * SSD_HEAD_DIM]
                zero = jnp.zeros_like(xp)
                rhs = jnp.concatenate([jnp.where(lane_lo, xp, zero),
                                       jnp.where(lane_lo, zero, xp)], axis=0)
                ys.append(_dot(jnp.concatenate(ws, axis=1), rhs))
            y_g = y_g + jnp.concatenate(ys, axis=1)
            y_g = y_g + skip[:, cols] * xs[:, cols].astype(F32)
            y_g = y_g * zs_ref[rows, cols].astype(F32)
            ms = jnp.mean(y_g * y_g, axis=-1, keepdims=True)
            y_g = y_g * lax.rsqrt(ms + EPS) * ng_ref[:, cols]
            ycat_ref[rows, cols] = y_g.astype(BF16)
        for sp in range(SG_GROUPS // 2):
            cols = slice(sp * 2 * SG_GROUP_DIM, (sp + 1) * 2 * SG_GROUP_DIM)
            vp = v_ref[rows, cols]
            zero = jnp.zeros_like(vp)
            rhs = jnp.concatenate([jnp.where(lane2_lo, vp, zero),
                                   jnp.where(lane2_lo, zero, vp)], axis=0)
            s = _dot(sgw_ref[sp], rhs) + sgb_ref[:, cols]
            ycat_ref[rows, d_inner + sp * 2 * SG_GROUP_DIM:d_inner + (sp + 1) * 2 * SG_GROUP_DIM] = (
                gate_ref[rows, cols].astype(F32) * s).astype(BF16)
        return carry

    lax.fori_loop(0, n_chunks, chunk, 0)
    out = _dot(ycat_ref[...], wout_ref[...])
    o_ref[...] = h_ref[...] + _mod_vec(mod_ref, 2) * out


def _mix_out_call(h, xbc, dt, zs, gate, v, hin_f, hin_b, alog_p, e_f, e_b, dskip, norm_g,
                  sg_w, sg_b_exp, w_out, mod, mod_row, tq):
    bsz, seq, d = h.shape
    d_inner = e_f.shape[-1]
    gw = d_inner // SSD_GROUPS
    cpb = tq // CHUNK
    const = lambda shape: pl.BlockSpec(shape, lambda b, i: (0,) * len(shape))
    tile = lambda n: pl.BlockSpec((None, tq, n), lambda b, i: (b, i, 0))
    hin_spec = pl.BlockSpec((None, cpb, SSD_GROUPS, SSD_STATE, gw), lambda b, i: (b, i, 0, 0, 0))
    return pl.pallas_call(
        _mix_out_kernel,
        out_shape=jax.ShapeDtypeStruct((bsz, seq, d), F32),
        grid=(bsz, seq // tq),
        in_specs=[
            tile(d), tile(xbc.shape[-1]), tile(LANES), tile(d_inner), tile(d), tile(d),
            hin_spec, hin_spec,
            const((1, LANES)), const(e_f.shape), const(e_b.shape),
            const((1, d_inner)), const((1, d_inner)),
            const(sg_w.shape), const(sg_b_exp.shape), _resident(w_out.shape),
            pl.BlockSpec((None, 1, mod.shape[-1]), lambda b, i: (mod_row(b), 0, 0)),
        ],
        out_specs=tile(d),
        scratch_shapes=[pltpu.VMEM((tq, w_out.shape[0]), BF16)],
        compiler_params=_cparams("parallel", "parallel"),
        name="hy_mix_out",
    )(h, xbc, dt, zs, gate, v, hin_f, hin_b, alog_p, e_f, e_b, dskip, norm_g.reshape(1, d_inner),
      sg_w, sg_b_exp, w_out, mod)


def _conformer_kernel(hp_ref, h_ref, hn_ref, mod_ref, g_ref, w1_ref, b1_ref, wd_ref, bd_ref,
                      lng_ref, lnb_ref, w2_ref, b2_ref, o_ref):
    tm, d = h_ref.shape
    cw = w2_ref.shape[0]
    ext = _ext_rows(hp_ref, h_ref, hn_ref)
    u = _modulate(ext, g_ref[...], _mod_vec(mod_ref, 0), _mod_vec(mod_ref, 1)).astype(BF16)
    a = (_dot(u, w1_ref[:, 0:cw]) + b1_ref[:, 0:cw]) * _sigmoid(
        _dot(u, w1_ref[:, cw:2 * cw]) + b1_ref[:, cw:2 * cw])
    a = jnp.where(_ext_valid(tm), a, 0.0)
    acc = _dwconv_rows(a, wd_ref, bd_ref[...], HALO - CONF_CONV // 2, CONF_CONV, tm)
    y = _silu(_layernorm(acc, lng_ref[...], lnb_ref[...])).astype(BF16)
    out = _dot(y, w2_ref[...]) + b2_ref[...]
    o_ref[...] = h_ref[...] + _mod_vec(mod_ref, 2) * out


def _conformer_call(h, mod, mod_row, norm_g, w1, b1, wd, bd, ln_g, ln_b, w2, b2, tm):
    bsz, seq, d = h.shape
    n_tiles = seq // tm
    cw = w2.shape[0]
    hv = h.reshape(bsz, seq // HALO, HALO, d)
    prev, nxt = _halo_specs(tm, n_tiles, d)
    const = lambda shape: pl.BlockSpec(shape, lambda b, i: (0,) * len(shape))
    tile = pl.BlockSpec((None, tm, d), lambda b, i: (b, i, 0))
    return pl.pallas_call(
        _conformer_kernel,
        out_shape=jax.ShapeDtypeStruct((bsz, seq, d), F32),
        grid=(bsz, n_tiles),
        in_specs=[
            prev, tile, nxt,
            pl.BlockSpec((None, 1, mod.shape[-1]), lambda b, i: (mod_row(b), 0, 0)),
            const((1, d)), _resident(w1.shape), const((1, 2 * cw)), const(wd.shape),
            const((1, cw)), const((1, cw)), const((1, cw)), _resident(w2.shape), const((1, d)),
        ],
        out_specs=tile,
        compiler_params=_cparams("parallel", "parallel"),
        name="conformer",
    )(hv, h, hv, mod, norm_g.reshape(1, d), w1, b1.reshape(1, 2 * cw), wd, bd.reshape(1, cw),
      ln_g.reshape(1, cw), ln_b.reshape(1, cw), w2, b2.reshape(1, d))


def _mlp_kernel(h_ref, mod_ref, g_ref, w1_ref, w2_ref, fg_ref, o_ref, a_ref, *, final_norm):
    hf = h_ref[...]
    u = _modulate(hf, g_ref[...], _mod_vec(mod_ref, 3), _mod_vec(mod_ref, 4)).astype(BF16)
    hid = w1_ref.shape[1]
    step = min(hid, 1024)
    for k in range(0, hid, step):
        r = jnp.maximum(_dot(u, w1_ref[:, k:k + step]), 0.0)
        a_ref[:, k:k + step] = (r * r).astype(BF16)
    out = hf + _mod_vec(mod_ref, 5) * _dot(a_ref[...], w2_ref[...])
    if final_norm:
        ms = jnp.mean(out * out, axis=-1, keepdims=True)
        out = out * lax.rsqrt(ms + EPS) * fg_ref[...]
    o_ref[...] = out


def _mlp_call(h, mod, mod_row, norm_g, w1, w2, final_g, final_norm, tm):
    bsz, seq, d = h.shape
    const = lambda shape: pl.BlockSpec(shape, lambda b, i: (0,) * len(shape))
    tile = pl.BlockSpec((None, tm, d), lambda b, i: (b, i, 0))
    return pl.pallas_call(
        functools.partial(_mlp_kernel, final_norm=final_norm),
        out_shape=jax.ShapeDtypeStruct((bsz, seq, d), F32),
        grid=(bsz, seq // tm),
        in_specs=[
            tile,
            pl.BlockSpec((None, 1, mod.shape[-1]), lambda b, i: (mod_row(b), 0, 0)),
            const((1, d)), _resident(w1.shape), _resident(w2.shape), const((1, d)),
        ],
        out_specs=tile,
        scratch_shapes=[pltpu.VMEM((tm, w1.shape[1]), BF16)],
        compiler_params=_cparams("parallel", "parallel"),
        name="mlp_final" if final_norm else "mlp",
    )(h, mod, norm_g.reshape(1, d), w1, w2, final_g.reshape(1, d))


def _tile(seq, want):
    return min(seq, want)


def _hybrid_mixer(h, mod, mod_row, norm_g, p, h0_f, h0_b, need_out):
    seq = h.shape[1]
    zs, xbc, dt, gate, v = _inproj_call(
        h, mod, mod_row, norm_g, p["w_in"], p["conv_w"], p["conv_b"], p["dt_bias"],
        p["ln_g"], p["ln_b"], _tile(seq, TOKEN_TILE))
    ts = _tile(seq, SCAN_TILE)
    tq = _tile(seq, TOKEN_TILE)
    hin_f, hfin_f = _ssd_state_call(xbc, dt, p["alog"], p["e_f"], h0_f, False, ts)
    hin_b, hfin_b = _ssd_state_call(xbc, dt, p["alog"], p["e_b"], h0_b, True, ts)
    out = None
    if need_out:
        out = _mix_out_call(h, xbc, dt, zs, gate, v, hin_f, hin_b, p["alog"], p["e_f"], p["e_b"],
                            p["dskip"], p["norm_g"], p["sg_w"], p["sg_b"], p["w_out"], mod,
                            mod_row, tq)
    return out, hfin_f, hfin_b


def _hybrid_params(j, hy_w_in, ssd_conv_w, ssd_conv_b, ssd_dt_bias, ssd_a_log, ssd_d, ssd_norm_g,
                   sg_ln_g, sg_ln_b, sg_w, sg_b, hy_w_out):
    d_inner = SSD_HEADS * SSD_HEAD_DIM
    n_xbc = d_inner + 2 * SSD_GROUPS * SSD_STATE
    w = hy_w_in[j]
    o2 = d_inner + n_xbc
    o3 = o2 + 2 * SSD_HEADS
    pad = jnp.zeros((w.shape[0], LANES - 2 * SSD_HEADS), w.dtype)
    w_in_r = jnp.concatenate([w[:, :o2], w[:, o3:], w[:, o2:o3], pad], axis=1).astype(BF16)
    lane_pad = lambda t: jnp.pad(t.reshape(1, -1), ((0, 0), (0, LANES - t.size)))
    head = jnp.arange(d_inner, dtype=jnp.int32) // SSD_HEAD_DIM
    row = jnp.arange(LANES, dtype=jnp.int32)[:, None]
    sg_width = SG_GROUPS * SG_GROUP_DIM
    sg_b_exp = jnp.broadcast_to(sg_b[j].T[:, :, None],
                                (CHUNK, SG_GROUPS, SG_GROUP_DIM)).reshape(CHUNK, sg_width)
    sg_pairs = sg_w[j].reshape(SG_GROUPS // 2, 2, CHUNK, CHUNK).transpose(0, 2, 1, 3).reshape(
        SG_GROUPS // 2, CHUNK, 2 * CHUNK)
    return dict(
        w_in=w_in_r, conv_w=ssd_conv_w[j], conv_b=ssd_conv_b[j],
        dt_bias=lane_pad(ssd_dt_bias[j]), alog=lane_pad(ssd_a_log[j]),
        e_f=(row == head[None, :]).astype(BF16),
        e_b=(row == head[None, :] + SSD_HEADS).astype(BF16),
        dskip=jnp.repeat(ssd_d[j], SSD_HEAD_DIM).reshape(1, d_inner),
        norm_g=ssd_norm_g[j], ln_g=sg_ln_g[j], ln_b=sg_ln_b[j],
        sg_w=sg_pairs.astype(BF16), sg_b=sg_b_exp, w_out=hy_w_out[j].astype(BF16))


def kernel(x, c, ctx, c_ctx, ada_w, ada_b, norm_mix_g, norm_mlp_g, mlp_w1, mlp_w2, hy_w_in,
           ssd_conv_w, ssd_conv_b, ssd_dt_bias, ssd_a_log, ssd_d, ssd_norm_g, sg_ln_g, sg_ln_b,
           sg_w, sg_b, hy_w_out, cf_w_pw1, cf_b_pw1, cf_w_dw, cf_b_dw, cf_ln_g, cf_ln_b, cf_w_pw2,
           cf_b_pw2, final_norm_g):
    bsz, seq, d = x.shape
    depth = ada_w.shape[0]
    assert bsz < MOD_ROWS and seq % 512 == 0 and ctx.shape[1] % (2 * CHUNK) == 0
    cs = jnp.concatenate([c, c_ctx[None, :], jnp.zeros((MOD_ROWS - bsz - 1, d), c.dtype)], axis=0)
    mod = _ada_call(cs, ada_w, ada_b)

    gw = SSD_HEADS * SSD_HEAD_DIM // SSD_GROUPS
    zeros = jnp.zeros((bsz, SSD_GROUPS, SSD_STATE, gw), F32)
    h, hc = x, ctx
    for i in range(depth):
        ctx_needed = i < depth - 1
        last = i == depth - 1
        lat_row = lambda b, i=i: i * MOD_ROWS + b
        ctx_row = lambda b, i=i: i * MOD_ROWS + bsz
        j = i // 2
        if i % 2 == 0:
            p = _hybrid_params(j, hy_w_in, ssd_conv_w, ssd_conv_b, ssd_dt_bias, ssd_a_log, ssd_d,
                               ssd_norm_g, sg_ln_g, sg_ln_b, sg_w, sg_b, hy_w_out)
            hc_new, h0_f, h0_b = _hybrid_mixer(hc, mod, ctx_row, norm_mix_g[i], p, zeros, zeros,
                                               ctx_needed)
            h, _, _ = _hybrid_mixer(h, mod, lat_row, norm_mix_g[i], p, h0_f, h0_b, True)
            if ctx_needed:
                hc = hc_new
        else:
            w1 = cf_w_pw1[j].astype(BF16)
            w2 = cf_w_pw2[j].astype(BF16)
            args = (w1, cf_b_pw1[j], cf_w_dw[j], cf_b_dw[j], cf_ln_g[j], cf_ln_b[j], w2,
                    cf_b_pw2[j])
            h = _conformer_call(h, mod, lat_row, norm_mix_g[i], *args, _tile(seq, TOKEN_TILE))
            if ctx_needed:
                hc = _conformer_call(hc, mod, ctx_row, norm_mix_g[i], *args,
                                     _tile(hc.shape[1], TOKEN_TILE))
        w1 = mlp_w1[i].astype(BF16)
        w2 = mlp_w2[i].astype(BF16)
        h = _mlp_call(h, mod, lat_row, norm_mlp_g[i], w1, w2, final_norm_g, last,
                      _tile(seq, MLP_TILE))
        if ctx_needed:
            hc = _mlp_call(hc, mod, ctx_row, norm_mlp_g[i], w1, w2, final_norm_g, False,
                           _tile(hc.shape[1], MLP_TILE))
    return h
```

```python
import functools

import jax
import jax.numpy as jnp
from jax import lax
from jax.experimental import pallas as pl
from jax.experimental.pallas import tpu as pltpu

F32 = jnp.float32
BF16 = jnp.bfloat16

SSD_HEADS = 16
SSD_HEAD_DIM = 64
SSD_GROUPS = 2
SSD_STATE = 128
SSD_CONV = 5
CHUNK = 128
SG_GROUPS = 8
SG_GROUP_DIM = 128
CONF_CONV = 31
EPS = 1e-6
LOG2E = 1.4426950408889634
N_MOD = 6
MOD_ROWS = 8
HALO = 16
LANES = 128
VMEM_LIMIT = 56 * 1024 * 1024
TOKEN_TILE = 512
MLP_TILE = 1024
CONF_COLS = 256
SCAN_TILE = 1024


def _cparams(*sem):
    return pltpu.CompilerParams(dimension_semantics=sem, vmem_limit_bytes=VMEM_LIMIT)


def _dot(a, b):
    return jnp.dot(a, b, preferred_element_type=F32)


def _sigmoid(x):
    return 1.0 / (1.0 + jnp.exp(-x))


def _silu(x):
    return x * _sigmoid(x)


def _gelu_tanh(x):
    c = 0.7978845608028654
    return x * (0.5 + 0.5 * jnp.tanh(x * (c + (c * 0.044715) * (x * x))))


def _softplus(x):
    return jnp.maximum(x, 0.0) + jnp.log1p(jnp.exp(-jnp.abs(x)))


def _modulate(hf, g, shift, scale):
    ms = jnp.mean(hf * hf, axis=-1, keepdims=True)
    return (hf * lax.rsqrt(ms + EPS)) * (g * (1.0 + scale)) + shift


def _layernorm(x, g, b):
    mu = jnp.mean(x, axis=-1, keepdims=True)
    xc = x - mu
    var = jnp.mean(xc * xc, axis=-1, keepdims=True)
    return xc * lax.rsqrt(var + EPS) * g + b


def _mod_vec(mod_ref, k):
    d = mod_ref.shape[-1] // N_MOD
    return mod_ref[:, k * d:(k + 1) * d]


def _cumsum_rows(x, reverse):
    n = x.shape[0]
    row = lax.broadcasted_iota(jnp.int32, x.shape, 0)
    s = 1
    while s < n:
        if reverse:
            x = x + jnp.where(row < n - s, pltpu.roll(x, n - s, axis=0), 0.0)
        else:
            x = x + jnp.where(row >= s, pltpu.roll(x, s, axis=0), 0.0)
        s *= 2
    return x


def _expand_heads(x, e, terms):
    out = None
    r = x
    for _ in range(terms):
        p = r.astype(BF16)
        t = _dot(p, e)
        out = t if out is None else out + t
        r = r - p.astype(F32)
    return out


def _dwconv_rows(ext, w_ref, bias, start, taps, tm, cols=slice(None)):
    n = ext.shape[0]
    acc = None
    for s in range(8):
        ks = [k for k in range(taps) if (start + k) % 8 == s]
        if not ks:
            continue
        rolled = ext if s == 0 else pltpu.roll(ext, n - s, axis=0)
        for k in ks:
            base = start + k - s
            term = w_ref[k:k + 1, cols] * rolled[base:base + tm]
            acc = term if acc is None else acc + term
    return acc + bias


def _resident(shape):
    return pl.BlockSpec(shape, lambda *_: (0,) * len(shape), pipeline_mode=pl.Buffered(1))


def _ada_kernel(cs_ref, w_ref, b_ref, o_ref):
    s = _silu(cs_ref[...]).astype(BF16)
    o_ref[...] = _dot(s, w_ref[...].astype(BF16)) + b_ref[...]


def _ada_call(cs, ada_w, ada_b):
    depth, d, n = ada_w.shape
    tn = n // 4
    out = pl.pallas_call(
        _ada_kernel,
        out_shape=jax.ShapeDtypeStruct((depth, MOD_ROWS, n), F32),
        grid=(depth, n // tn),
        in_specs=[
            pl.BlockSpec((MOD_ROWS, d), lambda l, j: (0, 0)),
            pl.BlockSpec((None, d, tn), lambda l, j: (l, 0, j)),
            pl.BlockSpec((None, 1, tn), lambda l, j: (l, 0, j)),
        ],
        out_specs=pl.BlockSpec((None, MOD_ROWS, tn), lambda l, j: (l, 0, j)),
        compiler_params=_cparams("parallel", "parallel"),
        name="ada_mod",
    )(cs, ada_w, ada_b.reshape(depth, 1, n))
    return out.reshape(depth * MOD_ROWS, 1, n)


def _halo_specs(tm, n_tiles, d):
    per = tm // HALO
    last = n_tiles * per - 1
    prev = pl.BlockSpec((None, None, HALO, d),
                        lambda b, i: (b, jnp.maximum(i * per - 1, 0), 0, 0))
    nxt = pl.BlockSpec((None, None, HALO, d),
                       lambda b, i: (b, jnp.minimum((i + 1) * per, last), 0, 0))
    return prev, nxt


def _ext_rows(hp_ref, h_ref, hn_ref):
    return jnp.concatenate([hp_ref[...], h_ref[...], hn_ref[...]], axis=0)


def _ext_valid(tm):
    i = pl.program_id(1)
    n = pl.num_programs(1)
    row = lax.broadcasted_iota(jnp.int32, (tm + 2 * HALO, 1), 0)
    return jnp.logical_and(jnp.logical_or(row >= HALO, i > 0),
                           jnp.logical_or(row < HALO + tm, i < n - 1))


def _inproj_kernel(hp_ref, h_ref, hn_ref, mod_ref, g_ref, w_ref, cw_ref, cb_ref, dtb_ref,
                   lng_ref, lnb_ref, zs_ref, xbc_ref, dt_ref, gate_ref, v_ref):
    tm, d = h_ref.shape
    n_xbc = xbc_ref.shape[-1]
    o_xbc = d
    o_uv = o_xbc + n_xbc
    o_dt = o_uv + 2 * d
    ext = _ext_rows(hp_ref, h_ref, hn_ref)
    u = _modulate(ext, g_ref[...], _mod_vec(mod_ref, 0), _mod_vec(mod_ref, 1))
    u = jnp.where(_ext_valid(tm), u, 0.0).astype(BF16)
    um = u[HALO:HALO + tm]

    xe = _dot(u, w_ref[:, o_xbc:o_uv])
    acc = _dwconv_rows(xe, cw_ref, cb_ref[...], HALO - SSD_CONV // 2, SSD_CONV, tm)
    xbc_ref[...] = _silu(acc).astype(BF16)

    zs_ref[...] = _silu(_dot(um, w_ref[:, 0:d])).astype(BF16)

    lane = lax.broadcasted_iota(jnp.int32, (1, LANES), 1)
    dt = _softplus(_dot(um, w_ref[:, o_dt:o_dt + LANES]) + dtb_ref[...])
    dt_ref[...] = jnp.where(lane < 2 * SSD_HEADS, dt, 0.0)

    gate_ref[...] = _gelu_tanh(_dot(um, w_ref[:, o_uv:o_uv + d])).astype(BF16)
    v = _gelu_tanh(_dot(um, w_ref[:, o_uv + d:o_uv + 2 * d]))
    v_ref[...] = _layernorm(v, lng_ref[...], lnb_ref[...]).astype(BF16)


def _inproj_call(h, mod, mod_row, norm_g, w_in_r, conv_w, conv_b, dt_bias_p, ln_g, ln_b, tm):
    bsz, seq, d = h.shape
    n_tiles = seq // tm
    n_xbc = conv_w.shape[-1]
    hv = h.reshape(bsz, seq // HALO, HALO, d)
    prev, nxt = _halo_specs(tm, n_tiles, d)
    const = lambda shape: pl.BlockSpec(shape, lambda b, i: (0,) * len(shape))
    tile = lambda n: pl.BlockSpec((None, tm, n), lambda b, i: (b, i, 0))
    out_shape = (
        jax.ShapeDtypeStruct((bsz, seq, d), BF16),
        jax.ShapeDtypeStruct((bsz, seq, n_xbc), BF16),
        jax.ShapeDtypeStruct((bsz, seq, LANES), F32),
        jax.ShapeDtypeStruct((bsz, seq, d), BF16),
        jax.ShapeDtypeStruct((bsz, seq, d), BF16),
    )
    return pl.pallas_call(
        _inproj_kernel,
        out_shape=out_shape,
        grid=(bsz, n_tiles),
        in_specs=[
            prev, tile(d), nxt,
            pl.BlockSpec((None, 1, mod.shape[-1]), lambda b, i: (mod_row(b), 0, 0)),
            const((1, d)),
            _resident(w_in_r.shape),
            const(conv_w.shape), const((1, n_xbc)), const((1, LANES)),
            const((1, d)), const((1, d)),
        ],
        out_specs=(tile(d), tile(n_xbc), tile(LANES), tile(d), tile(d)),
        compiler_params=_cparams("parallel", "parallel"),
        name="hy_inproj",
    )(hv, h, hv, mod, norm_g.reshape(1, d), w_in_r, conv_w, conv_b.reshape(1, n_xbc),
      dt_bias_p, ln_g.reshape(1, d), ln_b.reshape(1, d))


def _ssd_state_kernel(xbc_ref, dt_ref, alog_ref, e_ref, h0_ref, hin_ref, hfin_ref, hs_ref,
                      xw_ref, *, reverse):
    n_chunks = hin_ref.shape[0]
    d_inner = e_ref.shape[-1]
    gw = d_inner // SSD_GROUPS
    i = pl.program_id(1)

    @pl.when(i == 0)
    def _():
        hs_ref[...] = h0_ref[...]

    a = -jnp.exp(alog_ref[...])
    e = e_ref[...]
    row8 = lax.broadcasted_iota(jnp.int32, (8, LANES), 0)
    ws = []
    edges = jnp.zeros((8, LANES), F32)
    for c in range(n_chunks):
        dt = dt_ref[c * CHUNK:(c + 1) * CHUNK, :]
        cs = _cumsum_rows(dt * a, reverse)
        edge = cs[0:1, :] if reverse else cs[CHUNK - 1:CHUNK, :]
        ws.append(jnp.exp(edge - cs) * dt)
        edges = jnp.where(row8 == c, edge, edges)
    w = _expand_heads(jnp.concatenate(ws, axis=0), e, 1)
    decay = _expand_heads(jnp.exp(edges), e, 3)
    xw_ref[...] = (xbc_ref[:, 0:d_inner].astype(F32) * w).astype(BF16)
    for t in range(n_chunks):
        c = n_chunks - 1 - t if reverse else t
        rows = slice(c * CHUNK, (c + 1) * CHUNK)
        for g in range(SSD_GROUPS):
            bm = xbc_ref[rows, d_inner + g * SSD_STATE:d_inner + (g + 1) * SSD_STATE]
            s = lax.dot_general(bm, xw_ref[rows, g * gw:(g + 1) * gw], (((0,), (0,)), ((), ())),
                                preferred_element_type=F32)
            hcur = hs_ref[g]
            hin_ref[c, g] = hcur.astype(BF16)
            hs_ref[g] = hcur * decay[c:c + 1, g * gw:(g + 1) * gw] + s

    @pl.when(i == pl.num_programs(1) - 1)
    def _():
        hfin_ref[...] = hs_ref[...]


def _ssd_state_call(xbc, dt, alog_p, e, h0, reverse, tq):
    bsz, seq, _ = xbc.shape
    d_inner = e.shape[-1]
    gw = d_inner // SSD_GROUPS
    n_blk = seq // tq
    cpb = tq // CHUNK
    blk = (lambda i: n_blk - 1 - i) if reverse else (lambda i: i)
    n_xb = d_inner + SSD_GROUPS * SSD_STATE
    return pl.pallas_call(
        functools.partial(_ssd_state_kernel, reverse=reverse),
        out_shape=(
            jax.ShapeDtypeStruct((bsz, seq // CHUNK, SSD_GROUPS, SSD_STATE, gw), BF16),
            jax.ShapeDtypeStruct((bsz, SSD_GROUPS, SSD_STATE, gw), F32),
        ),
        grid=(bsz, n_blk),
        in_specs=[
            pl.BlockSpec((None, tq, n_xb), lambda b, i: (b, blk(i), 0)),
            pl.BlockSpec((None, tq, LANES), lambda b, i: (b, blk(i), 0)),
            pl.BlockSpec((1, LANES), lambda b, i: (0, 0)),
            pl.BlockSpec(e.shape, lambda b, i: (0, 0)),
            pl.BlockSpec((None, SSD_GROUPS, SSD_STATE, gw), lambda b, i: (b, 0, 0, 0)),
        ],
        out_specs=(
            pl.BlockSpec((None, cpb, SSD_GROUPS, SSD_STATE, gw), lambda b, i: (b, blk(i), 0, 0, 0)),
            pl.BlockSpec((None, SSD_GROUPS, SSD_STATE, gw), lambda b, i: (b, 0, 0, 0)),
        ),
        scratch_shapes=[pltpu.VMEM((SSD_GROUPS, SSD_STATE, gw), F32),
                        pltpu.VMEM((tq, d_inner), BF16)],
        compiler_params=_cparams("parallel", "arbitrary"),
        name="ssd_state_bwd" if reverse else "ssd_state_fwd",
    )(xbc, dt, alog_p, e, h0)


def _mix_out_kernel(h_ref, xbc_ref, dt_ref, zs_ref, gate_ref, v_ref, hinf_ref, hinb_ref,
                    alog_ref, ef_ref, eb_ref, dskip_ref, ng_ref, sgw_ref, sgb_ref, wout_ref,
                    mod_ref, o_ref, ycat_ref):
    tq, d = h_ref.shape
    n_chunks = tq // CHUNK
    d_inner = ef_ref.shape[-1]
    gw = d_inner // SSD_GROUPS
    hpg = SSD_HEADS // SSD_GROUPS
    a2 = -jnp.exp(alog_ref[...]) * LOG2E
    lane = lax.broadcasted_iota(jnp.int32, (CHUNK, LANES), 1)
    ii = lax.broadcasted_iota(jnp.int32, (CHUNK, CHUNK), 0)
    jj = lax.broadcasted_iota(jnp.int32, (CHUNK, CHUNK), 1)
    low = ii >= jj
    diag = ii == jj
    lane_lo = lane < SSD_HEAD_DIM
    lane2 = lax.broadcasted_iota(jnp.int32, (CHUNK, 2 * LANES), 1)
    lane2_lo = lane2 < LANES

    def chunk(c, carry):
        rows = pl.ds(pl.multiple_of(c * CHUNK, CHUNK), CHUNK)
        dt = dt_ref[rows, :]
        da = dt * a2
        cs = jnp.where(lane < SSD_HEADS, _cumsum_rows(da, False), _cumsum_rows(da, True))
        ecs = jnp.exp2(cs)
        scale_f = _expand_heads(ecs, ef_ref[...], 1)
        scale_b = _expand_heads(ecs, eb_ref[...], 1)
        rw_t = (cs - jnp.log(dt) * LOG2E).T
        xs = xbc_ref[rows, 0:d_inner]
        cms, scs, dsel = [], [], jnp.zeros((CHUNK, LANES), F32)
        for g in range(SSD_GROUPS):
            bm = xbc_ref[rows, d_inner + g * SSD_STATE:d_inner + (g + 1) * SSD_STATE]
            o_c = d_inner + (SSD_GROUPS + g) * SSD_STATE
            cm = xbc_ref[rows, o_c:o_c + SSD_STATE]
            scores = lax.dot_general(cm, bm, (((1,), (1,)), ((), ())),
                                     preferred_element_type=F32)
            dg = jnp.sum(jnp.where(diag, scores, 0.0), axis=1, keepdims=True)
            lo = SSD_HEADS + g * hpg
            dsel = jnp.where(jnp.logical_and(lane >= lo, lane < lo + hpg), dg, dsel)
            cms.append(cm)
            scs.append(scores)
        skip = dskip_ref[...] + _expand_heads(dsel * dt, eb_ref[...], 1)
        for g in range(SSD_GROUPS):
            cm, scores = cms[g], scs[g]
            cols = slice(g * gw, (g + 1) * gw)
            y_g = (_dot(cm, hinf_ref[c, g]) * scale_f[:, cols]
                   + _dot(cm, hinb_ref[c, g]) * scale_b[:, cols])
            ys = []
            for pair in range(hpg // 2):
                ws = []
                for hh in range(2):
                    hd = g * hpg + pair * 2 + hh
                    hb = SSD_HEADS + hd
                    seg = jnp.where(low, cs[:, hd:hd + 1] - rw_t[hd:hd + 1, :],
                                    cs[:, hb:hb + 1] - rw_t[hb:hb + 1, :])
                    ws.append((scores * jnp.exp2(seg)).astype(BF16))
                p0 = (g * hpg + pair * 2) * SSD_HEAD_DIM
                xp = xs[:, p0:p0 + 2 * SSD_HEAD_DIM]
                zero = jnp.zeros_like(xp)
                rhs = jnp.concatenate([jnp.where(lane_lo, xp, zero),
                                       jnp.where(lane_lo, zero, xp)], axis=0)
                ys.append(_dot(jnp.concatenate(ws, axis=1), rhs))
            y_g = y_g + jnp.concatenate(ys, axis=1)
            y_g = y_g + skip[:, cols] * xs[:, cols].astype(F32)
            y_g = y_g * zs_ref[rows, cols].astype(F32)
            ms = jnp.mean(y_g * y_g, axis=-1, keepdims=True)
            y_g = y_g * lax.rsqrt(ms + EPS) * ng_ref[:, cols]
            ycat_ref[rows, cols] = y_g.astype(BF16)
        for sp in range(SG_GROUPS // 2):
            cols = slice(sp * 2 * SG_GROUP_DIM, (sp + 1) * 2 * SG_GROUP_DIM)
            vp = v_ref[rows, cols]
            zero = jnp.zeros_like(vp)
            rhs = jnp.concatenate([jnp.where(lane2_lo, vp, zero),
                                   jnp.where(lane2_lo, zero, vp)], axis=0)
            s = _dot(sgw_ref[sp], rhs) + sgb_ref[:, cols]
            ycat_ref[rows, d_inner + sp * 2 * SG_GROUP_DIM:d_inner + (sp + 1) * 2 * SG_GROUP_DIM] = (
                gate_ref[rows, cols].astype(F32) * s).astype(BF16)
        return carry

    lax.fori_loop(0, n_chunks, chunk, 0)
    out = _dot(ycat_ref[...], wout_ref[...])
    o_ref[...] = h_ref[...] + _mod_vec(mod_ref, 2) * out


def _mix_out_call(h, xbc, dt, zs, gate, v, hin_f, hin_b, alog_p, e_f, e_b, dskip, norm_g,
                  sg_w, sg_b_exp, w_out, mod, mod_row, tq):
    bsz, seq, d = h.shape
    d_inner = e_f.shape[-1]
    gw = d_inner // SSD_GROUPS
    cpb = tq // CHUNK
    const = lambda shape: pl.BlockSpec(shape, lambda b, i: (0,) * len(shape))
    tile = lambda n: pl.BlockSpec((None, tq, n), lambda b, i: (b, i, 0))
    hin_spec = pl.BlockSpec((None, cpb, SSD_GROUPS, SSD_STATE, gw), lambda b, i: (b, i, 0, 0, 0))
    return pl.pallas_call(
        _mix_out_kernel,
        out_shape=jax.ShapeDtypeStruct((bsz, seq, d), F32),
        grid=(bsz, seq // tq),
        in_specs=[
            tile(d), tile(xbc.shape[-1]), tile(LANES), tile(d_inner), tile(d), tile(d),
            hin_spec, hin_spec,
            const((1, LANES)), const(e_f.shape), const(e_b.shape),
            const((1, d_inner)), const((1, d_inner)),
            const(sg_w.shape), const(sg_b_exp.shape), _resident(w_out.shape),
            pl.BlockSpec((None, 1, mod.shape[-1]), lambda b, i: (mod_row(b), 0, 0)),
        ],
        out_specs=tile(d),
        scratch_shapes=[pltpu.VMEM((tq, w_out.shape[0]), BF16)],
        compiler_params=_cparams("parallel", "parallel"),
        name="hy_mix_out",
    )(h, xbc, dt, zs, gate, v, hin_f, hin_b, alog_p, e_f, e_b, dskip, norm_g.reshape(1, d_inner),
      sg_w, sg_b_exp, w_out, mod)


def _conformer_kernel(hp_ref, h_ref, hn_ref, mod_ref, g_ref, w1_ref, b1_ref, wd_ref, bd_ref,
                      lng_ref, lnb_ref, w2_ref, b2_ref, o_ref):
    tm, d = h_ref.shape
    cw = w2_ref.shape[0]
    ext = _ext_rows(hp_ref, h_ref, hn_ref)
    u = _modulate(ext, g_ref[...], _mod_vec(mod_ref, 0), _mod_vec(mod_ref, 1)).astype(BF16)
    valid = _ext_valid(tm)
    accs = []
    for c0 in range(0, cw, CONF_COLS):
        cols = slice(c0, c0 + CONF_COLS)
        gcols = slice(cw + c0, cw + c0 + CONF_COLS)
        a = (_dot(u, w1_ref[:, cols]) + b1_ref[:, cols]) * _sigmoid(
            _dot(u, w1_ref[:, gcols]) + b1_ref[:, gcols])
        a = jnp.where(valid, a, 0.0)
        accs.append(_dwconv_rows(a, wd_ref, bd_ref[:, cols], HALO - CONF_CONV // 2, CONF_CONV, tm,
                                 cols))
    acc = jnp.concatenate(accs, axis=1)
    y = _silu(_layernorm(acc, lng_ref[...], lnb_ref[...])).astype(BF16)
    out = _dot(y, w2_ref[...]) + b2_ref[...]
    o_ref[...] = h_ref[...] + _mod_vec(mod_ref, 2) * out


def _conformer_call(h, mod, mod_row, norm_g, w1, b1, wd, bd, ln_g, ln_b, w2, b2, tm):
    bsz, seq, d = h.shape
    n_tiles = seq // tm
    cw = w2.shape[0]
    hv = h.reshape(bsz, seq // HALO, HALO, d)
    prev, nxt = _halo_specs(tm, n_tiles, d)
    const = lambda shape: pl.BlockSpec(shape, lambda b, i: (0,) * len(shape))
    tile = pl.BlockSpec((None, tm, d), lambda b, i: (b, i, 0))
    return pl.pallas_call(
        _conformer_kernel,
        out_shape=jax.ShapeDtypeStruct((bsz, seq, d), F32),
        grid=(bsz, n_tiles),
        in_specs=[
            prev, tile, nxt,
            pl.BlockSpec((None, 1, mod.shape[-1]), lambda b, i: (mod_row(b), 0, 0)),
            const((1, d)), _resident(w1.shape), const((1, 2 * cw)), const(wd.shape),
            const((1, cw)), const((1, cw)), const((1, cw)), _resident(w2.shape), const((1, d)),
        ],
        out_specs=tile,
        compiler_params=_cparams("parallel", "parallel"),
        name="conformer",
    )(hv, h, hv, mod, norm_g.reshape(1, d), w1, b1.reshape(1, 2 * cw), wd, bd.reshape(1, cw),
      ln_g.reshape(1, cw), ln_b.reshape(1, cw), w2, b2.reshape(1, d))


def _mlp_kernel(h_ref, mod_ref, g_ref, w1_ref, w2_ref, fg_ref, o_ref, a_ref, *, final_norm):
    hf = h_ref[...]
    u = _modulate(hf, g_ref[...], _mod_vec(mod_ref, 3), _mod_vec(mod_ref, 4)).astype(BF16)
    hid = w1_ref.shape[1]
    step = min(hid, 1024)
    for k in range(0, hid, step):
        r = jnp.maximum(_dot(u, w1_ref[:, k:k + step]), 0.0)
        a_ref[:, k:k + step] = (r * r).astype(BF16)
    out = hf + _mod_vec(mod_ref, 5) * _dot(a_ref[...], w2_ref[...])
    if final_norm:
        ms = jnp.mean(out * out, axis=-1, keepdims=True)
        out = out * lax.rsqrt(ms + EPS) * fg_ref[...]
    o_ref[...] = out


def _mlp_call(h, mod, mod_row, norm_g, w1, w2, final_g, final_norm, tm):
    bsz, seq, d = h.shape
    const = lambda shape: pl.BlockSpec(shape, lambda b, i: (0,) * len(shape))
    tile = pl.BlockSpec((None, tm, d), lambda b, i: (b, i, 0))
    return pl.pallas_call(
        functools.partial(_mlp_kernel, final_norm=final_norm),
        out_shape=jax.ShapeDtypeStruct((bsz, seq, d), F32),
        grid=(bsz, seq // tm),
        in_specs=[
            tile,
            pl.BlockSpec((None, 1, mod.shape[-1]), lambda b, i: (mod_row(b), 0, 0)),
            const((1, d)), _resident(w1.shape), _resident(w2.shape), const((1, d)),
        ],
        out_specs=tile,
        scratch_shapes=[pltpu.VMEM((tm, w1.shape[1]), BF16)],
        compiler_params=_cparams("parallel", "parallel"),
        name="mlp_final" if final_norm else "mlp",
    )(h, mod, norm_g.reshape(1, d), w1, w2, final_g.reshape(1, d))


def _tile(seq, want):
    return min(seq, want)


def _hybrid_mixer(h, mod, mod_row, norm_g, p, h0_f, h0_b, need_out):
    seq = h.shape[1]
    zs, xbc, dt, gate, v = _inproj_call(
        h, mod, mod_row, norm_g, p["w_in"], p["conv_w"], p["conv_b"], p["dt_bias"],
        p["ln_g"], p["ln_b"], _tile(seq, TOKEN_TILE))
    ts = _tile(seq, SCAN_TILE)
    tq = _tile(seq, TOKEN_TILE)
    hin_f, hfin_f = _ssd_state_call(xbc, dt, p["alog"], p["e_f"], h0_f, False, ts)
    hin_b, hfin_b = _ssd_state_call(xbc, dt, p["alog"], p["e_b"], h0_b, True, ts)
    out = None
    if need_out:
        out = _mix_out_call(h, xbc, dt, zs, gate, v, hin_f, hin_b, p["alog"], p["e_f"], p["e_b"],
                            p["dskip"], p["norm_g"], p["sg_w"], p["sg_b"], p["w_out"], mod,
                            mod_row, tq)
    return out, hfin_f, hfin_b


def _hybrid_params(j, hy_w_in, ssd_conv_w, ssd_conv_b, ssd_dt_bias, ssd_a_log, ssd_d, ssd_norm_g,
                   sg_ln_g, sg_ln_b, sg_w, sg_b, hy_w_out):
    d_inner = SSD_HEADS * SSD_HEAD_DIM
    n_xbc = d_inner + 2 * SSD_GROUPS * SSD_STATE
    w = hy_w_in[j]
    o2 = d_inner + n_xbc
    o3 = o2 + 2 * SSD_HEADS
    pad = jnp.zeros((w.shape[0], LANES - 2 * SSD_HEADS), w.dtype)
    w_in_r = jnp.concatenate([w[:, :o2], w[:, o3:], w[:, o2:o3], pad], axis=1).astype(BF16)
    lane_pad = lambda t: jnp.pad(t.reshape(1, -1), ((0, 0), (0, LANES - t.size)))
    head = jnp.arange(d_inner, dtype=jnp.int32) // SSD_HEAD_DIM
    row = jnp.arange(LANES, dtype=jnp.int32)[:, None]
    sg_width = SG_GROUPS * SG_GROUP_DIM
    sg_b_exp = jnp.broadcast_to(sg_b[j].T[:, :, None],
                                (CHUNK, SG_GROUPS, SG_GROUP_DIM)).reshape(CHUNK, sg_width)
    sg_pairs = sg_w[j].reshape(SG_GROUPS // 2, 2, CHUNK, CHUNK).transpose(0, 2, 1, 3).reshape(
        SG_GROUPS // 2, CHUNK, 2 * CHUNK)
    return dict(
        w_in=w_in_r, conv_w=ssd_conv_w[j], conv_b=ssd_conv_b[j],
        dt_bias=lane_pad(ssd_dt_bias[j]), alog=lane_pad(ssd_a_log[j]),
        e_f=(row == head[None, :]).astype(BF16),
        e_b=(row == head[None, :] + SSD_HEADS).astype(BF16),
        dskip=jnp.repeat(ssd_d[j], SSD_HEAD_DIM).reshape(1, d_inner),
        norm_g=ssd_norm_g[j], ln_g=sg_ln_g[j], ln_b=sg_ln_b[j],
        sg_w=sg_pairs.astype(BF16), sg_b=sg_b_exp, w_out=hy_w_out[j].astype(BF16))


def kernel(x, c, ctx, c_ctx, ada_w, ada_b, norm_mix_g, norm_mlp_g, mlp_w1, mlp_w2, hy_w_in,
           ssd_conv_w, ssd_conv_b, ssd_dt_bias, ssd_a_log, ssd_d, ssd_norm_g, sg_ln_g, sg_ln_b,
           sg_w, sg_b, hy_w_out, cf_w_pw1, cf_b_pw1, cf_w_dw, cf_b_dw, cf_ln_g, cf_ln_b, cf_w_pw2,
           cf_b_pw2, final_norm_g):
    bsz, seq, d = x.shape
    depth = ada_w.shape[0]
    assert bsz < MOD_ROWS and seq % 512 == 0 and ctx.shape[1] % (2 * CHUNK) == 0
    cs = jnp.concatenate([c, c_ctx[None, :], jnp.zeros((MOD_ROWS - bsz - 1, d), c.dtype)], axis=0)
    mod = _ada_call(cs, ada_w, ada_b)

    gw = SSD_HEADS * SSD_HEAD_DIM // SSD_GROUPS
    zeros = jnp.zeros((bsz, SSD_GROUPS, SSD_STATE, gw), F32)
    h, hc = x, ctx
    for i in range(depth):
        ctx_needed = i < depth - 1
        last = i == depth - 1
        lat_row = lambda b, i=i: i * MOD_ROWS + b
        ctx_row = lambda b, i=i: i * MOD_ROWS + bsz
        j = i // 2
        if i % 2 == 0:
            p = _hybrid_params(j, hy_w_in, ssd_conv_w, ssd_conv_b, ssd_dt_bias, ssd_a_log, ssd_d,
                               ssd_norm_g, sg_ln_g, sg_ln_b, sg_w, sg_b, hy_w_out)
            hc_new, h0_f, h0_b = _hybrid_mixer(hc, mod, ctx_row, norm_mix_g[i], p, zeros, zeros,
                                               ctx_needed)
            h, _, _ = _hybrid_mixer(h, mod, lat_row, norm_mix_g[i], p, h0_f, h0_b, True)
            if ctx_needed:
                hc = hc_new
        else:
            w1 = cf_w_pw1[j].astype(BF16)
            w2 = cf_w_pw2[j].astype(BF16)
            args = (w1, cf_b_pw1[j], cf_w_dw[j], cf_b_dw[j], cf_ln_g[j], cf_ln_b[j], w2,
                    cf_b_pw2[j])
            h = _conformer_call(h, mod, lat_row, norm_mix_g[i], *args, _tile(seq, MLP_TILE))
            if ctx_needed:
                hc = _conformer_call(hc, mod, ctx_row, norm_mix_g[i], *args,
                                     _tile(hc.shape[1], TOKEN_TILE))
        w1 = mlp_w1[i].astype(BF16)
        w2 = mlp_w2[i].astype(BF16)
        h = _mlp_call(h, mod, lat_row, norm_mlp_g[i], w1, w2, final_norm_g, last,
                      _tile(seq, MLP_TILE))
        if ctx_needed:
            hc = _mlp_call(hc, mod, ctx_row, norm_mlp_g[i], w1, w2, final_norm_g, False,
                           _tile(hc.shape[1], MLP_TILE))
    return h
```

```python
import functools

import jax
import jax.numpy as jnp
from jax import lax
from jax.experimental import pallas as pl
from jax.experimental.pallas import tpu as pltpu

F32 = jnp.float32
BF16 = jnp.bfloat16

SSD_HEADS = 16
SSD_HEAD_DIM = 64
SSD_GROUPS = 2
SSD_STATE = 128
SSD_CONV = 5
CHUNK = 128
SG_GROUPS = 8
SG_GROUP_DIM = 128
CONF_CONV = 31
EPS = 1e-6
LOG2E = 1.4426950408889634
N_MOD = 6
MOD_ROWS = 8
HALO = 16
LANES = 128
VMEM_LIMIT = 56 * 1024 * 1024
TOKEN_TILE = 512
MLP_TILE = 1024
CONF_COLS = 256
SCAN_TILE = 1024


def _cparams(*sem):
    return pltpu.CompilerParams(dimension_semantics=sem, vmem_limit_bytes=VMEM_LIMIT)


def _dot(a, b):
    return jnp.dot(a, b, preferred_element_type=F32)


def _sigmoid(x):
    return 1.0 / (1.0 + jnp.exp(-x))


def _silu(x):
    return x * _sigmoid(x)


def _gelu_tanh(x):
    c = 0.7978845608028654
    return x * (0.5 + 0.5 * jnp.tanh(x * (c + (c * 0.044715) * (x * x))))


def _softplus(x):
    return jnp.maximum(x, 0.0) + jnp.log1p(jnp.exp(-jnp.abs(x)))


def _modulate(hf, g, shift, scale):
    ms = jnp.mean(hf * hf, axis=-1, keepdims=True)
    return (hf * lax.rsqrt(ms + EPS)) * (g * (1.0 + scale)) + shift


def _layernorm(x, g, b):
    mu = jnp.mean(x, axis=-1, keepdims=True)
    xc = x - mu
    var = jnp.mean(xc * xc, axis=-1, keepdims=True)
    return xc * lax.rsqrt(var + EPS) * g + b


def _mod_vec(mod_ref, k):
    d = mod_ref.shape[-1] // N_MOD
    return mod_ref[:, k * d:(k + 1) * d]


def _cumsum_rows(x, reverse):
    n = x.shape[0]
    row = lax.broadcasted_iota(jnp.int32, x.shape, 0)
    s = 1
    while s < n:
        if reverse:
            x = x + jnp.where(row < n - s, pltpu.roll(x, n - s, axis=0), 0.0)
        else:
            x = x + jnp.where(row >= s, pltpu.roll(x, s, axis=0), 0.0)
        s *= 2
    return x


def _expand_heads(x, e, terms):
    out = None
    r = x
    for _ in range(terms):
        p = r.astype(BF16)
        t = _dot(p, e)
        out = t if out is None else out + t
        r = r - p.astype(F32)
    return out


def _dwconv_rows(ext, w_ref, bias, start, taps, tm, cols=slice(None)):
    n = ext.shape[0]
    acc = None
    for s in range(8):
        ks = [k for k in range(taps) if (start + k) % 8 == s]
        if not ks:
            continue
        rolled = ext if s == 0 else pltpu.roll(ext, n - s, axis=0)
        for k in ks:
            base = start + k - s
            term = w_ref[k:k + 1, cols] * rolled[base:base + tm]
            acc = term if acc is None else acc + term
    return acc + bias


def _resident(shape):
    return pl.BlockSpec(shape, lambda *_: (0,) * len(shape), pipeline_mode=pl.Buffered(1))


def _ada_kernel(cs_ref, w_ref, b_ref, o_ref):
    s = _silu(cs_ref[...]).astype(BF16)
    o_ref[...] = _dot(s, w_ref[...].astype(BF16)) + b_ref[...]


def _ada_call(cs, ada_w, ada_b):
    depth, d, n = ada_w.shape
    tn = n // 4
    out = pl.pallas_call(
        _ada_kernel,
        out_shape=jax.ShapeDtypeStruct((depth, MOD_ROWS, n), F32),
        grid=(depth, n // tn),
        in_specs=[
            pl.BlockSpec((MOD_ROWS, d), lambda l, j: (0, 0)),
            pl.BlockSpec((None, d, tn), lambda l, j: (l, 0, j)),
            pl.BlockSpec((None, 1, tn), lambda l, j: (l, 0, j)),
        ],
        out_specs=pl.BlockSpec((None, MOD_ROWS, tn), lambda l, j: (l, 0, j)),
        compiler_params=_cparams("parallel", "parallel"),
        name="ada_mod",
    )(cs, ada_w, ada_b.reshape(depth, 1, n))
    return out.reshape(depth * MOD_ROWS, 1, n)


def _halo_specs(tm, n_tiles, d):
    per = tm // HALO
    last = n_tiles * per - 1
    prev = pl.BlockSpec((None, None, HALO, d),
                        lambda b, i: (b, jnp.maximum(i * per - 1, 0), 0, 0))
    nxt = pl.BlockSpec((None, None, HALO, d),
                       lambda b, i: (b, jnp.minimum((i + 1) * per, last), 0, 0))
    return prev, nxt


def _ext_rows(hp_ref, h_ref, hn_ref):
    return jnp.concatenate([hp_ref[...], h_ref[...], hn_ref[...]], axis=0)


def _ext_valid(tm):
    i = pl.program_id(1)
    n = pl.num_programs(1)
    row = lax.broadcasted_iota(jnp.int32, (tm + 2 * HALO, 1), 0)
    return jnp.logical_and(jnp.logical_or(row >= HALO, i > 0),
                           jnp.logical_or(row < HALO + tm, i < n - 1))


def _inproj_kernel(hp_ref, h_ref, hn_ref, mod_ref, g_ref, w_ref, cw_ref, cb_ref, dtb_ref,
                   lng_ref, lnb_ref, zs_ref, xbc_ref, dt_ref, gate_ref, v_ref):
    tm, d = h_ref.shape
    n_xbc = xbc_ref.shape[-1]
    o_xbc = d
    o_uv = o_xbc + n_xbc
    o_dt = o_uv + 2 * d
    ext = _ext_rows(hp_ref, h_ref, hn_ref)
    u = _modulate(ext, g_ref[...], _mod_vec(mod_ref, 0), _mod_vec(mod_ref, 1))
    u = jnp.where(_ext_valid(tm), u, 0.0).astype(BF16)
    um = u[HALO:HALO + tm]

    xe = _dot(u, w_ref[:, o_xbc:o_uv])
    acc = _dwconv_rows(xe, cw_ref, cb_ref[...], HALO - SSD_CONV // 2, SSD_CONV, tm)
    xbc_ref[...] = _silu(acc).astype(BF16)

    zs_ref[...] = _silu(_dot(um, w_ref[:, 0:d])).astype(BF16)

    lane = lax.broadcasted_iota(jnp.int32, (1, LANES), 1)
    dt = _softplus(_dot(um, w_ref[:, o_dt:o_dt + LANES]) + dtb_ref[...])
    dt_ref[...] = jnp.where(lane < 2 * SSD_HEADS, dt, 0.0)

    gate_ref[...] = _gelu_tanh(_dot(um, w_ref[:, o_uv:o_uv + d])).astype(BF16)
    v = _gelu_tanh(_dot(um, w_ref[:, o_uv + d:o_uv + 2 * d]))
    v_ref[...] = _layernorm(v, lng_ref[...], lnb_ref[...]).astype(BF16)


def _inproj_call(h, mod, mod_row, norm_g, w_in_r, conv_w, conv_b, dt_bias_p, ln_g, ln_b, tm):
    bsz, seq, d = h.shape
    n_tiles = seq // tm
    n_xbc = conv_w.shape[-1]
    hv = h.reshape(bsz, seq // HALO, HALO, d)
    prev, nxt = _halo_specs(tm, n_tiles, d)
    const = lambda shape: pl.BlockSpec(shape, lambda b, i: (0,) * len(shape))
    tile = lambda n: pl.BlockSpec((None, tm, n), lambda b, i: (b, i, 0))
    out_shape = (
        jax.ShapeDtypeStruct((bsz, seq, d), BF16),
        jax.ShapeDtypeStruct((bsz, seq, n_xbc), BF16),
        jax.ShapeDtypeStruct((bsz, seq, LANES), F32),
        jax.ShapeDtypeStruct((bsz, seq, d), BF16),
        jax.ShapeDtypeStruct((bsz, seq, d), BF16),
    )
    return pl.pallas_call(
        _inproj_kernel,
        out_shape=out_shape,
        grid=(bsz, n_tiles),
        in_specs=[
            prev, tile(d), nxt,
            pl.BlockSpec((None, 1, mod.shape[-1]), lambda b, i: (mod_row(b), 0, 0)),
            const((1, d)),
            _resident(w_in_r.shape),
            const(conv_w.shape), const((1, n_xbc)), const((1, LANES)),
            const((1, d)), const((1, d)),
        ],
        out_specs=(tile(d), tile(n_xbc), tile(LANES), tile(d), tile(d)),
        compiler_params=_cparams("parallel", "parallel"),
        name="hy_inproj",
    )(hv, h, hv, mod, norm_g.reshape(1, d), w_in_r, conv_w, conv_b.reshape(1, n_xbc),
      dt_bias_p, ln_g.reshape(1, d), ln_b.reshape(1, d))


def _ssd_state_kernel(xbc_ref, dt_ref, alog_ref, e_ref, h0_ref, hin_ref, hfin_ref, hs_ref,
                      xw_ref, *, reverse):
    n_chunks = hin_ref.shape[0]
    d_inner = e_ref.shape[-1]
    gw = d_inner // SSD_GROUPS
    i = pl.program_id(1)

    @pl.when(i == 0)
    def _():
        hs_ref[...] = h0_ref[...]

    a = -jnp.exp(alog_ref[...])
    e = e_ref[...]
    row8 = lax.broadcasted_iota(jnp.int32, (8, LANES), 0)
    ws = []
    edges = jnp.zeros((8, LANES), F32)
    for c in range(n_chunks):
        dt = dt_ref[c * CHUNK:(c + 1) * CHUNK, :]
        cs = _cumsum_rows(dt * a, reverse)
        edge = cs[0:1, :] if reverse else cs[CHUNK - 1:CHUNK, :]
        ws.append(jnp.exp(edge - cs) * dt)
        edges = jnp.where(row8 == c, edge, edges)
    w = _expand_heads(jnp.concatenate(ws, axis=0), e, 1)
    decay = _expand_heads(jnp.exp(edges), e, 3)
    xw_ref[...] = (xbc_ref[:, 0:d_inner].astype(F32) * w).astype(BF16)
    for t in range(n_chunks):
        c = n_chunks - 1 - t if reverse else t
        rows = slice(c * CHUNK, (c + 1) * CHUNK)
        for g in range(SSD_GROUPS):
            bm = xbc_ref[rows, d_inner + g * SSD_STATE:d_inner + (g + 1) * SSD_STATE]
            s = lax.dot_general(bm, xw_ref[rows, g * gw:(g + 1) * gw], (((0,), (0,)), ((), ())),
                                preferred_element_type=F32)
            hcur = hs_ref[g]
            hin_ref[c, g] = hcur.astype(BF16)
            hs_ref[g] = hcur * decay[c:c + 1, g * gw:(g + 1) * gw] + s

    @pl.when(i == pl.num_programs(1) - 1)
    def _():
        hfin_ref[...] = hs_ref[...]


def _ssd_state_call(xbc, dt, alog_p, e, h0, reverse, tq):
    bsz, seq, _ = xbc.shape
    d_inner = e.shape[-1]
    gw = d_inner // SSD_GROUPS
    n_blk = seq // tq
    cpb = tq // CHUNK
    blk = (lambda i: n_blk - 1 - i) if reverse else (lambda i: i)
    n_xb = d_inner + SSD_GROUPS * SSD_STATE
    return pl.pallas_call(
        functools.partial(_ssd_state_kernel, reverse=reverse),
        out_shape=(
            jax.ShapeDtypeStruct((bsz, seq // CHUNK, SSD_GROUPS, SSD_STATE, gw), BF16),
            jax.ShapeDtypeStruct((bsz, SSD_GROUPS, SSD_STATE, gw), F32),
        ),
        grid=(bsz, n_blk),
        in_specs=[
            pl.BlockSpec((None, tq, n_xb), lambda b, i: (b, blk(i), 0)),
            pl.BlockSpec((None, tq, LANES), lambda b, i: (b, blk(i), 0)),
            pl.BlockSpec((1, LANES), lambda b, i: (0, 0)),
            pl.BlockSpec(e.shape, lambda b, i: (0, 0)),
            pl.BlockSpec((None, SSD_GROUPS, SSD_STATE, gw), lambda b, i: (b, 0, 0, 0)),
        ],
        out_specs=(
            pl.BlockSpec((None, cpb, SSD_GROUPS, SSD_STATE, gw), lambda b, i: (b, blk(i), 0, 0, 0)),
            pl.BlockSpec((None, SSD_GROUPS, SSD_STATE, gw), lambda b, i: (b, 0, 0, 0)),
        ),
        scratch_shapes=[pltpu.VMEM((SSD_GROUPS, SSD_STATE, gw), F32),
                        pltpu.VMEM((tq, d_inner), BF16)],
        compiler_params=_cparams("parallel", "arbitrary"),
        name="ssd_state_bwd" if reverse else "ssd_state_fwd",
    )(xbc, dt, alog_p, e, h0)


def _mix_out_kernel(h_ref, xbc_ref, dt_ref, zs_ref, gate_ref, v_ref, hinf_ref, hinb_ref,
                    alog_ref, ef_ref, eb_ref, dskip_ref, ng_ref, sgw_ref, sgb_ref, wout_ref,
                    mod_ref, o_ref, ycat_ref):
    tq, d = h_ref.shape
    n_chunks = tq // CHUNK
    d_inner = ef_ref.shape[-1]
    gw = d_inner // SSD_GROUPS
    hpg = SSD_HEADS // SSD_GROUPS
    a2 = -jnp.exp(alog_ref[...]) * LOG2E
    lane = lax.broadcasted_iota(jnp.int32, (CHUNK, LANES), 1)
    ii = lax.broadcasted_iota(jnp.int32, (CHUNK, CHUNK), 0)
    jj = lax.broadcasted_iota(jnp.int32, (CHUNK, CHUNK), 1)
    low = ii >= jj
    diag = ii == jj
    lane_lo = lane < SSD_HEAD_DIM
    lane2 = lax.broadcasted_iota(jnp.int32, (CHUNK, 2 * LANES), 1)
    lane2_lo = lane2 < LANES

    def chunk(c, carry):
        rows = pl.ds(pl.multiple_of(c * CHUNK, CHUNK), CHUNK)
        dt = dt_ref[rows, :]
        da = dt * a2
        cs = jnp.where(lane < SSD_HEADS, _cumsum_rows(da, False), _cumsum_rows(da, True))
        ecs = jnp.exp2(cs)
        scale_f = _expand_heads(ecs, ef_ref[...], 2)
        scale_b = _expand_heads(ecs, eb_ref[...], 2)
        rw_t = (cs - jnp.log(dt) * LOG2E).T
        xs = xbc_ref[rows, 0:d_inner]
        cms, scs, dsel = [], [], jnp.zeros((CHUNK, LANES), F32)
        for g in range(SSD_GROUPS):
            bm = xbc_ref[rows, d_inner + g * SSD_STATE:d_inner + (g + 1) * SSD_STATE]
            o_c = d_inner + (SSD_GROUPS + g) * SSD_STATE
            cm = xbc_ref[rows, o_c:o_c + SSD_STATE]
            scores = lax.dot_general(cm, bm, (((1,), (1,)), ((), ())),
                                     preferred_element_type=F32)
            dg = jnp.sum(jnp.where(diag, scores, 0.0), axis=1, keepdims=True)
            lo = SSD_HEADS + g * hpg
            dsel = jnp.where(jnp.logical_and(lane >= lo, lane < lo + hpg), dg, dsel)
            cms.append(cm)
            scs.append(scores)
        skip = dskip_ref[...] + _expand_heads(dsel * dt, eb_ref[...], 1)
        for g in range(SSD_GROUPS):
            cm, scores = cms[g], scs[g]
            cols = slice(g * gw, (g + 1) * gw)
            y_g = (_dot(cm, hinf_ref[c, g]) * scale_f[:, cols]
                   + _dot(cm, hinb_ref[c, g]) * scale_b[:, cols])
            ys = []
            for pair in range(hpg // 2):
                ws = []
                for hh in range(2):
                    hd = g * hpg + pair * 2 + hh
                    hb = SSD_HEADS + hd
                    seg = jnp.where(low, cs[:, hd:hd + 1] - rw_t[hd:hd + 1, :],
                                    cs[:, hb:hb + 1] - rw_t[hb:hb + 1, :])
                    ws.append((scores * jnp.exp2(seg)).astype(BF16))
                p0 = (g * hpg + pair * 2) * SSD_HEAD_DIM
                xp = xs[:, p0:p0 + 2 * SSD_HEAD_DIM]
                zero = jnp.zeros_like(xp)
                rhs = jnp.concatenate([jnp.where(lane_lo, xp, zero),
                                       jnp.where(lane_lo, zero, xp)], axis=0)
                ys.append(_dot(jnp.concatenate(ws, axis=1), rhs))
            y_g = y_g + jnp.concatenate(ys, axis=1)
            y_g = y_g + skip[:, cols] * xs[:, cols].astype(F32)
            y_g = y_g * zs_ref[rows, cols].astype(F32)
            ms = jnp.mean(y_g * y_g, axis=-1, keepdims=True)
            y_g = y_g * lax.rsqrt(ms + EPS) * ng_ref[:, cols]
            ycat_ref[rows, cols] = y_g.astype(BF16)
        for sp in range(SG_GROUPS // 2):
            cols = slice(sp * 2 * SG_GROUP_DIM, (sp + 1) * 2 * SG_GROUP_DIM)
            vp = v_ref[rows, cols]
            zero = jnp.zeros_like(vp)
            rhs = jnp.concatenate([jnp.where(lane2_lo, vp, zero),
                                   jnp.where(lane2_lo, zero, vp)], axis=0)
            s = _dot(sgw_ref[sp], rhs) + sgb_ref[:, cols]
            ycat_ref[rows, d_inner + sp * 2 * SG_GROUP_DIM:d_inner + (sp + 1) * 2 * SG_GROUP_DIM] = (
                gate_ref[rows, cols].astype(F32) * s).astype(BF16)
        return carry

    lax.fori_loop(0, n_chunks, chunk, 0)
    out = _dot(ycat_ref[...], wout_ref[...])
    o_ref[...] = h_ref[...] + _mod_vec(mod_ref, 2) * out


def _mix_out_call(h, xbc, dt, zs, gate, v, hin_f, hin_b, alog_p, e_f, e_b, dskip, norm_g,
                  sg_w, sg_b_exp, w_out, mod, mod_row, tq):
    bsz, seq, d = h.shape
    d_inner = e_f.shape[-1]
    gw = d_inner // SSD_GROUPS
    cpb = tq // CHUNK
    const = lambda shape: pl.BlockSpec(shape, lambda b, i: (0,) * len(shape))
    tile = lambda n: pl.BlockSpec((None, tq, n), lambda b, i: (b, i, 0))
    hin_spec = pl.BlockSpec((None, cpb, SSD_GROUPS, SSD_STATE, gw), lambda b, i: (b, i, 0, 0, 0))
    return pl.pallas_call(
        _mix_out_kernel,
        out_shape=jax.ShapeDtypeStruct((bsz, seq, d), F32),
        grid=(bsz, seq // tq),
        in_specs=[
            tile(d), tile(xbc.shape[-1]), tile(LANES), tile(d_inner), tile(d), tile(d),
            hin_spec, hin_spec,
            const((1, LANES)), const(e_f.shape), const(e_b.shape),
            const((1, d_inner)), const((1, d_inner)),
            const(sg_w.shape), const(sg_b_exp.shape), _resident(w_out.shape),
            pl.BlockSpec((None, 1, mod.shape[-1]), lambda b, i: (mod_row(b), 0, 0)),
        ],
        out_specs=tile(d),
        scratch_shapes=[pltpu.VMEM((tq, w_out.shape[0]), BF16)],
        compiler_params=_cparams("parallel", "parallel"),
        name="hy_mix_out",
    )(h, xbc, dt, zs, gate, v, hin_f, hin_b, alog_p, e_f, e_b, dskip, norm_g.reshape(1, d_inner),
      sg_w, sg_b_exp, w_out, mod)


def _conformer_kernel(hp_ref, h_ref, hn_ref, mod_ref, g_ref, w1_ref, b1_ref, wd_ref, bd_ref,
                      lng_ref, lnb_ref, w2_ref, b2_ref, o_ref):
    tm, d = h_ref.shape
    cw = w2_ref.shape[0]
    ext = _ext_rows(hp_ref, h_ref, hn_ref)
    u = _modulate(ext, g_ref[...], _mod_vec(mod_ref, 0), _mod_vec(mod_ref, 1)).astype(BF16)
    valid = _ext_valid(tm)
    accs = []
    for c0 in range(0, cw, CONF_COLS):
        cols = slice(c0, c0 + CONF_COLS)
        gcols = slice(cw + c0, cw + c0 + CONF_COLS)
        a = (_dot(u, w1_ref[:, cols]) + b1_ref[:, cols]) * _sigmoid(
            _dot(u, w1_ref[:, gcols]) + b1_ref[:, gcols])
        a = jnp.where(valid, a, 0.0)
        accs.append(_dwconv_rows(a, wd_ref, bd_ref[:, cols], HALO - CONF_CONV // 2, CONF_CONV, tm,
                                 cols))
    acc = jnp.concatenate(accs, axis=1)
    y = _silu(_layernorm(acc, lng_ref[...], lnb_ref[...])).astype(BF16)
    out = _dot(y, w2_ref[...]) + b2_ref[...]
    o_ref[...] = h_ref[...] + _mod_vec(mod_ref, 2) * out


def _conformer_call(h, mod, mod_row, norm_g, w1, b1, wd, bd, ln_g, ln_b, w2, b2, tm):
    bsz, seq, d = h.shape
    n_tiles = seq // tm
    cw = w2.shape[0]
    hv = h.reshape(bsz, seq // HALO, HALO, d)
    prev, nxt = _halo_specs(tm, n_tiles, d)
    const = lambda shape: pl.BlockSpec(shape, lambda b, i: (0,) * len(shape))
    tile = pl.BlockSpec((None, tm, d), lambda b, i: (b, i, 0))
    return pl.pallas_call(
        _conformer_kernel,
        out_shape=jax.ShapeDtypeStruct((bsz, seq, d), F32),
        grid=(bsz, n_tiles),
        in_specs=[
            prev, tile, nxt,
            pl.BlockSpec((None, 1, mod.shape[-1]), lambda b, i: (mod_row(b), 0, 0)),
            const((1, d)), _resident(w1.shape), const((1, 2 * cw)), const(wd.shape),
            const((1, cw)), const((1, cw)), const((1, cw)), _resident(w2.shape), const((1, d)),
        ],
        out_specs=tile,
        compiler_params=_cparams("parallel", "parallel"),
        name="conformer",
    )(hv, h, hv, mod, norm_g.reshape(1, d), w1, b1.reshape(1, 2 * cw), wd, bd.reshape(1, cw),
      ln_g.reshape(1, cw), ln_b.reshape(1, cw), w2, b2.reshape(1, d))


def _mlp_kernel(h_ref, mod_ref, g_ref, w1_ref, w2_ref, fg_ref, o_ref, a_ref, *, final_norm):
    hf = h_ref[...]
    u = _modulate(hf, g_ref[...], _mod_vec(mod_ref, 3), _mod_vec(mod_ref, 4)).astype(BF16)
    hid = w1_ref.shape[1]
    step = min(hid, 1024)
    for k in range(0, hid, step):
        r = jnp.maximum(_dot(u, w1_ref[:, k:k + step]), 0.0)
        a_ref[:, k:k + step] = (r * r).astype(BF16)
    out = hf + _mod_vec(mod_ref, 5) * _dot(a_ref[...], w2_ref[...])
    if final_norm:
        ms = jnp.mean(out * out, axis=-1, keepdims=True)
        out = out * lax.rsqrt(ms + EPS) * fg_ref[...]
    o_ref[...] = out


def _mlp_call(h, mod, mod_row, norm_g, w1, w2, final_g, final_norm, tm):
    bsz, seq, d = h.shape
    const = lambda shape: pl.BlockSpec(shape, lambda b, i: (0,) * len(shape))
    tile = pl.BlockSpec((None, tm, d), lambda b, i: (b, i, 0))
    return pl.pallas_call(
        functools.partial(_mlp_kernel, final_norm=final_norm),
        out_shape=jax.ShapeDtypeStruct((bsz, seq, d), F32),
        grid=(bsz, seq // tm),
        in_specs=[
            tile,
            pl.BlockSpec((None, 1, mod.shape[-1]), lambda b, i: (mod_row(b), 0, 0)),
            const((1, d)), _resident(w1.shape), _resident(w2.shape), const((1, d)),
        ],
        out_specs=tile,
        scratch_shapes=[pltpu.VMEM((tm, w1.shape[1]), BF16)],
        compiler_params=_cparams("parallel", "parallel"),
        name="mlp_final" if final_norm else "mlp",
    )(h, mod, norm_g.reshape(1, d), w1, w2, final_g.reshape(1, d))


def _conf_mlp_kernel(hp_ref, h_ref, hn_ref, moda_ref, modb_ref, g_ref, w1_ref, b1_ref, wd_ref,
                     bd_ref, lng_ref, lnb_ref, w2_ref, b2_ref, mg_ref, mw1_ref, mw2_ref, fg_ref,
                     o_ref, hmid_ref, *, n_tiles, final_norm):
    s = pl.program_id(0)
    tm, d = h_ref.shape
    cw = w2_ref.shape[0]
    slot = lax.rem(s + 1, 2)

    @pl.when(s == 0)
    def _():
        hmid_ref[...] = jnp.zeros_like(hmid_ref)

    hm = hmid_ref[slot]
    um = _modulate(hm, mg_ref[...], _mod_vec(modb_ref, 3), _mod_vec(modb_ref, 4)).astype(BF16)
    ext = _ext_rows(hp_ref, h_ref, hn_ref)
    uc = _modulate(ext, g_ref[...], _mod_vec(moda_ref, 0), _mod_vec(moda_ref, 1)).astype(BF16)
    i = lax.rem(jnp.minimum(s, pl.num_programs(0) - 2), n_tiles)
    row = lax.broadcasted_iota(jnp.int32, (tm + 2 * HALO, 1), 0)
    valid = jnp.logical_and(jnp.logical_or(row >= HALO, i > 0),
                            jnp.logical_or(row < HALO + tm, i < n_tiles - 1))

    hid = mw1_ref.shape[1]
    n_blk = cw // CONF_COLS
    step = hid // n_blk
    accs = []
    mlp = None
    for j in range(n_blk):
        cols = slice(j * CONF_COLS, (j + 1) * CONF_COLS)
        gcols = slice(cw + j * CONF_COLS, cw + (j + 1) * CONF_COLS)
        r = jnp.maximum(_dot(um, mw1_ref[:, j * step:(j + 1) * step]), 0.0)
        part = _dot((r * r).astype(BF16), mw2_ref[j * step:(j + 1) * step, :])
        mlp = part if mlp is None else mlp + part
        a = (_dot(uc, w1_ref[:, cols]) + b1_ref[:, cols]) * _sigmoid(
            _dot(uc, w1_ref[:, gcols]) + b1_ref[:, gcols])
        a = jnp.where(valid, a, 0.0)
        accs.append(_dwconv_rows(a, wd_ref, bd_ref[:, cols], HALO - CONF_CONV // 2, CONF_CONV, tm,
                                 cols))

    out = hm + _mod_vec(modb_ref, 5) * mlp
    if final_norm:
        ms = jnp.mean(out * out, axis=-1, keepdims=True)
        out = out * lax.rsqrt(ms + EPS) * fg_ref[...]
    o_ref[...] = out

    acc = jnp.concatenate(accs, axis=1)
    y = _silu(_layernorm(acc, lng_ref[...], lnb_ref[...])).astype(BF16)
    outc = _dot(y, w2_ref[...]) + b2_ref[...]
    hmid_ref[lax.rem(s, 2)] = h_ref[...] + _mod_vec(moda_ref, 2) * outc


def _conf_mlp_call(h, mod, mod_row, cg, w1, b1, wd, bd, ln_g, ln_b, w2, b2, mg, mw1, mw2, final_g,
                   final_norm, tm):
    bsz, seq, d = h.shape
    n_tiles = seq // tm
    total = bsz * n_tiles
    cw = w2.shape[0]
    per = tm // HALO
    hv = h.reshape(bsz, seq // HALO, HALO, d)
    cur = lambda s: jnp.minimum(s, total - 1)
    done = lambda s: jnp.maximum(s - 1, 0)
    const = lambda shape: pl.BlockSpec(shape, lambda s: (0,) * len(shape))
    mod_spec = lambda t: pl.BlockSpec((None, 1, mod.shape[-1]),
                                      lambda s: (mod_row(t(s) // n_tiles), 0, 0))
    return pl.pallas_call(
        functools.partial(_conf_mlp_kernel, n_tiles=n_tiles, final_norm=final_norm),
        out_shape=jax.ShapeDtypeStruct((bsz, seq, d), F32),
        grid=(total + 1,),
        in_specs=[
            pl.BlockSpec((None, None, HALO, d), lambda s: (
                cur(s) // n_tiles, jnp.maximum(cur(s) % n_tiles * per - 1, 0), 0, 0)),
            pl.BlockSpec((None, tm, d), lambda s: (cur(s) // n_tiles, cur(s) % n_tiles, 0)),
            pl.BlockSpec((None, None, HALO, d), lambda s: (
                cur(s) // n_tiles, jnp.minimum((cur(s) % n_tiles + 1) * per, n_tiles * per - 1),
                0, 0)),
            mod_spec(cur), mod_spec(done),
            const((1, d)), _resident(w1.shape), const((1, 2 * cw)), const(wd.shape),
            const((1, cw)), const((1, cw)), const((1, cw)), _resident(w2.shape), const((1, d)),
            const((1, d)), _resident(mw1.shape), _resident(mw2.shape), const((1, d)),
        ],
        out_specs=pl.BlockSpec((None, tm, d), lambda s: (done(s) // n_tiles, done(s) % n_tiles, 0)),
        scratch_shapes=[pltpu.VMEM((2, tm, d), F32)],
        compiler_params=_cparams("arbitrary"),
        name="conf_mlp_final" if final_norm else "conf_mlp",
    )(hv, h, hv, mod, mod, cg.reshape(1, d), w1, b1.reshape(1, 2 * cw), wd, bd.reshape(1, cw),
      ln_g.reshape(1, cw), ln_b.reshape(1, cw), w2, b2.reshape(1, d), mg.reshape(1, d), mw1, mw2,
      final_g.reshape(1, d))


def _tile(seq, want):
    return min(seq, want)


def _hybrid_mixer(h, mod, mod_row, norm_g, p, h0_f, h0_b, need_out):
    seq = h.shape[1]
    zs, xbc, dt, gate, v = _inproj_call(
        h, mod, mod_row, norm_g, p["w_in"], p["conv_w"], p["conv_b"], p["dt_bias"],
        p["ln_g"], p["ln_b"], _tile(seq, TOKEN_TILE))
    ts = _tile(seq, SCAN_TILE)
    tq = _tile(seq, TOKEN_TILE)
    hin_f, hfin_f = _ssd_state_call(xbc, dt, p["alog"], p["e_f"], h0_f, False, ts)
    hin_b, hfin_b = _ssd_state_call(xbc, dt, p["alog"], p["e_b"], h0_b, True, ts)
    out = None
    if need_out:
        out = _mix_out_call(h, xbc, dt, zs, gate, v, hin_f, hin_b, p["alog"], p["e_f"], p["e_b"],
                            p["dskip"], p["norm_g"], p["sg_w"], p["sg_b"], p["w_out"], mod,
                            mod_row, tq)
    return out, hfin_f, hfin_b


def _hybrid_params(j, hy_w_in, ssd_conv_w, ssd_conv_b, ssd_dt_bias, ssd_a_log, ssd_d, ssd_norm_g,
                   sg_ln_g, sg_ln_b, sg_w, sg_b, hy_w_out):
    d_inner = SSD_HEADS * SSD_HEAD_DIM
    n_xbc = d_inner + 2 * SSD_GROUPS * SSD_STATE
    w = hy_w_in[j]
    o2 = d_inner + n_xbc
    o3 = o2 + 2 * SSD_HEADS
    pad = jnp.zeros((w.shape[0], LANES - 2 * SSD_HEADS), w.dtype)
    w_in_r = jnp.concatenate([w[:, :o2], w[:, o3:], w[:, o2:o3], pad], axis=1).astype(BF16)
    lane_pad = lambda t: jnp.pad(t.reshape(1, -1), ((0, 0), (0, LANES - t.size)))
    head = jnp.arange(d_inner, dtype=jnp.int32) // SSD_HEAD_DIM
    row = jnp.arange(LANES, dtype=jnp.int32)[:, None]
    sg_width = SG_GROUPS * SG_GROUP_DIM
    sg_b_exp = jnp.broadcast_to(sg_b[j].T[:, :, None],
                                (CHUNK, SG_GROUPS, SG_GROUP_DIM)).reshape(CHUNK, sg_width)
    sg_pairs = sg_w[j].reshape(SG_GROUPS // 2, 2, CHUNK, CHUNK).transpose(0, 2, 1, 3).reshape(
        SG_GROUPS // 2, CHUNK, 2 * CHUNK)
    return dict(
        w_in=w_in_r, conv_w=ssd_conv_w[j], conv_b=ssd_conv_b[j],
        dt_bias=lane_pad(ssd_dt_bias[j]), alog=lane_pad(ssd_a_log[j]),
        e_f=(row == head[None, :]).astype(BF16),
        e_b=(row == head[None, :] + SSD_HEADS).astype(BF16),
        dskip=jnp.repeat(ssd_d[j], SSD_HEAD_DIM).reshape(1, d_inner),
        norm_g=ssd_norm_g[j], ln_g=sg_ln_g[j], ln_b=sg_ln_b[j],
        sg_w=sg_pairs.astype(BF16), sg_b=sg_b_exp, w_out=hy_w_out[j].astype(BF16))


def kernel(x, c, ctx, c_ctx, ada_w, ada_b, norm_mix_g, norm_mlp_g, mlp_w1, mlp_w2, hy_w_in,
           ssd_conv_w, ssd_conv_b, ssd_dt_bias, ssd_a_log, ssd_d, ssd_norm_g, sg_ln_g, sg_ln_b,
           sg_w, sg_b, hy_w_out, cf_w_pw1, cf_b_pw1, cf_w_dw, cf_b_dw, cf_ln_g, cf_ln_b, cf_w_pw2,
           cf_b_pw2, final_norm_g):
    bsz, seq, d = x.shape
    depth = ada_w.shape[0]
    assert bsz < MOD_ROWS and seq % 512 == 0 and ctx.shape[1] % (2 * CHUNK) == 0
    cs = jnp.concatenate([c, c_ctx[None, :], jnp.zeros((MOD_ROWS - bsz - 1, d), c.dtype)], axis=0)
    mod = _ada_call(cs, ada_w, ada_b)

    gw = SSD_HEADS * SSD_HEAD_DIM // SSD_GROUPS
    zeros = jnp.zeros((bsz, SSD_GROUPS, SSD_STATE, gw), F32)
    h, hc = x, ctx
    for i in range(depth):
        ctx_needed = i < depth - 1
        last = i == depth - 1
        lat_row = lambda b, i=i: i * MOD_ROWS + b
        ctx_row = lambda b, i=i: i * MOD_ROWS + bsz
        j = i // 2
        if i % 2 == 0:
            p = _hybrid_params(j, hy_w_in, ssd_conv_w, ssd_conv_b, ssd_dt_bias, ssd_a_log, ssd_d,
                               ssd_norm_g, sg_ln_g, sg_ln_b, sg_w, sg_b, hy_w_out)
            hc_new, h0_f, h0_b = _hybrid_mixer(hc, mod, ctx_row, norm_mix_g[i], p, zeros, zeros,
                                               ctx_needed)
            h, _, _ = _hybrid_mixer(h, mod, lat_row, norm_mix_g[i], p, h0_f, h0_b, True)
            if ctx_needed:
                hc = hc_new
        else:
            w1 = cf_w_pw1[j].astype(BF16)
            w2 = cf_w_pw2[j].astype(BF16)
            args = (w1, cf_b_pw1[j], cf_w_dw[j], cf_b_dw[j], cf_ln_g[j], cf_ln_b[j], w2,
                    cf_b_pw2[j])
            h = _conf_mlp_call(h, mod, lat_row, norm_mix_g[i], *args, norm_mlp_g[i],
                               mlp_w1[i].astype(BF16), mlp_w2[i].astype(BF16), final_norm_g, last,
                               _tile(seq, TOKEN_TILE))
            if ctx_needed:
                hc = _conformer_call(hc, mod, ctx_row, norm_mix_g[i], *args,
                                     _tile(hc.shape[1], TOKEN_TILE))
        w1 = mlp_w1[i].astype(BF16)
        w2 = mlp_w2[i].astype(BF16)
        if i % 2 == 0:
            h = _mlp_call(h, mod, lat_row, norm_mlp_g[i], w1, w2, final_norm_g, last,
                          _tile(seq, MLP_TILE))
        if ctx_needed:
            hc = _mlp_call(hc, mod, ctx_row, norm_mlp_g[i], w1, w2, final_norm_g, False,
                           _tile(hc.shape[1], MLP_TILE))
    return h
```

```python
import functools

import jax
import jax.numpy as jnp
from jax import lax
from jax.experimental import pallas as pl
from jax.experimental.pallas import tpu as pltpu

F32 = jnp.float32
BF16 = jnp.bfloat16

SSD_HEADS = 16
SSD_HEAD_DIM = 64
SSD_GROUPS = 2
SSD_STATE = 128
SSD_CONV = 5
CHUNK = 128
SG_GROUPS = 8
SG_GROUP_DIM = 128
CONF_CONV = 31
EPS = 1e-6
LOG2E = 1.4426950408889634
N_MOD = 6
MOD_ROWS = 8
HALO = 16
LANES = 128
VMEM_LIMIT = 60 * 1024 * 1024
TOKEN_TILE = 512
MLP_TILE = 1024
CONF_COLS = 256
SCAN_TILE = 1024


def _cparams(*sem):
    return pltpu.CompilerParams(dimension_semantics=sem, vmem_limit_bytes=VMEM_LIMIT)


def _dot(a, b):
    return jnp.dot(a, b, preferred_element_type=F32)


def _sigmoid(x):
    return 1.0 / (1.0 + jnp.exp(-x))


def _silu(x):
    return x * _sigmoid(x)


def _gelu_tanh(x):
    c = 0.7978845608028654
    return x * (0.5 + 0.5 * jnp.tanh(x * (c + (c * 0.044715) * (x * x))))


def _softplus(x):
    return jnp.maximum(x, 0.0) + jnp.log1p(jnp.exp(-jnp.abs(x)))


def _modulate(hf, g, shift, scale):
    ms = jnp.mean(hf * hf, axis=-1, keepdims=True)
    return (hf * lax.rsqrt(ms + EPS)) * (g * (1.0 + scale)) + shift


def _layernorm(x, g, b):
    mu = jnp.mean(x, axis=-1, keepdims=True)
    xc = x - mu
    var = jnp.mean(xc * xc, axis=-1, keepdims=True)
    return xc * lax.rsqrt(var + EPS) * g + b


def _mod_vec(mod_ref, k):
    d = mod_ref.shape[-1] // N_MOD
    return mod_ref[:, k * d:(k + 1) * d]


def _cumsum_rows(x, reverse):
    n = x.shape[0]
    row = lax.broadcasted_iota(jnp.int32, x.shape, 0)
    s = 1
    while s < n:
        if reverse:
            x = x + jnp.where(row < n - s, pltpu.roll(x, n - s, axis=0), 0.0)
        else:
            x = x + jnp.where(row >= s, pltpu.roll(x, s, axis=0), 0.0)
        s *= 2
    return x


def _expand_heads(x, e, terms):
    out = None
    r = x
    for _ in range(terms):
        p = r.astype(BF16)
        t = _dot(p, e)
        out = t if out is None else out + t
        r = r - p.astype(F32)
    return out


def _dwconv_rows(ext, w_ref, bias, start, taps, tm, cols=slice(None)):
    n = ext.shape[0]
    acc = None
    for s in range(8):
        ks = [k for k in range(taps) if (start + k) % 8 == s]
        if not ks:
            continue
        rolled = ext if s == 0 else pltpu.roll(ext, n - s, axis=0)
        for k in ks:
            base = start + k - s
            term = w_ref[k:k + 1, cols] * rolled[base:base + tm]
            acc = term if acc is None else acc + term
    return acc + bias


def _resident(shape):
    return pl.BlockSpec(shape, lambda *_: (0,) * len(shape), pipeline_mode=pl.Buffered(1))


def _ada_kernel(cs_ref, w_ref, b_ref, o_ref):
    s = _silu(cs_ref[...]).astype(BF16)
    o_ref[...] = _dot(s, w_ref[...].astype(BF16)) + b_ref[...]


def _ada_call(cs, ada_w, ada_b):
    depth, d, n = ada_w.shape
    tn = n // 4
    out = pl.pallas_call(
        _ada_kernel,
        out_shape=jax.ShapeDtypeStruct((depth, MOD_ROWS, n), F32),
        grid=(depth, n // tn),
        in_specs=[
            pl.BlockSpec((MOD_ROWS, d), lambda l, j: (0, 0)),
            pl.BlockSpec((None, d, tn), lambda l, j: (l, 0, j)),
            pl.BlockSpec((None, 1, tn), lambda l, j: (l, 0, j)),
        ],
        out_specs=pl.BlockSpec((None, MOD_ROWS, tn), lambda l, j: (l, 0, j)),
        compiler_params=_cparams("parallel", "parallel"),
        name="ada_mod",
    )(cs, ada_w, ada_b.reshape(depth, 1, n))
    return out.reshape(depth * MOD_ROWS, 1, n)


def _halo_specs(tm, n_tiles, d):
    per = tm // HALO
    last = n_tiles * per - 1
    prev = pl.BlockSpec((None, None, HALO, d),
                        lambda b, i: (b, jnp.maximum(i * per - 1, 0), 0, 0))
    nxt = pl.BlockSpec((None, None, HALO, d),
                       lambda b, i: (b, jnp.minimum((i + 1) * per, last), 0, 0))
    return prev, nxt


def _ext_rows(hp_ref, h_ref, hn_ref):
    return jnp.concatenate([hp_ref[...], h_ref[...], hn_ref[...]], axis=0)


def _ext_valid(tm):
    i = pl.program_id(1)
    n = pl.num_programs(1)
    row = lax.broadcasted_iota(jnp.int32, (tm + 2 * HALO, 1), 0)
    return jnp.logical_and(jnp.logical_or(row >= HALO, i > 0),
                           jnp.logical_or(row < HALO + tm, i < n - 1))


def _inproj_kernel(hp_ref, h_ref, hn_ref, mod_ref, g_ref, w_ref, cw_ref, cb_ref, dtb_ref,
                   lng_ref, lnb_ref, zs_ref, xbc_ref, dt_ref, gate_ref, v_ref):
    tm, d = h_ref.shape
    n_xbc = xbc_ref.shape[-1]
    o_xbc = d
    o_uv = o_xbc + n_xbc
    o_dt = o_uv + 2 * d
    ext = _ext_rows(hp_ref, h_ref, hn_ref)
    u = _modulate(ext, g_ref[...], _mod_vec(mod_ref, 0), _mod_vec(mod_ref, 1))
    u = jnp.where(_ext_valid(tm), u, 0.0).astype(BF16)
    um = u[HALO:HALO + tm]

    xe = _dot(u, w_ref[:, o_xbc:o_uv])
    acc = _dwconv_rows(xe, cw_ref, cb_ref[...], HALO - SSD_CONV // 2, SSD_CONV, tm)
    xbc_ref[...] = _silu(acc).astype(BF16)

    zs_ref[...] = _silu(_dot(um, w_ref[:, 0:d])).astype(BF16)

    lane = lax.broadcasted_iota(jnp.int32, (1, LANES), 1)
    dt = _softplus(_dot(um, w_ref[:, o_dt:o_dt + LANES]) + dtb_ref[...])
    dt_ref[...] = jnp.where(lane < 2 * SSD_HEADS, dt, 0.0)

    gate_ref[...] = _gelu_tanh(_dot(um, w_ref[:, o_uv:o_uv + d])).astype(BF16)
    v = _gelu_tanh(_dot(um, w_ref[:, o_uv + d:o_uv + 2 * d]))
    v_ref[...] = _layernorm(v, lng_ref[...], lnb_ref[...]).astype(BF16)


def _inproj_call(h, mod, mod_row, norm_g, w_in_r, conv_w, conv_b, dt_bias_p, ln_g, ln_b, tm):
    bsz, seq, d = h.shape
    n_tiles = seq // tm
    n_xbc = conv_w.shape[-1]
    hv = h.reshape(bsz, seq // HALO, HALO, d)
    prev, nxt = _halo_specs(tm, n_tiles, d)
    const = lambda shape: pl.BlockSpec(shape, lambda b, i: (0,) * len(shape))
    tile = lambda n: pl.BlockSpec((None, tm, n), lambda b, i: (b, i, 0))
    out_shape = (
        jax.ShapeDtypeStruct((bsz, seq, d), BF16),
        jax.ShapeDtypeStruct((bsz, seq, n_xbc), BF16),
        jax.ShapeDtypeStruct((bsz, seq, LANES), F32),
        jax.ShapeDtypeStruct((bsz, seq, d), BF16),
        jax.ShapeDtypeStruct((bsz, seq, d), BF16),
    )
    return pl.pallas_call(
        _inproj_kernel,
        out_shape=out_shape,
        grid=(bsz, n_tiles),
        in_specs=[
            prev, tile(d), nxt,
            pl.BlockSpec((None, 1, mod.shape[-1]), lambda b, i: (mod_row(b), 0, 0)),
            const((1, d)),
            _resident(w_in_r.shape),
            const(conv_w.shape), const((1, n_xbc)), const((1, LANES)),
            const((1, d)), const((1, d)),
        ],
        out_specs=(tile(d), tile(n_xbc), tile(LANES), tile(d), tile(d)),
        compiler_params=_cparams("parallel", "parallel"),
        name="hy_inproj",
    )(hv, h, hv, mod, norm_g.reshape(1, d), w_in_r, conv_w, conv_b.reshape(1, n_xbc),
      dt_bias_p, ln_g.reshape(1, d), ln_b.reshape(1, d))


def _ssd_state_kernel(xbc_ref, dt_ref, alog_ref, e_ref, h0_ref, hin_ref, hfin_ref, hs_ref,
                      xw_ref, *, reverse):
    n_chunks = hin_ref.shape[0]
    d_inner = e_ref.shape[-1]
    gw = d_inner // SSD_GROUPS
    i = pl.program_id(1)

    @pl.when(i == 0)
    def _():
        hs_ref[...] = h0_ref[...]

    a = -jnp.exp(alog_ref[...])
    e = e_ref[...]
    row8 = lax.broadcasted_iota(jnp.int32, (8, LANES), 0)
    ws = []
    edges = jnp.zeros((8, LANES), F32)
    for c in range(n_chunks):
        dt = dt_ref[c * CHUNK:(c + 1) * CHUNK, :]
        cs = _cumsum_rows(dt * a, reverse)
        edge = cs[0:1, :] if reverse else cs[CHUNK - 1:CHUNK, :]
        ws.append(jnp.exp(edge - cs) * dt)
        edges = jnp.where(row8 == c, edge, edges)
    w = _expand_heads(jnp.concatenate(ws, axis=0), e, 1)
    decay = _expand_heads(jnp.exp(edges), e, 3)
    xw_ref[...] = (xbc_ref[:, 0:d_inner].astype(F32) * w).astype(BF16)
    for t in range(n_chunks):
        c = n_chunks - 1 - t if reverse else t
        rows = slice(c * CHUNK, (c + 1) * CHUNK)
        for g in range(SSD_GROUPS):
            bm = xbc_ref[rows, d_inner + g * SSD_STATE:d_inner + (g + 1) * SSD_STATE]
            s = lax.dot_general(bm, xw_ref[rows, g * gw:(g + 1) * gw], (((0,), (0,)), ((), ())),
                                preferred_element_type=F32)
            hcur = hs_ref[g]
            hin_ref[c, g] = hcur.astype(BF16)
            hs_ref[g] = hcur * decay[c:c + 1, g * gw:(g + 1) * gw] + s

    @pl.when(i == pl.num_programs(1) - 1)
    def _():
        hfin_ref[...] = hs_ref[...]


def _ssd_state_call(xbc, dt, alog_p, e, h0, reverse, tq):
    bsz, seq, _ = xbc.shape
    d_inner = e.shape[-1]
    gw = d_inner // SSD_GROUPS
    n_blk = seq // tq
    cpb = tq // CHUNK
    blk = (lambda i: n_blk - 1 - i) if reverse else (lambda i: i)
    n_xb = d_inner + SSD_GROUPS * SSD_STATE
    return pl.pallas_call(
        functools.partial(_ssd_state_kernel, reverse=reverse),
        out_shape=(
            jax.ShapeDtypeStruct((bsz, seq // CHUNK, SSD_GROUPS, SSD_STATE, gw), BF16),
            jax.ShapeDtypeStruct((bsz, SSD_GROUPS, SSD_STATE, gw), F32),
        ),
        grid=(bsz, n_blk),
        in_specs=[
            pl.BlockSpec((None, tq, n_xb), lambda b, i: (b, blk(i), 0)),
            pl.BlockSpec((None, tq, LANES), lambda b, i: (b, blk(i), 0)),
            pl.BlockSpec((1, LANES), lambda b, i: (0, 0)),
            pl.BlockSpec(e.shape, lambda b, i: (0, 0)),
            pl.BlockSpec((None, SSD_GROUPS, SSD_STATE, gw), lambda b, i: (b, 0, 0, 0)),
        ],
        out_specs=(
            pl.BlockSpec((None, cpb, SSD_GROUPS, SSD_STATE, gw), lambda b, i: (b, blk(i), 0, 0, 0)),
            pl.BlockSpec((None, SSD_GROUPS, SSD_STATE, gw), lambda b, i: (b, 0, 0, 0)),
        ),
        scratch_shapes=[pltpu.VMEM((SSD_GROUPS, SSD_STATE, gw), F32),
                        pltpu.VMEM((tq, d_inner), BF16)],
        compiler_params=_cparams("parallel", "arbitrary"),
        name="ssd_state_bwd" if reverse else "ssd_state_fwd",
    )(xbc, dt, alog_p, e, h0)


def _mix_tile(xbc_ref, dt_ref, zs_ref, gate_ref, v_ref, hinf_ref, hinb_ref, alog_ref, ef_ref,
              eb_ref, dskip_ref, ng_ref, sgw_ref, sgb_ref, wout_ref, ycat_ref, per_chunk=None):
    tq = dt_ref.shape[0]
    n_chunks = tq // CHUNK
    d_inner = ef_ref.shape[-1]
    gw = d_inner // SSD_GROUPS
    hpg = SSD_HEADS // SSD_GROUPS
    a2 = -jnp.exp(alog_ref[...]) * LOG2E
    lane = lax.broadcasted_iota(jnp.int32, (CHUNK, LANES), 1)
    ii = lax.broadcasted_iota(jnp.int32, (CHUNK, CHUNK), 0)
    jj = lax.broadcasted_iota(jnp.int32, (CHUNK, CHUNK), 1)
    low = ii >= jj
    diag = ii == jj
    lane_lo = lane < SSD_HEAD_DIM
    lane2 = lax.broadcasted_iota(jnp.int32, (CHUNK, 2 * LANES), 1)
    lane2_lo = lane2 < LANES

    def chunk(c, carry):
        r0 = c * CHUNK if isinstance(c, int) else pl.multiple_of(c * CHUNK, CHUNK)
        rows = pl.ds(r0, CHUNK)
        dt = dt_ref[rows, :]
        da = dt * a2
        cs = jnp.where(lane < SSD_HEADS, _cumsum_rows(da, False), _cumsum_rows(da, True))
        ecs = jnp.exp2(cs)
        scale_f = _expand_heads(ecs, ef_ref[...], 1)
        scale_b = _expand_heads(ecs, eb_ref[...], 1)
        rw_t = (cs - jnp.log(dt) * LOG2E).T
        xs = xbc_ref[rows, 0:d_inner]
        cms, scs, dsel = [], [], jnp.zeros((CHUNK, LANES), F32)
        for g in range(SSD_GROUPS):
            bm = xbc_ref[rows, d_inner + g * SSD_STATE:d_inner + (g + 1) * SSD_STATE]
            o_c = d_inner + (SSD_GROUPS + g) * SSD_STATE
            cm = xbc_ref[rows, o_c:o_c + SSD_STATE]
            scores = lax.dot_general(cm, bm, (((1,), (1,)), ((), ())),
                                     preferred_element_type=F32)
            dg = jnp.sum(jnp.where(diag, scores, 0.0), axis=1, keepdims=True)
            lo = SSD_HEADS + g * hpg
            dsel = jnp.where(jnp.logical_and(lane >= lo, lane < lo + hpg), dg, dsel)
            cms.append(cm)
            scs.append(scores)
        skip = dskip_ref[...] + _expand_heads(dsel * dt, eb_ref[...], 1)
        for g in range(SSD_GROUPS):
            cm, scores = cms[g], scs[g]
            cols = slice(g * gw, (g + 1) * gw)
            y_g = (_dot(cm, hinf_ref[c, g]) * scale_f[:, cols]
                   + _dot(cm, hinb_ref[c, g]) * scale_b[:, cols])
            ys = []
            for pair in range(hpg // 2):
                ws = []
                for hh in range(2):
                    hd = g * hpg + pair * 2 + hh
                    hb = SSD_HEADS + hd
                    seg = jnp.where(low, cs[:, hd:hd + 1] - rw_t[hd:hd + 1, :],
                                    cs[:, hb:hb + 1] - rw_t[hb:hb + 1, :])
                    ws.append((scores * jnp.exp2(seg)).astype(BF16))
                p0 = (g * hpg + pair * 2) * SSD_HEAD_DIM
                xp = xs[:, p0:p0 + 2 * SSD_HEAD_DIM]
                zero = jnp.zeros_like(xp)
                rhs = jnp.concatenate([jnp.where(lane_lo, xp, zero),
                                       jnp.where(lane_lo, zero, xp)], axis=0)
                ys.append(_dot(jnp.concatenate(ws, axis=1), rhs))
            y_g = y_g + jnp.concatenate(ys, axis=1)
            y_g = y_g + skip[:, cols] * xs[:, cols].astype(F32)
            y_g = y_g * zs_ref[rows, cols].astype(F32)
            ms = jnp.mean(y_g * y_g, axis=-1, keepdims=True)
            y_g = y_g * lax.rsqrt(ms + EPS) * ng_ref[:, cols]
            ycat_ref[rows, cols] = y_g.astype(BF16)
        for sp in range(SG_GROUPS // 2):
            cols = slice(sp * 2 * SG_GROUP_DIM, (sp + 1) * 2 * SG_GROUP_DIM)
            vp = v_ref[rows, cols]
            zero = jnp.zeros_like(vp)
            rhs = jnp.concatenate([jnp.where(lane2_lo, vp, zero),
                                   jnp.where(lane2_lo, zero, vp)], axis=0)
            s = _dot(sgw_ref[sp], rhs) + sgb_ref[:, cols]
            ycat_ref[rows, d_inner + sp * 2 * SG_GROUP_DIM:d_inner + (sp + 1) * 2 * SG_GROUP_DIM] = (
                gate_ref[rows, cols].astype(F32) * s).astype(BF16)
        return carry

    if per_chunk is None:
        lax.fori_loop(0, n_chunks, chunk, 0)
    else:
        for c in range(n_chunks):
            per_chunk(c)
            chunk(c, 0)
    return _dot(ycat_ref[...], wout_ref[...])


def _mix_out_kernel(h_ref, xbc_ref, dt_ref, zs_ref, gate_ref, v_ref, hinf_ref, hinb_ref,
                    alog_ref, ef_ref, eb_ref, dskip_ref, ng_ref, sgw_ref, sgb_ref, wout_ref,
                    mod_ref, o_ref, ycat_ref):
    out = _mix_tile(xbc_ref, dt_ref, zs_ref, gate_ref, v_ref, hinf_ref, hinb_ref, alog_ref, ef_ref,
                    eb_ref, dskip_ref, ng_ref, sgw_ref, sgb_ref, wout_ref, ycat_ref)
    o_ref[...] = h_ref[...] + _mod_vec(mod_ref, 2) * out


def _mix_mlp_kernel(h_ref, xbc_ref, dt_ref, zs_ref, gate_ref, v_ref, hinf_ref, hinb_ref,
                    alog_ref, ef_ref, eb_ref, dskip_ref, ng_ref, sgw_ref, sgb_ref, wout_ref,
                    moda_ref, modb_ref, mg_ref, mw1_ref, mw2_ref, fg_ref, o_ref, ycat_ref,
                    hmid_ref, *, final_norm):
    s = pl.program_id(0)
    slot = lax.rem(s + 1, 2)
    n_chunks = dt_ref.shape[0] // CHUNK
    step = mw1_ref.shape[1] // n_chunks

    @pl.when(s == 0)
    def _():
        hmid_ref[...] = jnp.zeros_like(hmid_ref)

    hm = hmid_ref[slot]
    um = _modulate(hm, mg_ref[...], _mod_vec(modb_ref, 3), _mod_vec(modb_ref, 4)).astype(BF16)
    parts = []

    def mlp_block(c):
        r = jnp.maximum(_dot(um, mw1_ref[:, c * step:(c + 1) * step]), 0.0)
        parts.append(_dot((r * r).astype(BF16), mw2_ref[c * step:(c + 1) * step, :]))

    out = _mix_tile(xbc_ref, dt_ref, zs_ref, gate_ref, v_ref, hinf_ref, hinb_ref, alog_ref, ef_ref,
                    eb_ref, dskip_ref, ng_ref, sgw_ref, sgb_ref, wout_ref, ycat_ref,
                    per_chunk=mlp_block)
    mlp = parts[0]
    for part in parts[1:]:
        mlp = mlp + part
    res = hm + _mod_vec(modb_ref, 5) * mlp
    if final_norm:
        ms = jnp.mean(res * res, axis=-1, keepdims=True)
        res = res * lax.rsqrt(ms + EPS) * fg_ref[...]
    o_ref[...] = res
    hmid_ref[lax.rem(s, 2)] = h_ref[...] + _mod_vec(moda_ref, 2) * out


def _mix_out_call(h, xbc, dt, zs, gate, v, hin_f, hin_b, alog_p, e_f, e_b, dskip, norm_g,
                  sg_w, sg_b_exp, w_out, mod, mod_row, tq):
    bsz, seq, d = h.shape
    d_inner = e_f.shape[-1]
    gw = d_inner // SSD_GROUPS
    cpb = tq // CHUNK
    const = lambda shape: pl.BlockSpec(shape, lambda b, i: (0,) * len(shape))
    tile = lambda n: pl.BlockSpec((None, tq, n), lambda b, i: (b, i, 0))
    hin_spec = pl.BlockSpec((None, cpb, SSD_GROUPS, SSD_STATE, gw), lambda b, i: (b, i, 0, 0, 0))
    return pl.pallas_call(
        _mix_out_kernel,
        out_shape=jax.ShapeDtypeStruct((bsz, seq, d), F32),
        grid=(bsz, seq // tq),
        in_specs=[
            tile(d), tile(xbc.shape[-1]), tile(LANES), tile(d_inner), tile(d), tile(d),
            hin_spec, hin_spec,
            const((1, LANES)), const(e_f.shape), const(e_b.shape),
            const((1, d_inner)), const((1, d_inner)),
            const(sg_w.shape), const(sg_b_exp.shape), _resident(w_out.shape),
            pl.BlockSpec((None, 1, mod.shape[-1]), lambda b, i: (mod_row(b), 0, 0)),
        ],
        out_specs=tile(d),
        scratch_shapes=[pltpu.VMEM((tq, w_out.shape[0]), BF16)],
        compiler_params=_cparams("parallel", "parallel"),
        name="hy_mix_out",
    )(h, xbc, dt, zs, gate, v, hin_f, hin_b, alog_p, e_f, e_b, dskip, norm_g.reshape(1, d_inner),
      sg_w, sg_b_exp, w_out, mod)


def _mix_mlp_call(h, xbc, dt, zs, gate, v, hin_f, hin_b, alog_p, e_f, e_b, dskip, norm_g,
                  sg_w, sg_b_exp, w_out, mod, mod_row, mg, mw1, mw2, final_g, final_norm, tq):
    bsz, seq, d = h.shape
    d_inner = e_f.shape[-1]
    gw = d_inner // SSD_GROUPS
    cpb = tq // CHUNK
    n_tiles = seq // tq
    total = bsz * n_tiles
    cur = lambda s: jnp.minimum(s, total - 1)
    done = lambda s: jnp.maximum(s - 1, 0)
    const = lambda shape: pl.BlockSpec(shape, lambda s: (0,) * len(shape))
    tile = lambda n: pl.BlockSpec((None, tq, n), lambda s: (cur(s) // n_tiles, cur(s) % n_tiles, 0))
    hin_spec = pl.BlockSpec((None, cpb, SSD_GROUPS, SSD_STATE, gw),
                            lambda s: (cur(s) // n_tiles, cur(s) % n_tiles, 0, 0, 0))
    mod_spec = lambda t: pl.BlockSpec((None, 1, mod.shape[-1]),
                                      lambda s: (mod_row(t(s) // n_tiles), 0, 0))
    return pl.pallas_call(
        functools.partial(_mix_mlp_kernel, final_norm=final_norm),
        out_shape=jax.ShapeDtypeStruct((bsz, seq, d), F32),
        grid=(total + 1,),
        in_specs=[
            tile(d), tile(xbc.shape[-1]), tile(LANES), tile(d_inner), tile(d), tile(d),
            hin_spec, hin_spec,
            const((1, LANES)), const(e_f.shape), const(e_b.shape),
            const((1, d_inner)), const((1, d_inner)),
            const(sg_w.shape), const(sg_b_exp.shape), _resident(w_out.shape),
            mod_spec(cur), mod_spec(done),
            const((1, d)), _resident(mw1.shape), _resident(mw2.shape), const((1, d)),
        ],
        out_specs=pl.BlockSpec((None, tq, d), lambda s: (done(s) // n_tiles, done(s) % n_tiles, 0)),
        scratch_shapes=[pltpu.VMEM((tq, w_out.shape[0]), BF16), pltpu.VMEM((2, tq, d), F32)],
        compiler_params=_cparams("arbitrary"),
        name="mix_mlp_final" if final_norm else "mix_mlp",
    )(h, xbc, dt, zs, gate, v, hin_f, hin_b, alog_p, e_f, e_b, dskip, norm_g.reshape(1, d_inner),
      sg_w, sg_b_exp, w_out, mod, mod, mg.reshape(1, d), mw1, mw2, final_g.reshape(1, d))


def _conformer_kernel(hp_ref, h_ref, hn_ref, mod_ref, g_ref, w1_ref, b1_ref, wd_ref, bd_ref,
                      lng_ref, lnb_ref, w2_ref, b2_ref, o_ref):
    tm, d = h_ref.shape
    cw = w2_ref.shape[0]
    ext = _ext_rows(hp_ref, h_ref, hn_ref)
    u = _modulate(ext, g_ref[...], _mod_vec(mod_ref, 0), _mod_vec(mod_ref, 1)).astype(BF16)
    valid = _ext_valid(tm)
    accs = []
    for c0 in range(0, cw, CONF_COLS):
        cols = slice(c0, c0 + CONF_COLS)
        gcols = slice(cw + c0, cw + c0 + CONF_COLS)
        a = (_dot(u, w1_ref[:, cols]) + b1_ref[:, cols]) * _sigmoid(
            _dot(u, w1_ref[:, gcols]) + b1_ref[:, gcols])
        a = jnp.where(valid, a, 0.0)
        accs.append(_dwconv_rows(a, wd_ref, bd_ref[:, cols], HALO - CONF_CONV // 2, CONF_CONV, tm,
                                 cols))
    acc = jnp.concatenate(accs, axis=1)
    y = _silu(_layernorm(acc, lng_ref[...], lnb_ref[...])).astype(BF16)
    out = _dot(y, w2_ref[...]) + b2_ref[...]
    o_ref[...] = h_ref[...] + _mod_vec(mod_ref, 2) * out


def _conformer_call(h, mod, mod_row, norm_g, w1, b1, wd, bd, ln_g, ln_b, w2, b2, tm):
    bsz, seq, d = h.shape
    n_tiles = seq // tm
    cw = w2.shape[0]
    hv = h.reshape(bsz, seq // HALO, HALO, d)
    prev, nxt = _halo_specs(tm, n_tiles, d)
    const = lambda shape: pl.BlockSpec(shape, lambda b, i: (0,) * len(shape))
    tile = pl.BlockSpec((None, tm, d), lambda b, i: (b, i, 0))
    return pl.pallas_call(
        _conformer_kernel,
        out_shape=jax.ShapeDtypeStruct((bsz, seq, d), F32),
        grid=(bsz, n_tiles),
        in_specs=[
            prev, tile, nxt,
            pl.BlockSpec((None, 1, mod.shape[-1]), lambda b, i: (mod_row(b), 0, 0)),
            const((1, d)), _resident(w1.shape), const((1, 2 * cw)), const(wd.shape),
            const((1, cw)), const((1, cw)), const((1, cw)), _resident(w2.shape), const((1, d)),
        ],
        out_specs=tile,
        compiler_params=_cparams("parallel", "parallel"),
        name="conformer",
    )(hv, h, hv, mod, norm_g.reshape(1, d), w1, b1.reshape(1, 2 * cw), wd, bd.reshape(1, cw),
      ln_g.reshape(1, cw), ln_b.reshape(1, cw), w2, b2.reshape(1, d))


def _mlp_kernel(h_ref, mod_ref, g_ref, w1_ref, w2_ref, fg_ref, o_ref, a_ref, *, final_norm):
    hf = h_ref[...]
    u = _modulate(hf, g_ref[...], _mod_vec(mod_ref, 3), _mod_vec(mod_ref, 4)).astype(BF16)
    hid = w1_ref.shape[1]
    step = min(hid, 1024)
    for k in range(0, hid, step):
        r = jnp.maximum(_dot(u, w1_ref[:, k:k + step]), 0.0)
        a_ref[:, k:k + step] = (r * r).astype(BF16)
    out = hf + _mod_vec(mod_ref, 5) * _dot(a_ref[...], w2_ref[...])
    if final_norm:
        ms = jnp.mean(out * out, axis=-1, keepdims=True)
        out = out * lax.rsqrt(ms + EPS) * fg_ref[...]
    o_ref[...] = out


def _mlp_call(h, mod, mod_row, norm_g, w1, w2, final_g, final_norm, tm):
    bsz, seq, d = h.shape
    const = lambda shape: pl.BlockSpec(shape, lambda b, i: (0,) * len(shape))
    tile = pl.BlockSpec((None, tm, d), lambda b, i: (b, i, 0))
    return pl.pallas_call(
        functools.partial(_mlp_kernel, final_norm=final_norm),
        out_shape=jax.ShapeDtypeStruct((bsz, seq, d), F32),
        grid=(bsz, seq // tm),
        in_specs=[
            tile,
            pl.BlockSpec((None, 1, mod.shape[-1]), lambda b, i: (mod_row(b), 0, 0)),
            const((1, d)), _resident(w1.shape), _resident(w2.shape), const((1, d)),
        ],
        out_specs=tile,
        scratch_shapes=[pltpu.VMEM((tm, w1.shape[1]), BF16)],
        compiler_params=_cparams("parallel", "parallel"),
        name="mlp_final" if final_norm else "mlp",
    )(h, mod, norm_g.reshape(1, d), w1, w2, final_g.reshape(1, d))


def _conf_mlp_kernel(hp_ref, h_ref, hn_ref, moda_ref, modb_ref, g_ref, w1_ref, b1_ref, wd_ref,
                     bd_ref, lng_ref, lnb_ref, w2_ref, b2_ref, mg_ref, mw1_ref, mw2_ref, fg_ref,
                     o_ref, hmid_ref, *, n_tiles, final_norm):
    s = pl.program_id(0)
    tm, d = h_ref.shape
    cw = w2_ref.shape[0]
    slot = lax.rem(s + 1, 2)

    @pl.when(s == 0)
    def _():
        hmid_ref[...] = jnp.zeros_like(hmid_ref)

    hm = hmid_ref[slot]
    um = _modulate(hm, mg_ref[...], _mod_vec(modb_ref, 3), _mod_vec(modb_ref, 4)).astype(BF16)
    ext = _ext_rows(hp_ref, h_ref, hn_ref)
    uc = _modulate(ext, g_ref[...], _mod_vec(moda_ref, 0), _mod_vec(moda_ref, 1)).astype(BF16)
    i = lax.rem(jnp.minimum(s, pl.num_programs(0) - 2), n_tiles)
    row = lax.broadcasted_iota(jnp.int32, (tm + 2 * HALO, 1), 0)
    valid = jnp.logical_and(jnp.logical_or(row >= HALO, i > 0),
                            jnp.logical_or(row < HALO + tm, i < n_tiles - 1))

    hid = mw1_ref.shape[1]
    n_blk = cw // CONF_COLS
    step = hid // n_blk
    accs = []
    mlp = None
    for j in range(n_blk):
        cols = slice(j * CONF_COLS, (j + 1) * CONF_COLS)
        gcols = slice(cw + j * CONF_COLS, cw + (j + 1) * CONF_COLS)
        r = jnp.maximum(_dot(um, mw1_ref[:, j * step:(j + 1) * step]), 0.0)
        part = _dot((r * r).astype(BF16), mw2_ref[j * step:(j + 1) * step, :])
        mlp = part if mlp is None else mlp + part
        a = (_dot(uc, w1_ref[:, cols]) + b1_ref[:, cols]) * _sigmoid(
            _dot(uc, w1_ref[:, gcols]) + b1_ref[:, gcols])
        a = jnp.where(valid, a, 0.0)
        accs.append(_dwconv_rows(a, wd_ref, bd_ref[:, cols], HALO - CONF_CONV // 2, CONF_CONV, tm,
                                 cols))

    out = hm + _mod_vec(modb_ref, 5) * mlp
    if final_norm:
        ms = jnp.mean(out * out, axis=-1, keepdims=True)
        out = out * lax.rsqrt(ms + EPS) * fg_ref[...]
    o_ref[...] = out

    acc = jnp.concatenate(accs, axis=1)
    y = _silu(_layernorm(acc, lng_ref[...], lnb_ref[...])).astype(BF16)
    outc = _dot(y, w2_ref[...]) + b2_ref[...]
    hmid_ref[lax.rem(s, 2)] = h_ref[...] + _mod_vec(moda_ref, 2) * outc


def _conf_mlp_call(h, mod, mod_row, cg, w1, b1, wd, bd, ln_g, ln_b, w2, b2, mg, mw1, mw2, final_g,
                   final_norm, tm):
    bsz, seq, d = h.shape
    n_tiles = seq // tm
    total = bsz * n_tiles
    cw = w2.shape[0]
    per = tm // HALO
    hv = h.reshape(bsz, seq // HALO, HALO, d)
    cur = lambda s: jnp.minimum(s, total - 1)
    done = lambda s: jnp.maximum(s - 1, 0)
    const = lambda shape: pl.BlockSpec(shape, lambda s: (0,) * len(shape))
    mod_spec = lambda t: pl.BlockSpec((None, 1, mod.shape[-1]),
                                      lambda s: (mod_row(t(s) // n_tiles), 0, 0))
    return pl.pallas_call(
        functools.partial(_conf_mlp_kernel, n_tiles=n_tiles, final_norm=final_norm),
        out_shape=jax.ShapeDtypeStruct((bsz, seq, d), F32),
        grid=(total + 1,),
        in_specs=[
            pl.BlockSpec((None, None, HALO, d), lambda s: (
                cur(s) // n_tiles, jnp.maximum(cur(s) % n_tiles * per - 1, 0), 0, 0)),
            pl.BlockSpec((None, tm, d), lambda s: (cur(s) // n_tiles, cur(s) % n_tiles, 0)),
            pl.BlockSpec((None, None, HALO, d), lambda s: (
                cur(s) // n_tiles, jnp.minimum((cur(s) % n_tiles + 1) * per, n_tiles * per - 1),
                0, 0)),
            mod_spec(cur), mod_spec(done),
            const((1, d)), _resident(w1.shape), const((1, 2 * cw)), const(wd.shape),
            const((1, cw)), const((1, cw)), const((1, cw)), _resident(w2.shape), const((1, d)),
            const((1, d)), _resident(mw1.shape), _resident(mw2.shape), const((1, d)),
        ],
        out_specs=pl.BlockSpec((None, tm, d), lambda s: (done(s) // n_tiles, done(s) % n_tiles, 0)),
        scratch_shapes=[pltpu.VMEM((2, tm, d), F32)],
        compiler_params=_cparams("arbitrary"),
        name="conf_mlp_final" if final_norm else "conf_mlp",
    )(hv, h, hv, mod, mod, cg.reshape(1, d), w1, b1.reshape(1, 2 * cw), wd, bd.reshape(1, cw),
      ln_g.reshape(1, cw), ln_b.reshape(1, cw), w2, b2.reshape(1, d), mg.reshape(1, d), mw1, mw2,
      final_g.reshape(1, d))


def _tile(seq, want):
    return min(seq, want)


def _hybrid_mixer(h, mod, mod_row, norm_g, p, h0_f, h0_b, need_out, mlp=None):
    seq = h.shape[1]
    zs, xbc, dt, gate, v = _inproj_call(
        h, mod, mod_row, norm_g, p["w_in"], p["conv_w"], p["conv_b"], p["dt_bias"],
        p["ln_g"], p["ln_b"], _tile(seq, TOKEN_TILE))
    ts = _tile(seq, SCAN_TILE)
    tq = _tile(seq, TOKEN_TILE)
    hin_f, hfin_f = _ssd_state_call(xbc, dt, p["alog"], p["e_f"], h0_f, False, ts)
    hin_b, hfin_b = _ssd_state_call(xbc, dt, p["alog"], p["e_b"], h0_b, True, ts)
    out = None
    args = (h, xbc, dt, zs, gate, v, hin_f, hin_b, p["alog"], p["e_f"], p["e_b"], p["dskip"],
            p["norm_g"], p["sg_w"], p["sg_b"], p["w_out"], mod, mod_row)
    if mlp is not None:
        out = _mix_mlp_call(*args, *mlp, tq)
    elif need_out:
        out = _mix_out_call(*args, tq)
    return out, hfin_f, hfin_b


def _hybrid_params(j, hy_w_in, ssd_conv_w, ssd_conv_b, ssd_dt_bias, ssd_a_log, ssd_d, ssd_norm_g,
                   sg_ln_g, sg_ln_b, sg_w, sg_b, hy_w_out):
    d_inner = SSD_HEADS * SSD_HEAD_DIM
    n_xbc = d_inner + 2 * SSD_GROUPS * SSD_STATE
    w = hy_w_in[j]
    o2 = d_inner + n_xbc
    o3 = o2 + 2 * SSD_HEADS
    pad = jnp.zeros((w.shape[0], LANES - 2 * SSD_HEADS), w.dtype)
    w_in_r = jnp.concatenate([w[:, :o2], w[:, o3:], w[:, o2:o3], pad], axis=1).astype(BF16)
    lane_pad = lambda t: jnp.pad(t.reshape(1, -1), ((0, 0), (0, LANES - t.size)))
    head = jnp.arange(d_inner, dtype=jnp.int32) // SSD_HEAD_DIM
    row = jnp.arange(LANES, dtype=jnp.int32)[:, None]
    sg_width = SG_GROUPS * SG_GROUP_DIM
    sg_b_exp = jnp.broadcast_to(sg_b[j].T[:, :, None],
                                (CHUNK, SG_GROUPS, SG_GROUP_DIM)).reshape(CHUNK, sg_width)
    sg_pairs = sg_w[j].reshape(SG_GROUPS // 2, 2, CHUNK, CHUNK).transpose(0, 2, 1, 3).reshape(
        SG_GROUPS // 2, CHUNK, 2 * CHUNK)
    return dict(
        w_in=w_in_r, conv_w=ssd_conv_w[j], conv_b=ssd_conv_b[j],
        dt_bias=lane_pad(ssd_dt_bias[j]), alog=lane_pad(ssd_a_log[j]),
        e_f=(row == head[None, :]).astype(BF16),
        e_b=(row == head[None, :] + SSD_HEADS).astype(BF16),
        dskip=jnp.repeat(ssd_d[j], SSD_HEAD_DIM).reshape(1, d_inner),
        norm_g=ssd_norm_g[j], ln_g=sg_ln_g[j], ln_b=sg_ln_b[j],
        sg_w=sg_pairs.astype(BF16), sg_b=sg_b_exp, w_out=hy_w_out[j].astype(BF16))


def kernel(x, c, ctx, c_ctx, ada_w, ada_b, norm_mix_g, norm_mlp_g, mlp_w1, mlp_w2, hy_w_in,
           ssd_conv_w, ssd_conv_b, ssd_dt_bias, ssd_a_log, ssd_d, ssd_norm_g, sg_ln_g, sg_ln_b,
           sg_w, sg_b, hy_w_out, cf_w_pw1, cf_b_pw1, cf_w_dw, cf_b_dw, cf_ln_g, cf_ln_b, cf_w_pw2,
           cf_b_pw2, final_norm_g):
    bsz, seq, d = x.shape
    depth = ada_w.shape[0]
    assert bsz < MOD_ROWS and seq % 512 == 0 and ctx.shape[1] % (2 * CHUNK) == 0
    cs = jnp.concatenate([c, c_ctx[None, :], jnp.zeros((MOD_ROWS - bsz - 1, d), c.dtype)], axis=0)
    mod = _ada_call(cs, ada_w, ada_b)

    gw = SSD_HEADS * SSD_HEAD_DIM // SSD_GROUPS
    zeros = jnp.zeros((bsz, SSD_GROUPS, SSD_STATE, gw), F32)
    h, hc = x, ctx
    for i in range(depth):
        ctx_needed = i < depth - 1
        last = i == depth - 1
        lat_row = lambda b, i=i: i * MOD_ROWS + b
        ctx_row = lambda b, i=i: i * MOD_ROWS + bsz
        j = i // 2
        if i % 2 == 0:
            p = _hybrid_params(j, hy_w_in, ssd_conv_w, ssd_conv_b, ssd_dt_bias, ssd_a_log, ssd_d,
                               ssd_norm_g, sg_ln_g, sg_ln_b, sg_w, sg_b, hy_w_out)
            hc_new, h0_f, h0_b = _hybrid_mixer(hc, mod, ctx_row, norm_mix_g[i], p, zeros, zeros,
                                               ctx_needed)
            h, _, _ = _hybrid_mixer(h, mod, lat_row, norm_mix_g[i], p, h0_f, h0_b, True,
                                    mlp=(norm_mlp_g[i], mlp_w1[i].astype(BF16),
                                         mlp_w2[i].astype(BF16), final_norm_g, last))
            if ctx_needed:
                hc = hc_new
        else:
            w1 = cf_w_pw1[j].astype(BF16)
            w2 = cf_w_pw2[j].astype(BF16)
            args = (w1, cf_b_pw1[j], cf_w_dw[j], cf_b_dw[j], cf_ln_g[j], cf_ln_b[j], w2,
                    cf_b_pw2[j])
            h = _conf_mlp_call(h, mod, lat_row, norm_mix_g[i], *args, norm_mlp_g[i],
                               mlp_w1[i].astype(BF16), mlp_w2[i].astype(BF16), final_norm_g, last,
                               _tile(seq, TOKEN_TILE))
            if ctx_needed:
                hc = _conformer_call(hc, mod, ctx_row, norm_mix_g[i], *args,
                                     _tile(hc.shape[1], TOKEN_TILE))
        if ctx_needed:
            hc = _mlp_call(hc, mod, ctx_row, norm_mlp_g[i], mlp_w1[i].astype(BF16),
                           mlp_w2[i].astype(BF16), final_norm_g, False,
                           _tile(hc.shape[1], MLP_TILE))
    return h
```

```python
import functools

import jax
import jax.numpy as jnp
from jax import lax
from jax.experimental import pallas as pl
from jax.experimental.pallas import tpu as pltpu

F32 = jnp.float32
BF16 = jnp.bfloat16

SSD_HEADS = 16
SSD_HEAD_DIM = 64
SSD_GROUPS = 2
SSD_STATE = 128
SSD_CONV = 5
CHUNK = 128
SG_GROUPS = 8
SG_GROUP_DIM = 128
CONF_CONV = 31
EPS = 1e-6
LOG2E = 1.4426950408889634
N_MOD = 6
MOD_ROWS = 8
HALO = 16
LANES = 128
VMEM_LIMIT = 60 * 1024 * 1024
TOKEN_TILE = 512
MLP_TILE = 1024
CONF_COLS = 256
SCAN_TILE = 1024


def _cparams(*sem):
    return pltpu.CompilerParams(dimension_semantics=sem, vmem_limit_bytes=VMEM_LIMIT)


def _dot(a, b):
    return jnp.dot(a, b, preferred_element_type=F32)


def _sigmoid(x):
    return 1.0 / (1.0 + jnp.exp(-x))


def _silu(x):
    return x * _sigmoid(x)


def _gelu_tanh(x):
    c = 0.7978845608028654
    return x * (0.5 + 0.5 * jnp.tanh(x * (c + (c * 0.044715) * (x * x))))


def _softplus(x):
    return jnp.maximum(x, 0.0) + jnp.log1p(jnp.exp(-jnp.abs(x)))


def _modulate(hf, g, shift, scale):
    ms = jnp.mean(hf * hf, axis=-1, keepdims=True)
    return (hf * lax.rsqrt(ms + EPS)) * (g * (1.0 + scale)) + shift


def _layernorm(x, g, b):
    mu = jnp.mean(x, axis=-1, keepdims=True)
    xc = x - mu
    var = jnp.mean(xc * xc, axis=-1, keepdims=True)
    return xc * lax.rsqrt(var + EPS) * g + b


def _mod_vec(mod_ref, k):
    d = mod_ref.shape[-1] // N_MOD
    return mod_ref[:, k * d:(k + 1) * d]


def _cumsum_rows(x, reverse):
    n = x.shape[0]
    row = lax.broadcasted_iota(jnp.int32, x.shape, 0)
    s = 1
    while s < n:
        if reverse:
            x = x + jnp.where(row < n - s, pltpu.roll(x, n - s, axis=0), 0.0)
        else:
            x = x + jnp.where(row >= s, pltpu.roll(x, s, axis=0), 0.0)
        s *= 2
    return x


def _expand_heads(x, e, terms):
    out = None
    r = x
    for _ in range(terms):
        p = r.astype(BF16)
        t = _dot(p, e)
        out = t if out is None else out + t
        r = r - p.astype(F32)
    return out


def _dwconv_rows(ext, w_ref, bias, start, taps, tm, cols=slice(None)):
    n = ext.shape[0]
    acc = None
    for s in range(8):
        ks = [k for k in range(taps) if (start + k) % 8 == s]
        if not ks:
            continue
        rolled = ext if s == 0 else pltpu.roll(ext, n - s, axis=0)
        for k in ks:
            base = start + k - s
            term = w_ref[k:k + 1, cols] * rolled[base:base + tm]
            acc = term if acc is None else acc + term
    return acc + bias


def _resident(shape):
    return pl.BlockSpec(shape, lambda *_: (0,) * len(shape), pipeline_mode=pl.Buffered(1))


def _ada_kernel(cs_ref, w_ref, b_ref, o_ref):
    s = _silu(cs_ref[...]).astype(BF16)
    o_ref[...] = _dot(s, w_ref[...].astype(BF16)) + b_ref[...]


def _ada_call(cs, ada_w, ada_b):
    depth, d, n = ada_w.shape
    tn = n // 4
    out = pl.pallas_call(
        _ada_kernel,
        out_shape=jax.ShapeDtypeStruct((depth, MOD_ROWS, n), F32),
        grid=(depth, n // tn),
        in_specs=[
            pl.BlockSpec((MOD_ROWS, d), lambda l, j: (0, 0)),
            pl.BlockSpec((None, d, tn), lambda l, j: (l, 0, j)),
            pl.BlockSpec((None, 1, tn), lambda l, j: (l, 0, j)),
        ],
        out_specs=pl.BlockSpec((None, MOD_ROWS, tn), lambda l, j: (l, 0, j)),
        compiler_params=_cparams("parallel", "parallel"),
        name="ada_mod",
    )(cs, ada_w, ada_b.reshape(depth, 1, n))
    return out.reshape(depth * MOD_ROWS, 1, n)


def _halo_specs(tm, n_tiles, d):
    per = tm // HALO
    last = n_tiles * per - 1
    prev = pl.BlockSpec((None, None, HALO, d),
                        lambda b, i: (b, jnp.maximum(i * per - 1, 0), 0, 0))
    nxt = pl.BlockSpec((None, None, HALO, d),
                       lambda b, i: (b, jnp.minimum((i + 1) * per, last), 0, 0))
    return prev, nxt


def _ext_rows(hp_ref, h_ref, hn_ref):
    return jnp.concatenate([hp_ref[...], h_ref[...], hn_ref[...]], axis=0)


def _ext_valid(tm):
    i = pl.program_id(1)
    n = pl.num_programs(1)
    row = lax.broadcasted_iota(jnp.int32, (tm + 2 * HALO, 1), 0)
    return jnp.logical_and(jnp.logical_or(row >= HALO, i > 0),
                           jnp.logical_or(row < HALO + tm, i < n - 1))


def _inproj_kernel(hp_ref, h_ref, hn_ref, mod_ref, g_ref, w_ref, cw_ref, cb_ref, dtb_ref,
                   lng_ref, lnb_ref, zs_ref, xbc_ref, dt_ref, gate_ref, v_ref):
    tm, d = h_ref.shape
    n_xbc = xbc_ref.shape[-1]
    o_xbc = d
    o_uv = o_xbc + n_xbc
    o_dt = o_uv + 2 * d
    ext = _ext_rows(hp_ref, h_ref, hn_ref)
    u = _modulate(ext, g_ref[...], _mod_vec(mod_ref, 0), _mod_vec(mod_ref, 1))
    u = jnp.where(_ext_valid(tm), u, 0.0).astype(BF16)
    um = u[HALO:HALO + tm]

    xe = _dot(u, w_ref[:, o_xbc:o_uv])
    acc = _dwconv_rows(xe, cw_ref, cb_ref[...], HALO - SSD_CONV // 2, SSD_CONV, tm)
    xbc_ref[...] = _silu(acc).astype(BF16)

    zs_ref[...] = _silu(_dot(um, w_ref[:, 0:d])).astype(BF16)

    lane = lax.broadcasted_iota(jnp.int32, (1, LANES), 1)
    dt = _softplus(_dot(um, w_ref[:, o_dt:o_dt + LANES]) + dtb_ref[...])
    dt_ref[...] = jnp.where(lane < 2 * SSD_HEADS, dt, 0.0)

    gate_ref[...] = _gelu_tanh(_dot(um, w_ref[:, o_uv:o_uv + d])).astype(BF16)
    v = _gelu_tanh(_dot(um, w_ref[:, o_uv + d:o_uv + 2 * d]))
    v_ref[...] = _layernorm(v, lng_ref[...], lnb_ref[...]).astype(BF16)


def _inproj_call(h, mod, mod_row, norm_g, w_in_r, conv_w, conv_b, dt_bias_p, ln_g, ln_b, tm):
    bsz, seq, d = h.shape
    n_tiles = seq // tm
    n_xbc = conv_w.shape[-1]
    hv = h.reshape(bsz, seq // HALO, HALO, d)
    prev, nxt = _halo_specs(tm, n_tiles, d)
    const = lambda shape: pl.BlockSpec(shape, lambda b, i: (0,) * len(shape))
    tile = lambda n: pl.BlockSpec((None, tm, n), lambda b, i: (b, i, 0))
    out_shape = (
        jax.ShapeDtypeStruct((bsz, seq, d), BF16),
        jax.ShapeDtypeStruct((bsz, seq, n_xbc), BF16),
        jax.ShapeDtypeStruct((bsz, seq, LANES), F32),
        jax.ShapeDtypeStruct((bsz, seq, d), BF16),
        jax.ShapeDtypeStruct((bsz, seq, d), BF16),
    )
    return pl.pallas_call(
        _inproj_kernel,
        out_shape=out_shape,
        grid=(bsz, n_tiles),
        in_specs=[
            prev, tile(d), nxt,
            pl.BlockSpec((None, 1, mod.shape[-1]), lambda b, i: (mod_row(b), 0, 0)),
            const((1, d)),
            _resident(w_in_r.shape),
            const(conv_w.shape), const((1, n_xbc)), const((1, LANES)),
            const((1, d)), const((1, d)),
        ],
        out_specs=(tile(d), tile(n_xbc), tile(LANES), tile(d), tile(d)),
        compiler_params=_cparams("parallel", "parallel"),
        name="hy_inproj",
    )(hv, h, hv, mod, norm_g.reshape(1, d), w_in_r, conv_w, conv_b.reshape(1, n_xbc),
      dt_bias_p, ln_g.reshape(1, d), ln_b.reshape(1, d))


def _ssd_scan_block(xbc_ref, dt_ref, alog_ref, e_ref, hin_ref, hs_ref, xw_ref, reverse):
    n_chunks = hin_ref.shape[0]
    d_inner = e_ref.shape[-1]
    gw = d_inner // SSD_GROUPS
    a = -jnp.exp(alog_ref[...])
    e = e_ref[...]
    row8 = lax.broadcasted_iota(jnp.int32, (8, LANES), 0)
    ws = []
    edges = jnp.zeros((8, LANES), F32)
    for c in range(n_chunks):
        dt = dt_ref[c * CHUNK:(c + 1) * CHUNK, :]
        cs = _cumsum_rows(dt * a, reverse)
        edge = cs[0:1, :] if reverse else cs[CHUNK - 1:CHUNK, :]
        ws.append(jnp.exp(edge - cs) * dt)
        edges = jnp.where(row8 == c, edge, edges)
    w = _expand_heads(jnp.concatenate(ws, axis=0), e, 1)
    decay = _expand_heads(jnp.exp(edges), e, 3)
    xw_ref[...] = (xbc_ref[:, 0:d_inner].astype(F32) * w).astype(BF16)
    for t in range(n_chunks):
        c = n_chunks - 1 - t if reverse else t
        rows = slice(c * CHUNK, (c + 1) * CHUNK)
        for g in range(SSD_GROUPS):
            bm = xbc_ref[rows, d_inner + g * SSD_STATE:d_inner + (g + 1) * SSD_STATE]
            s = lax.dot_general(bm, xw_ref[rows, g * gw:(g + 1) * gw], (((0,), (0,)), ((), ())),
                                preferred_element_type=F32)
            hcur = hs_ref[g]
            hin_ref[c, g] = hcur.astype(BF16)
            hs_ref[g] = hcur * decay[c:c + 1, g * gw:(g + 1) * gw] + s


def _ssd_state_kernel(xf_ref, dtf_ref, xb_ref, dtb_ref, alog_ref, ef_ref, eb_ref, h0f_ref, h0b_ref,
                      hinf_ref, hfinf_ref, hinb_ref, hfinb_ref, hsf_ref, xwf_ref, hsb_ref, xwb_ref):
    i = pl.program_id(1)

    @pl.when(i == 0)
    def _():
        hsf_ref[...] = h0f_ref[...]
        hsb_ref[...] = h0b_ref[...]

    _ssd_scan_block(xf_ref, dtf_ref, alog_ref, ef_ref, hinf_ref, hsf_ref, xwf_ref, False)
    _ssd_scan_block(xb_ref, dtb_ref, alog_ref, eb_ref, hinb_ref, hsb_ref, xwb_ref, True)

    @pl.when(i == pl.num_programs(1) - 1)
    def _():
        hfinf_ref[...] = hsf_ref[...]
        hfinb_ref[...] = hsb_ref[...]


def _ssd_state_call(xbc, dt, alog_p, e_f, e_b, h0_f, h0_b, tq):
    bsz, seq, _ = xbc.shape
    d_inner = e_f.shape[-1]
    gw = d_inner // SSD_GROUPS
    n_blk = seq // tq
    cpb = tq // CHUNK
    fwd = lambda i: i
    bwd = lambda i: n_blk - 1 - i
    n_xb = d_inner + SSD_GROUPS * SSD_STATE
    tile = lambda n, blk: pl.BlockSpec((None, tq, n), lambda b, i: (b, blk(i), 0))
    const = lambda shape: pl.BlockSpec(shape, lambda b, i: (0,) * len(shape))
    state = pl.BlockSpec((None, SSD_GROUPS, SSD_STATE, gw), lambda b, i: (b, 0, 0, 0))
    hin = lambda blk: pl.BlockSpec((None, cpb, SSD_GROUPS, SSD_STATE, gw),
                                   lambda b, i: (b, blk(i), 0, 0, 0))
    hin_shape = jax.ShapeDtypeStruct((bsz, seq // CHUNK, SSD_GROUPS, SSD_STATE, gw), BF16)
    hfin_shape = jax.ShapeDtypeStruct((bsz, SSD_GROUPS, SSD_STATE, gw), F32)
    return pl.pallas_call(
        _ssd_state_kernel,
        out_shape=(hin_shape, hfin_shape, hin_shape, hfin_shape),
        grid=(bsz, n_blk),
        in_specs=[
            tile(n_xb, fwd), tile(LANES, fwd), tile(n_xb, bwd), tile(LANES, bwd),
            const((1, LANES)), const(e_f.shape), const(e_b.shape), state, state,
        ],
        out_specs=(hin(fwd), state, hin(bwd), state),
        scratch_shapes=[pltpu.VMEM((SSD_GROUPS, SSD_STATE, gw), F32),
                        pltpu.VMEM((tq, d_inner), BF16),
                        pltpu.VMEM((SSD_GROUPS, SSD_STATE, gw), F32),
                        pltpu.VMEM((tq, d_inner), BF16)],
        compiler_params=_cparams("parallel", "arbitrary"),
        name="ssd_state",
    )(xbc, dt, xbc, dt, alog_p, e_f, e_b, h0_f, h0_b)


def _mix_tile(xbc_ref, dt_ref, zs_ref, gate_ref, v_ref, hinf_ref, hinb_ref, alog_ref, ef_ref,
              eb_ref, dskip_ref, ng_ref, sgw_ref, sgb_ref, wout_ref, ycat_ref, per_chunk=None):
    tq = dt_ref.shape[0]
    n_chunks = tq // CHUNK
    d_inner = ef_ref.shape[-1]
    gw = d_inner // SSD_GROUPS
    hpg = SSD_HEADS // SSD_GROUPS
    a2 = -jnp.exp(alog_ref[...]) * LOG2E
    lane = lax.broadcasted_iota(jnp.int32, (CHUNK, LANES), 1)
    ii = lax.broadcasted_iota(jnp.int32, (CHUNK, CHUNK), 0)
    jj = lax.broadcasted_iota(jnp.int32, (CHUNK, CHUNK), 1)
    low = ii >= jj
    diag = ii == jj
    lane_lo = lane < SSD_HEAD_DIM
    lane2 = lax.broadcasted_iota(jnp.int32, (CHUNK, 2 * LANES), 1)
    lane2_lo = lane2 < LANES

    def chunk(c, carry):
        r0 = c * CHUNK if isinstance(c, int) else pl.multiple_of(c * CHUNK, CHUNK)
        rows = pl.ds(r0, CHUNK)
        dt = dt_ref[rows, :]
        da = dt * a2
        cs = jnp.where(lane < SSD_HEADS, _cumsum_rows(da, False), _cumsum_rows(da, True))
        ecs = jnp.exp2(cs)
        scale_f = _expand_heads(ecs, ef_ref[...], 1)
        scale_b = _expand_heads(ecs, eb_ref[...], 1)
        rw_t = (cs - jnp.log(dt) * LOG2E).T
        xs = xbc_ref[rows, 0:d_inner]
        cms, scs, dsel = [], [], jnp.zeros((CHUNK, LANES), F32)
        for g in range(SSD_GROUPS):
            bm = xbc_ref[rows, d_inner + g * SSD_STATE:d_inner + (g + 1) * SSD_STATE]
            o_c = d_inner + (SSD_GROUPS + g) * SSD_STATE
            cm = xbc_ref[rows, o_c:o_c + SSD_STATE]
            scores = lax.dot_general(cm, bm, (((1,), (1,)), ((), ())),
                                     preferred_element_type=F32)
            dg = jnp.sum(jnp.where(diag, scores, 0.0), axis=1, keepdims=True)
            lo = SSD_HEADS + g * hpg
            dsel = jnp.where(jnp.logical_and(lane >= lo, lane < lo + hpg), dg, dsel)
            cms.append(cm)
            scs.append(scores)
        skip = dskip_ref[...] + _expand_heads(dsel * dt, eb_ref[...], 1)
        for g in range(SSD_GROUPS):
            cm, scores = cms[g], scs[g]
            cols = slice(g * gw, (g + 1) * gw)
            y_g = (_dot(cm, hinf_ref[c, g]) * scale_f[:, cols]
                   + _dot(cm, hinb_ref[c, g]) * scale_b[:, cols])
            ys = []
            for pair in range(hpg // 2):
                ws = []
                for hh in range(2):
                    hd = g * hpg + pair * 2 + hh
                    hb = SSD_HEADS + hd
                    seg = jnp.where(low, cs[:, hd:hd + 1] - rw_t[hd:hd + 1, :],
                                    cs[:, hb:hb + 1] - rw_t[hb:hb + 1, :])
                    ws.append((scores * jnp.exp2(seg)).astype(BF16))
                p0 = (g * hpg + pair * 2) * SSD_HEAD_DIM
                xp = xs[:, p0:p0 + 2 * SSD_HEAD_DIM]
                zero = jnp.zeros_like(xp)
                rhs = jnp.concatenate([jnp.where(lane_lo, xp, zero),
                                       jnp.where(lane_lo, zero, xp)], axis=0)
                ys.append(_dot(jnp.concatenate(ws, axis=1), rhs))
            y_g = y_g + jnp.concatenate(ys, axis=1)
            y_g = y_g + skip[:, cols] * xs[:, cols].astype(F32)
            y_g = y_g * zs_ref[rows, cols].astype(F32)
            ms = jnp.mean(y_g * y_g, axis=-1, keepdims=True)
            y_g = y_g * lax.rsqrt(ms + EPS) * ng_ref[:, cols]
            ycat_ref[rows, cols] = y_g.astype(BF16)
        for sp in range(SG_GROUPS // 2):
            cols = slice(sp * 2 * SG_GROUP_DIM, (sp + 1) * 2 * SG_GROUP_DIM)
            vp = v_ref[rows, cols]
            zero = jnp.zeros_like(vp)
            rhs = jnp.concatenate([jnp.where(lane2_lo, vp, zero),
                                   jnp.where(lane2_lo, zero, vp)], axis=0)
            s = _dot(sgw_ref[sp], rhs) + sgb_ref[:, cols]
            ycat_ref[rows, d_inner + sp * 2 * SG_GROUP_DIM:d_inner + (sp + 1) * 2 * SG_GROUP_DIM] = (
                gate_ref[rows, cols].astype(F32) * s).astype(BF16)
        return carry

    if per_chunk is None:
        lax.fori_loop(0, n_chunks, chunk, 0)
    else:
        for c in range(n_chunks):
            per_chunk(c)
            chunk(c, 0)
    return _dot(ycat_ref[...], wout_ref[...])


def _mix_out_kernel(h_ref, xbc_ref, dt_ref, zs_ref, gate_ref, v_ref, hinf_ref, hinb_ref,
                    alog_ref, ef_ref, eb_ref, dskip_ref, ng_ref, sgw_ref, sgb_ref, wout_ref,
                    mod_ref, o_ref, ycat_ref):
    out = _mix_tile(xbc_ref, dt_ref, zs_ref, gate_ref, v_ref, hinf_ref, hinb_ref, alog_ref, ef_ref,
                    eb_ref, dskip_ref, ng_ref, sgw_ref, sgb_ref, wout_ref, ycat_ref)
    o_ref[...] = h_ref[...] + _mod_vec(mod_ref, 2) * out


def _mix_mlp_kernel(h_ref, xbc_ref, dt_ref, zs_ref, gate_ref, v_ref, hinf_ref, hinb_ref,
                    alog_ref, ef_ref, eb_ref, dskip_ref, ng_ref, sgw_ref, sgb_ref, wout_ref,
                    moda_ref, modb_ref, mg_ref, mw1_ref, mw2_ref, fg_ref, o_ref, ycat_ref,
                    hmid_ref, *, final_norm):
    s = pl.program_id(0)
    slot = lax.rem(s + 1, 2)
    n_chunks = dt_ref.shape[0] // CHUNK
    step = mw1_ref.shape[1] // n_chunks

    @pl.when(s == 0)
    def _():
        hmid_ref[...] = jnp.zeros_like(hmid_ref)

    hm = hmid_ref[slot]
    um = _modulate(hm, mg_ref[...], _mod_vec(modb_ref, 3), _mod_vec(modb_ref, 4)).astype(BF16)
    parts = []

    def mlp_block(c):
        r = jnp.maximum(_dot(um, mw1_ref[:, c * step:(c + 1) * step]), 0.0)
        parts.append(_dot((r * r).astype(BF16), mw2_ref[c * step:(c + 1) * step, :]))

    out = _mix_tile(xbc_ref, dt_ref, zs_ref, gate_ref, v_ref, hinf_ref, hinb_ref, alog_ref, ef_ref,
                    eb_ref, dskip_ref, ng_ref, sgw_ref, sgb_ref, wout_ref, ycat_ref,
                    per_chunk=mlp_block)
    mlp = parts[0]
    for part in parts[1:]:
        mlp = mlp + part
    res = hm + _mod_vec(modb_ref, 5) * mlp
    if final_norm:
        ms = jnp.mean(res * res, axis=-1, keepdims=True)
        res = res * lax.rsqrt(ms + EPS) * fg_ref[...]
    o_ref[...] = res
    hmid_ref[lax.rem(s, 2)] = h_ref[...] + _mod_vec(moda_ref, 2) * out


def _mix_out_call(h, xbc, dt, zs, gate, v, hin_f, hin_b, alog_p, e_f, e_b, dskip, norm_g,
                  sg_w, sg_b_exp, w_out, mod, mod_row, tq):
    bsz, seq, d = h.shape
    d_inner = e_f.shape[-1]
    gw = d_inner // SSD_GROUPS
    cpb = tq // CHUNK
    const = lambda shape: pl.BlockSpec(shape, lambda b, i: (0,) * len(shape))
    tile = lambda n: pl.BlockSpec((None, tq, n), lambda b, i: (b, i, 0))
    hin_spec = pl.BlockSpec((None, cpb, SSD_GROUPS, SSD_STATE, gw), lambda b, i: (b, i, 0, 0, 0))
    return pl.pallas_call(
        _mix_out_kernel,
        out_shape=jax.ShapeDtypeStruct((bsz, seq, d), F32),
        grid=(bsz, seq // tq),
        in_specs=[
            tile(d), tile(xbc.shape[-1]), tile(LANES), tile(d_inner), tile(d), tile(d),
            hin_spec, hin_spec,
            const((1, LANES)), const(e_f.shape), const(e_b.shape),
            const((1, d_inner)), const((1, d_inner)),
            const(sg_w.shape), const(sg_b_exp.shape), _resident(w_out.shape),
            pl.BlockSpec((None, 1, mod.shape[-1]), lambda b, i: (mod_row(b), 0, 0)),
        ],
        out_specs=tile(d),
        scratch_shapes=[pltpu.VMEM((tq, w_out.shape[0]), BF16)],
        compiler_params=_cparams("parallel", "parallel"),
        name="hy_mix_out",
    )(h, xbc, dt, zs, gate, v, hin_f, hin_b, alog_p, e_f, e_b, dskip, norm_g.reshape(1, d_inner),
      sg_w, sg_b_exp, w_out, mod)


def _mix_mlp_call(h, xbc, dt, zs, gate, v, hin_f, hin_b, alog_p, e_f, e_b, dskip, norm_g,
                  sg_w, sg_b_exp, w_out, mod, mod_row, mg, mw1, mw2, final_g, final_norm, tq):
    bsz, seq, d = h.shape
    d_inner = e_f.shape[-1]
    gw = d_inner // SSD_GROUPS
    cpb = tq // CHUNK
    n_tiles = seq // tq
    total = bsz * n_tiles
    cur = lambda s: jnp.minimum(s, total - 1)
    done = lambda s: jnp.maximum(s - 1, 0)
    const = lambda shape: pl.BlockSpec(shape, lambda s: (0,) * len(shape))
    tile = lambda n: pl.BlockSpec((None, tq, n), lambda s: (cur(s) // n_tiles, cur(s) % n_tiles, 0))
    hin_spec = pl.BlockSpec((None, cpb, SSD_GROUPS, SSD_STATE, gw),
                            lambda s: (cur(s) // n_tiles, cur(s) % n_tiles, 0, 0, 0))
    mod_spec = lambda t: pl.BlockSpec((None, 1, mod.shape[-1]),
                                      lambda s: (mod_row(t(s) // n_tiles), 0, 0))
    return pl.pallas_call(
        functools.partial(_mix_mlp_kernel, final_norm=final_norm),
        out_shape=jax.ShapeDtypeStruct((bsz, seq, d), F32),
        grid=(total + 1,),
        in_specs=[
            tile(d), tile(xbc.shape[-1]), tile(LANES), tile(d_inner), tile(d), tile(d),
            hin_spec, hin_spec,
            const((1, LANES)), const(e_f.shape), const(e_b.shape),
            const((1, d_inner)), const((1, d_inner)),
            const(sg_w.shape), const(sg_b_exp.shape), _resident(w_out.shape),
            mod_spec(cur), mod_spec(done),
            const((1, d)), _resident(mw1.shape), _resident(mw2.shape), const((1, d)),
        ],
        out_specs=pl.BlockSpec((None, tq, d), lambda s: (done(s) // n_tiles, done(s) % n_tiles, 0)),
        scratch_shapes=[pltpu.VMEM((tq, w_out.shape[0]), BF16), pltpu.VMEM((2, tq, d), F32)],
        compiler_params=_cparams("arbitrary"),
        name="mix_mlp_final" if final_norm else "mix_mlp",
    )(h, xbc, dt, zs, gate, v, hin_f, hin_b, alog_p, e_f, e_b, dskip, norm_g.reshape(1, d_inner),
      sg_w, sg_b_exp, w_out, mod, mod, mg.reshape(1, d), mw1, mw2, final_g.reshape(1, d))


def _conformer_kernel(hp_ref, h_ref, hn_ref, mod_ref, g_ref, w1_ref, b1_ref, wd_ref, bd_ref,
                      lng_ref, lnb_ref, w2_ref, b2_ref, o_ref):
    tm, d = h_ref.shape
    cw = w2_ref.shape[0]
    ext = _ext_rows(hp_ref, h_ref, hn_ref)
    u = _modulate(ext, g_ref[...], _mod_vec(mod_ref, 0), _mod_vec(mod_ref, 1)).astype(BF16)
    valid = _ext_valid(tm)
    accs = []
    for c0 in range(0, cw, CONF_COLS):
        cols = slice(c0, c0 + CONF_COLS)
        gcols = slice(cw + c0, cw + c0 + CONF_COLS)
        a = (_dot(u, w1_ref[:, cols]) + b1_ref[:, cols]) * _sigmoid(
            _dot(u, w1_ref[:, gcols]) + b1_ref[:, gcols])
        a = jnp.where(valid, a, 0.0)
        accs.append(_dwconv_rows(a, wd_ref, bd_ref[:, cols], HALO - CONF_CONV // 2, CONF_CONV, tm,
                                 cols))
    acc = jnp.concatenate(accs, axis=1)
    y = _silu(_layernorm(acc, lng_ref[...], lnb_ref[...])).astype(BF16)
    out = _dot(y, w2_ref[...]) + b2_ref[...]
    o_ref[...] = h_ref[...] + _mod_vec(mod_ref, 2) * out


def _conformer_call(h, mod, mod_row, norm_g, w1, b1, wd, bd, ln_g, ln_b, w2, b2, tm):
    bsz, seq, d = h.shape
    n_tiles = seq // tm
    cw = w2.shape[0]
    hv = h.reshape(bsz, seq // HALO, HALO, d)
    prev, nxt = _halo_specs(tm, n_tiles, d)
    const = lambda shape: pl.BlockSpec(shape, lambda b, i: (0,) * len(shape))
    tile = pl.BlockSpec((None, tm, d), lambda b, i: (b, i, 0))
    return pl.pallas_call(
        _conformer_kernel,
        out_shape=jax.ShapeDtypeStruct((bsz, seq, d), F32),
        grid=(bsz, n_tiles),
        in_specs=[
            prev, tile, nxt,
            pl.BlockSpec((None, 1, mod.shape[-1]), lambda b, i: (mod_row(b), 0, 0)),
            const((1, d)), _resident(w1.shape), const((1, 2 * cw)), const(wd.shape),
            const((1, cw)), const((1, cw)), const((1, cw)), _resident(w2.shape), const((1, d)),
        ],
        out_specs=tile,
        compiler_params=_cparams("parallel", "parallel"),
        name="conformer",
    )(hv, h, hv, mod, norm_g.reshape(1, d), w1, b1.reshape(1, 2 * cw), wd, bd.reshape(1, cw),
      ln_g.reshape(1, cw), ln_b.reshape(1, cw), w2, b2.reshape(1, d))


def _mlp_kernel(h_ref, mod_ref, g_ref, w1_ref, w2_ref, fg_ref, o_ref, a_ref, *, final_norm):
    hf = h_ref[...]
    u = _modulate(hf, g_ref[...], _mod_vec(mod_ref, 3), _mod_vec(mod_ref, 4)).astype(BF16)
    hid = w1_ref.shape[1]
    step = min(hid, 1024)
    for k in range(0, hid, step):
        r = jnp.maximum(_dot(u, w1_ref[:, k:k + step]), 0.0)
        a_ref[:, k:k + step] = (r * r).astype(BF16)
    out = hf + _mod_vec(mod_ref, 5) * _dot(a_ref[...], w2_ref[...])
    if final_norm:
        ms = jnp.mean(out * out, axis=-1, keepdims=True)
        out = out * lax.rsqrt(ms + EPS) * fg_ref[...]
    o_ref[...] = out


def _mlp_call(h, mod, mod_row, norm_g, w1, w2, final_g, final_norm, tm):
    bsz, seq, d = h.shape
    const = lambda shape: pl.BlockSpec(shape, lambda b, i: (0,) * len(shape))
    tile = pl.BlockSpec((None, tm, d), lambda b, i: (b, i, 0))
    return pl.pallas_call(
        functools.partial(_mlp_kernel, final_norm=final_norm),
        out_shape=jax.ShapeDtypeStruct((bsz, seq, d), F32),
        grid=(bsz, seq // tm),
        in_specs=[
            tile,
            pl.BlockSpec((None, 1, mod.shape[-1]), lambda b, i: (mod_row(b), 0, 0)),
            const((1, d)), _resident(w1.shape), _resident(w2.shape), const((1, d)),
        ],
        out_specs=tile,
        scratch_shapes=[pltpu.VMEM((tm, w1.shape[1]), BF16)],
        compiler_params=_cparams("parallel", "parallel"),
        name="mlp_final" if final_norm else "mlp",
    )(h, mod, norm_g.reshape(1, d), w1, w2, final_g.reshape(1, d))


def _conf_mlp_kernel(hp_ref, h_ref, hn_ref, moda_ref, modb_ref, g_ref, w1_ref, b1_ref, wd_ref,
                     bd_ref, lng_ref, lnb_ref, w2_ref, b2_ref, mg_ref, mw1_ref, mw2_ref, fg_ref,
                     o_ref, hmid_ref, *, n_tiles, final_norm):
    s = pl.program_id(0)
    tm, d = h_ref.shape
    cw = w2_ref.shape[0]
    slot = lax.rem(s + 1, 2)

    @pl.when(s == 0)
    def _():
        hmid_ref[...] = jnp.zeros_like(hmid_ref)

    hm = hmid_ref[slot]
    um = _modulate(hm, mg_ref[...], _mod_vec(modb_ref, 3), _mod_vec(modb_ref, 4)).astype(BF16)
    ext = _ext_rows(hp_ref, h_ref, hn_ref)
    uc = _modulate(ext, g_ref[...], _mod_vec(moda_ref, 0), _mod_vec(moda_ref, 1)).astype(BF16)
    i = lax.rem(jnp.minimum(s, pl.num_programs(0) - 2), n_tiles)
    row = lax.broadcasted_iota(jnp.int32, (tm + 2 * HALO, 1), 0)
    valid = jnp.logical_and(jnp.logical_or(row >= HALO, i > 0),
                            jnp.logical_or(row < HALO + tm, i < n_tiles - 1))

    hid = mw1_ref.shape[1]
    n_blk = cw // CONF_COLS
    step = hid // n_blk
    accs = []
    mlp = None
    for j in range(n_blk):
        cols = slice(j * CONF_COLS, (j + 1) * CONF_COLS)
        gcols = slice(cw + j * CONF_COLS, cw + (j + 1) * CONF_COLS)
        r = jnp.maximum(_dot(um, mw1_ref[:, j * step:(j + 1) * step]), 0.0)
        part = _dot((r * r).astype(BF16), mw2_ref[j * step:(j + 1) * step, :])
        mlp = part if mlp is None else mlp + part
        a = (_dot(uc, w1_ref[:, cols]) + b1_ref[:, cols]) * _sigmoid(
            _dot(uc, w1_ref[:, gcols]) + b1_ref[:, gcols])
        a = jnp.where(valid, a, 0.0)
        accs.append(_dwconv_rows(a, wd_ref, bd_ref[:, cols], HALO - CONF_CONV // 2, CONF_CONV, tm,
                                 cols))

    out = hm + _mod_vec(modb_ref, 5) * mlp
    if final_norm:
        ms = jnp.mean(out * out, axis=-1, keepdims=True)
        out = out * lax.rsqrt(ms + EPS) * fg_ref[...]
    o_ref[...] = out

    acc = jnp.concatenate(accs, axis=1)
    y = _silu(_layernorm(acc, lng_ref[...], lnb_ref[...])).astype(BF16)
    outc = _dot(y, w2_ref[...]) + b2_ref[...]
    hmid_ref[lax.rem(s, 2)] = h_ref[...] + _mod_vec(moda_ref, 2) * outc


def _conf_mlp_call(h, mod, mod_row, cg, w1, b1, wd, bd, ln_g, ln_b, w2, b2, mg, mw1, mw2, final_g,
                   final_norm, tm):
    bsz, seq, d = h.shape
    n_tiles = seq // tm
    total = bsz * n_tiles
    cw = w2.shape[0]
    per = tm // HALO
    hv = h.reshape(bsz, seq // HALO, HALO, d)
    cur = lambda s: jnp.minimum(s, total - 1)
    done = lambda s: jnp.maximum(s - 1, 0)
    const = lambda shape: pl.BlockSpec(shape, lambda s: (0,) * len(shape))
    mod_spec = lambda t: pl.BlockSpec((None, 1, mod.shape[-1]),
                                      lambda s: (mod_row(t(s) // n_tiles), 0, 0))
    return pl.pallas_call(
        functools.partial(_conf_mlp_kernel, n_tiles=n_tiles, final_norm=final_norm),
        out_shape=jax.ShapeDtypeStruct((bsz, seq, d), F32),
        grid=(total + 1,),
        in_specs=[
            pl.BlockSpec((None, None, HALO, d), lambda s: (
                cur(s) // n_tiles, jnp.maximum(cur(s) % n_tiles * per - 1, 0), 0, 0)),
            pl.BlockSpec((None, tm, d), lambda s: (cur(s) // n_tiles, cur(s) % n_tiles, 0)),
            pl.BlockSpec((None, None, HALO, d), lambda s: (
                cur(s) // n_tiles, jnp.minimum((cur(s) % n_tiles + 1) * per, n_tiles * per - 1),
                0, 0)),
            mod_spec(cur), mod_spec(done),
            const((1, d)), _resident(w1.shape), const((1, 2 * cw)), const(wd.shape),
            const((1, cw)), const((1, cw)), const((1, cw)), _resident(w2.shape), const((1, d)),
            const((1, d)), _resident(mw1.shape), _resident(mw2.shape), const((1, d)),
        ],
        out_specs=pl.BlockSpec((None, tm, d), lambda s: (done(s) // n_tiles, done(s) % n_tiles, 0)),
        scratch_shapes=[pltpu.VMEM((2, tm, d), F32)],
        compiler_params=_cparams("arbitrary"),
        name="conf_mlp_final" if final_norm else "conf_mlp",
    )(hv, h, hv, mod, mod, cg.reshape(1, d), w1, b1.reshape(1, 2 * cw), wd, bd.reshape(1, cw),
      ln_g.reshape(1, cw), ln_b.reshape(1, cw), w2, b2.reshape(1, d), mg.reshape(1, d), mw1, mw2,
      final_g.reshape(1, d))


def _tile(seq, want):
    return min(seq, want)


def _hybrid_mixer(h, mod, mod_row, norm_g, p, h0_f, h0_b, need_out, mlp=None):
    seq = h.shape[1]
    zs, xbc, dt, gate, v = _inproj_call(
        h, mod, mod_row, norm_g, p["w_in"], p["conv_w"], p["conv_b"], p["dt_bias"],
        p["ln_g"], p["ln_b"], _tile(seq, TOKEN_TILE))
    ts = _tile(seq, SCAN_TILE)
    tq = _tile(seq, TOKEN_TILE)
    hin_f, hfin_f, hin_b, hfin_b = _ssd_state_call(xbc, dt, p["alog"], p["e_f"], p["e_b"], h0_f,
                                                   h0_b, ts)
    out = None
    args = (h, xbc, dt, zs, gate, v, hin_f, hin_b, p["alog"], p["e_f"], p["e_b"], p["dskip"],
            p["norm_g"], p["sg_w"], p["sg_b"], p["w_out"], mod, mod_row)
    if mlp is not None:
        out = _mix_mlp_call(*args, *mlp, tq)
    elif need_out:
        out = _mix_out_call(*args, tq)
    return out, hfin_f, hfin_b


def _hybrid_params(j, hy_w_in, ssd_conv_w, ssd_conv_b, ssd_dt_bias, ssd_a_log, ssd_d, ssd_norm_g,
                   sg_ln_g, sg_ln_b, sg_w, sg_b, hy_w_out):
    d_inner = SSD_HEADS * SSD_HEAD_DIM
    n_xbc = d_inner + 2 * SSD_GROUPS * SSD_STATE
    w = hy_w_in[j]
    o2 = d_inner + n_xbc
    o3 = o2 + 2 * SSD_HEADS
    pad = jnp.zeros((w.shape[0], LANES - 2 * SSD_HEADS), w.dtype)
    w_in_r = jnp.concatenate([w[:, :o2], w[:, o3:], w[:, o2:o3], pad], axis=1).astype(BF16)
    lane_pad = lambda t: jnp.pad(t.reshape(1, -1), ((0, 0), (0, LANES - t.size)))
    head = jnp.arange(d_inner, dtype=jnp.int32) // SSD_HEAD_DIM
    row = jnp.arange(LANES, dtype=jnp.int32)[:, None]
    sg_width = SG_GROUPS * SG_GROUP_DIM
    sg_b_exp = jnp.broadcast_to(sg_b[j].T[:, :, None],
                                (CHUNK, SG_GROUPS, SG_GROUP_DIM)).reshape(CHUNK, sg_width)
    sg_pairs = sg_w[j].reshape(SG_GROUPS // 2, 2, CHUNK, CHUNK).transpose(0, 2, 1, 3).reshape(
        SG_GROUPS // 2, CHUNK, 2 * CHUNK)
    return dict(
        w_in=w_in_r, conv_w=ssd_conv_w[j], conv_b=ssd_conv_b[j],
        dt_bias=lane_pad(ssd_dt_bias[j]), alog=lane_pad(ssd_a_log[j]),
        e_f=(row == head[None, :]).astype(BF16),
        e_b=(row == head[None, :] + SSD_HEADS).astype(BF16),
        dskip=jnp.repeat(ssd_d[j], SSD_HEAD_DIM).reshape(1, d_inner),
        norm_g=ssd_norm_g[j], ln_g=sg_ln_g[j], ln_b=sg_ln_b[j],
        sg_w=sg_pairs.astype(BF16), sg_b=sg_b_exp, w_out=hy_w_out[j].astype(BF16))


def kernel(x, c, ctx, c_ctx, ada_w, ada_b, norm_mix_g, norm_mlp_g, mlp_w1, mlp_w2, hy_w_in,
           ssd_conv_w, ssd_conv_b, ssd_dt_bias, ssd_a_log, ssd_d, ssd_norm_g, sg_ln_g, sg_ln_b,
           sg_w, sg_b, hy_w_out, cf_w_pw1, cf_b_pw1, cf_w_dw, cf_b_dw, cf_ln_g, cf_ln_b, cf_w_pw2,
           cf_b_pw2, final_norm_g):
    bsz, seq, d = x.shape
    depth = ada_w.shape[0]
    assert bsz < MOD_ROWS and seq % 512 == 0 and ctx.shape[1] % (2 * CHUNK) == 0
    cs = jnp.concatenate([c, c_ctx[None, :], jnp.zeros((MOD_ROWS - bsz - 1, d), c.dtype)], axis=0)
    mod = _ada_call(cs, ada_w, ada_b)

    gw = SSD_HEADS * SSD_HEAD_DIM // SSD_GROUPS
    zeros = jnp.zeros((bsz, SSD_GROUPS, SSD_STATE, gw), F32)
    h, hc = x, ctx
    for i in range(depth):
        ctx_needed = i < depth - 1
        last = i == depth - 1
        lat_row = lambda b, i=i: i * MOD_ROWS + b
        ctx_row = lambda b, i=i: i * MOD_ROWS + bsz
        j = i // 2
        if i % 2 == 0:
            p = _hybrid_params(j, hy_w_in, ssd_conv_w, ssd_conv_b, ssd_dt_bias, ssd_a_log, ssd_d,
                               ssd_norm_g, sg_ln_g, sg_ln_b, sg_w, sg_b, hy_w_out)
            hc_new, h0_f, h0_b = _hybrid_mixer(hc, mod, ctx_row, norm_mix_g[i], p, zeros, zeros,
                                               ctx_needed)
            h, _, _ = _hybrid_mixer(h, mod, lat_row, norm_mix_g[i], p, h0_f, h0_b, True,
                                    mlp=(norm_mlp_g[i], mlp_w1[i].astype(BF16),
                                         mlp_w2[i].astype(BF16), final_norm_g, last))
            if ctx_needed:
                hc = hc_new
        else:
            w1 = cf_w_pw1[j].astype(BF16)
            w2 = cf_w_pw2[j].astype(BF16)
            args = (w1, cf_b_pw1[j], cf_w_dw[j], cf_b_dw[j], cf_ln_g[j], cf_ln_b[j], w2,
                    cf_b_pw2[j])
            h = _conf_mlp_call(h, mod, lat_row, norm_mix_g[i], *args, norm_mlp_g[i],
                               mlp_w1[i].astype(BF16), mlp_w2[i].astype(BF16), final_norm_g, last,
                               _tile(seq, TOKEN_TILE))
            if ctx_needed:
                hc = _conformer_call(hc, mod, ctx_row, norm_mix_g[i], *args,
                                     _tile(hc.shape[1], TOKEN_TILE))
        if ctx_needed:
            hc = _mlp_call(hc, mod, ctx_row, norm_mlp_g[i], mlp_w1[i].astype(BF16),
                           mlp_w2[i].astype(BF16), final_norm_g, False,
                           _tile(hc.shape[1], MLP_TILE))
    return h
```

```python
import functools

import jax
import jax.numpy as jnp
from jax import lax
from jax.experimental import pallas as pl
from jax.experimental.pallas import tpu as pltpu

F32 = jnp.float32
BF16 = jnp.bfloat16

SSD_HEADS = 16
SSD_HEAD_DIM = 64
SSD_GROUPS = 2
SSD_STATE = 128
SSD_CONV = 5
CHUNK = 128
SG_GROUPS = 8
SG_GROUP_DIM = 128
CONF_CONV = 31
EPS = 1e-6
LOG2E = 1.4426950408889634
N_MOD = 6
MOD_ROWS = 8
HALO = 16
LANES = 128
VMEM_LIMIT = 60 * 1024 * 1024
TOKEN_TILE = 512
MLP_TILE = 1024
CONF_COLS = 256
SCAN_TILE = 1024


def _cparams(*sem):
    return pltpu.CompilerParams(dimension_semantics=sem, vmem_limit_bytes=VMEM_LIMIT)


def _dot(a, b):
    return jnp.dot(a, b, preferred_element_type=F32)


def _sigmoid(x):
    return 1.0 / (1.0 + jnp.exp(-x))


def _silu(x):
    return x * _sigmoid(x)


def _gelu_tanh(x):
    c = 0.7978845608028654
    return x * (0.5 + 0.5 * jnp.tanh(x * (c + (c * 0.044715) * (x * x))))


def _softplus(x):
    return jnp.maximum(x, 0.0) + jnp.log1p(jnp.exp(-jnp.abs(x)))


def _modulate(hf, g, shift, scale):
    ms = jnp.mean(hf * hf, axis=-1, keepdims=True)
    return (hf * lax.rsqrt(ms + EPS)) * (g * (1.0 + scale)) + shift


def _layernorm(x, g, b):
    mu = jnp.mean(x, axis=-1, keepdims=True)
    xc = x - mu
    var = jnp.mean(xc * xc, axis=-1, keepdims=True)
    return xc * lax.rsqrt(var + EPS) * g + b


def _mod_vec(mod_ref, k):
    d = mod_ref.shape[-1] // N_MOD
    return mod_ref[:, k * d:(k + 1) * d]


def _cumsum_rows(x, reverse):
    n = x.shape[0]
    row = lax.broadcasted_iota(jnp.int32, x.shape, 0)
    s = 1
    while s < n:
        if reverse:
            x = x + jnp.where(row < n - s, pltpu.roll(x, n - s, axis=0), 0.0)
        else:
            x = x + jnp.where(row >= s, pltpu.roll(x, s, axis=0), 0.0)
        s *= 2
    return x


def _expand_heads(x, e, terms):
    out = None
    r = x
    for _ in range(terms):
        p = r.astype(BF16)
        t = _dot(p, e)
        out = t if out is None else out + t
        r = r - p.astype(F32)
    return out


def _dwconv_rows(ext, w_ref, bias, start, taps, tm, cols=slice(None)):
    n = ext.shape[0]
    acc = None
    for s in range(8):
        ks = [k for k in range(taps) if (start + k) % 8 == s]
        if not ks:
            continue
        rolled = ext if s == 0 else pltpu.roll(ext, n - s, axis=0)
        for k in ks:
            base = start + k - s
            term = w_ref[k:k + 1, cols] * rolled[base:base + tm]
            acc = term if acc is None else acc + term
    return acc + bias


def _resident(shape):
    return pl.BlockSpec(shape, lambda *_: (0,) * len(shape), pipeline_mode=pl.Buffered(1))


def _ada_kernel(cs_ref, w_ref, b_ref, o_ref):
    s = _silu(cs_ref[...]).astype(BF16)
    o_ref[...] = _dot(s, w_ref[...].astype(BF16)) + b_ref[...]


def _ada_call(cs, ada_w, ada_b):
    depth, d, n = ada_w.shape
    tn = n // 4
    out = pl.pallas_call(
        _ada_kernel,
        out_shape=jax.ShapeDtypeStruct((depth, MOD_ROWS, n), F32),
        grid=(depth, n // tn),
        in_specs=[
            pl.BlockSpec((MOD_ROWS, d), lambda l, j: (0, 0)),
            pl.BlockSpec((None, d, tn), lambda l, j: (l, 0, j)),
            pl.BlockSpec((None, 1, tn), lambda l, j: (l, 0, j)),
        ],
        out_specs=pl.BlockSpec((None, MOD_ROWS, tn), lambda l, j: (l, 0, j)),
        compiler_params=_cparams("parallel", "parallel"),
        name="ada_mod",
    )(cs, ada_w, ada_b.reshape(depth, 1, n))
    return out.reshape(depth * MOD_ROWS, 1, n)


def _halo_specs(tm, n_tiles, d):
    per = tm // HALO
    last = n_tiles * per - 1
    prev = pl.BlockSpec((None, None, HALO, d),
                        lambda b, i: (b, jnp.maximum(i * per - 1, 0), 0, 0))
    nxt = pl.BlockSpec((None, None, HALO, d),
                       lambda b, i: (b, jnp.minimum((i + 1) * per, last), 0, 0))
    return prev, nxt


def _ext_rows(hp_ref, h_ref, hn_ref):
    return jnp.concatenate([hp_ref[...], h_ref[...], hn_ref[...]], axis=0)


def _ext_valid(tm):
    i = pl.program_id(1)
    n = pl.num_programs(1)
    row = lax.broadcasted_iota(jnp.int32, (tm + 2 * HALO, 1), 0)
    return jnp.logical_and(jnp.logical_or(row >= HALO, i > 0),
                           jnp.logical_or(row < HALO + tm, i < n - 1))


def _inproj_kernel(hp_ref, h_ref, hn_ref, mod_ref, g_ref, w_ref, cw_ref, cb_ref, dtb_ref,
                   lng_ref, lnb_ref, zs_ref, xbc_ref, dt_ref, gate_ref, v_ref):
    tm, d = h_ref.shape
    n_xbc = xbc_ref.shape[-1]
    o_xbc = d
    o_uv = o_xbc + n_xbc
    o_dt = o_uv + 2 * d
    ext = _ext_rows(hp_ref, h_ref, hn_ref)
    u = _modulate(ext, g_ref[...], _mod_vec(mod_ref, 0), _mod_vec(mod_ref, 1))
    u = jnp.where(_ext_valid(tm), u, 0.0).astype(BF16)
    um = u[HALO:HALO + tm]

    xe = _dot(u, w_ref[:, o_xbc:o_uv])
    acc = _dwconv_rows(xe, cw_ref, cb_ref[...], HALO - SSD_CONV // 2, SSD_CONV, tm)
    xbc_ref[...] = _silu(acc).astype(BF16)

    zs_ref[...] = _silu(_dot(um, w_ref[:, 0:d])).astype(BF16)

    lane = lax.broadcasted_iota(jnp.int32, (1, LANES), 1)
    dt = _softplus(_dot(um, w_ref[:, o_dt:o_dt + LANES]) + dtb_ref[...])
    dt_ref[...] = jnp.where(lane < 2 * SSD_HEADS, dt, 0.0)

    gate_ref[...] = _gelu_tanh(_dot(um, w_ref[:, o_uv:o_uv + d])).astype(BF16)
    v = _gelu_tanh(_dot(um, w_ref[:, o_uv + d:o_uv + 2 * d]))
    v_ref[...] = _layernorm(v, lng_ref[...], lnb_ref[...]).astype(BF16)


def _inproj_call(h, mod, mod_row, norm_g, w_in_r, conv_w, conv_b, dt_bias_p, ln_g, ln_b, tm):
    bsz, seq, d = h.shape
    n_tiles = seq // tm
    n_xbc = conv_w.shape[-1]
    hv = h.reshape(bsz, seq // HALO, HALO, d)
    prev, nxt = _halo_specs(tm, n_tiles, d)
    const = lambda shape: pl.BlockSpec(shape, lambda b, i: (0,) * len(shape))
    tile = lambda n: pl.BlockSpec((None, tm, n), lambda b, i: (b, i, 0))
    out_shape = (
        jax.ShapeDtypeStruct((bsz, seq, d), BF16),
        jax.ShapeDtypeStruct((bsz, seq, n_xbc), BF16),
        jax.ShapeDtypeStruct((bsz, seq, LANES), F32),
        jax.ShapeDtypeStruct((bsz, seq, d), BF16),
        jax.ShapeDtypeStruct((bsz, seq, d), BF16),
    )
    return pl.pallas_call(
        _inproj_kernel,
        out_shape=out_shape,
        grid=(bsz, n_tiles),
        in_specs=[
            prev, tile(d), nxt,
            pl.BlockSpec((None, 1, mod.shape[-1]), lambda b, i: (mod_row(b), 0, 0)),
            const((1, d)),
            _resident(w_in_r.shape),
            const(conv_w.shape), const((1, n_xbc)), const((1, LANES)),
            const((1, d)), const((1, d)),
        ],
        out_specs=(tile(d), tile(n_xbc), tile(LANES), tile(d), tile(d)),
        compiler_params=_cparams("parallel", "parallel"),
        name="hy_inproj",
    )(hv, h, hv, mod, norm_g.reshape(1, d), w_in_r, conv_w, conv_b.reshape(1, n_xbc),
      dt_bias_p, ln_g.reshape(1, d), ln_b.reshape(1, d))


def _ssd_scan_block(xbc_ref, dt_ref, alog_ref, e_ref, hin_ref, hs_ref, xw_ref, reverse):
    n_chunks = hin_ref.shape[0]
    d_inner = e_ref.shape[-1]
    gw = d_inner // SSD_GROUPS
    a = -jnp.exp(alog_ref[...])
    e = e_ref[...]
    row8 = lax.broadcasted_iota(jnp.int32, (8, LANES), 0)
    ws = []
    edges = jnp.zeros((8, LANES), F32)
    for c in range(n_chunks):
        dt = dt_ref[c * CHUNK:(c + 1) * CHUNK, :]
        cs = _cumsum_rows(dt * a, reverse)
        edge = cs[0:1, :] if reverse else cs[CHUNK - 1:CHUNK, :]
        ws.append(jnp.exp(edge - cs) * dt)
        edges = jnp.where(row8 == c, edge, edges)
    w = _expand_heads(jnp.concatenate(ws, axis=0), e, 1)
    decay = _expand_heads(jnp.exp(edges), e, 3)
    xw_ref[...] = (xbc_ref[:, 0:d_inner].astype(F32) * w).astype(BF16)
    for t in range(n_chunks):
        c = n_chunks - 1 - t if reverse else t
        rows = slice(c * CHUNK, (c + 1) * CHUNK)
        for g in range(SSD_GROUPS):
            bm = xbc_ref[rows, d_inner + g * SSD_STATE:d_inner + (g + 1) * SSD_STATE]
            s = lax.dot_general(bm, xw_ref[rows, g * gw:(g + 1) * gw], (((0,), (0,)), ((), ())),
                                preferred_element_type=F32)
            hcur = hs_ref[g]
            hin_ref[c, g] = hcur.astype(BF16)
            hs_ref[g] = hcur * decay[c:c + 1, g * gw:(g + 1) * gw] + s


def _ssd_state_kernel(xf_ref, dtf_ref, xb_ref, dtb_ref, alog_ref, ef_ref, eb_ref, h0f_ref, h0b_ref,
                      hinf_ref, hfinf_ref, hinb_ref, hfinb_ref, hsf_ref, xwf_ref, hsb_ref, xwb_ref):
    i = pl.program_id(1)

    @pl.when(i == 0)
    def _():
        hsf_ref[...] = h0f_ref[...]
        hsb_ref[...] = h0b_ref[...]

    _ssd_scan_block(xf_ref, dtf_ref, alog_ref, ef_ref, hinf_ref, hsf_ref, xwf_ref, False)
    _ssd_scan_block(xb_ref, dtb_ref, alog_ref, eb_ref, hinb_ref, hsb_ref, xwb_ref, True)

    @pl.when(i == pl.num_programs(1) - 1)
    def _():
        hfinf_ref[...] = hsf_ref[...]
        hfinb_ref[...] = hsb_ref[...]


def _ssd_state_call(xbc, dt, alog_p, e_f, e_b, h0_f, h0_b, tq):
    bsz, seq, _ = xbc.shape
    d_inner = e_f.shape[-1]
    gw = d_inner // SSD_GROUPS
    n_blk = seq // tq
    cpb = tq // CHUNK
    fwd = lambda i: i
    bwd = lambda i: n_blk - 1 - i
    n_xb = d_inner + SSD_GROUPS * SSD_STATE
    tile = lambda n, blk: pl.BlockSpec((None, tq, n), lambda b, i: (b, blk(i), 0))
    const = lambda shape: pl.BlockSpec(shape, lambda b, i: (0,) * len(shape))
    state = pl.BlockSpec((None, SSD_GROUPS, SSD_STATE, gw), lambda b, i: (b, 0, 0, 0))
    hin = lambda blk: pl.BlockSpec((None, cpb, SSD_GROUPS, SSD_STATE, gw),
                                   lambda b, i: (b, blk(i), 0, 0, 0))
    hin_shape = jax.ShapeDtypeStruct((bsz, seq // CHUNK, SSD_GROUPS, SSD_STATE, gw), BF16)
    hfin_shape = jax.ShapeDtypeStruct((bsz, SSD_GROUPS, SSD_STATE, gw), F32)
    return pl.pallas_call(
        _ssd_state_kernel,
        out_shape=(hin_shape, hfin_shape, hin_shape, hfin_shape),
        grid=(bsz, n_blk),
        in_specs=[
            tile(n_xb, fwd), tile(LANES, fwd), tile(n_xb, bwd), tile(LANES, bwd),
            const((1, LANES)), const(e_f.shape), const(e_b.shape), state, state,
        ],
        out_specs=(hin(fwd), state, hin(bwd), state),
        scratch_shapes=[pltpu.VMEM((SSD_GROUPS, SSD_STATE, gw), F32),
                        pltpu.VMEM((tq, d_inner), BF16),
                        pltpu.VMEM((SSD_GROUPS, SSD_STATE, gw), F32),
                        pltpu.VMEM((tq, d_inner), BF16)],
        compiler_params=_cparams("parallel", "arbitrary"),
        name="ssd_state",
    )(xbc, dt, xbc, dt, alog_p, e_f, e_b, h0_f, h0_b)


def _mix_tile(xbc_ref, dt_ref, zs_ref, gate_ref, v_ref, hinf_ref, hinb_ref, alog_ref, ef_ref,
              eb_ref, dskip_ref, ng_ref, sgw_ref, sgb_ref, wout_ref, ycat_ref, per_chunk=None):
    tq = dt_ref.shape[0]
    n_chunks = tq // CHUNK
    d_inner = ef_ref.shape[-1]
    gw = d_inner // SSD_GROUPS
    hpg = SSD_HEADS // SSD_GROUPS
    a2 = -jnp.exp(alog_ref[...]) * LOG2E
    lane = lax.broadcasted_iota(jnp.int32, (CHUNK, LANES), 1)
    ii = lax.broadcasted_iota(jnp.int32, (CHUNK, CHUNK), 0)
    jj = lax.broadcasted_iota(jnp.int32, (CHUNK, CHUNK), 1)
    low = ii >= jj
    diag = ii == jj
    lane_lo = lane < SSD_HEAD_DIM
    lane2 = lax.broadcasted_iota(jnp.int32, (CHUNK, 2 * LANES), 1)
    lane2_lo = lane2 < LANES

    def chunk(c, carry):
        r0 = c * CHUNK if isinstance(c, int) else pl.multiple_of(c * CHUNK, CHUNK)
        rows = pl.ds(r0, CHUNK)
        dt = dt_ref[rows, :]
        da = dt * a2
        cs = jnp.where(lane < SSD_HEADS, _cumsum_rows(da, False), _cumsum_rows(da, True))
        ecs = jnp.exp2(cs)
        scale_f = _expand_heads(ecs, ef_ref[...], 1)
        scale_b = _expand_heads(ecs, eb_ref[...], 1)
        rw_t = (cs - jnp.log(dt) * LOG2E).T
        xs = xbc_ref[rows, 0:d_inner]
        dt_t = dt.T
        for g in range(SSD_GROUPS):
            bm = xbc_ref[rows, d_inner + g * SSD_STATE:d_inner + (g + 1) * SSD_STATE]
            o_c = d_inner + (SSD_GROUPS + g) * SSD_STATE
            cm = xbc_ref[rows, o_c:o_c + SSD_STATE]
            scores = lax.dot_general(cm, bm, (((1,), (1,)), ((), ())),
                                     preferred_element_type=F32)
            cols = slice(g * gw, (g + 1) * gw)
            y_g = (_dot(cm, hinf_ref[c, g]) * scale_f[:, cols]
                   + _dot(cm, hinb_ref[c, g]) * scale_b[:, cols])
            ys = []
            for pair in range(hpg // 2):
                ws = []
                for hh in range(2):
                    hd = g * hpg + pair * 2 + hh
                    hb = SSD_HEADS + hd
                    seg = jnp.where(low, cs[:, hd:hd + 1] - rw_t[hd:hd + 1, :],
                                    cs[:, hb:hb + 1] - rw_t[hb:hb + 1, :])
                    decay = jnp.exp2(seg) + jnp.where(diag, dt_t[hb:hb + 1, :], 0.0)
                    ws.append((scores * decay).astype(BF16))
                p0 = (g * hpg + pair * 2) * SSD_HEAD_DIM
                xp = xs[:, p0:p0 + 2 * SSD_HEAD_DIM]
                zero = jnp.zeros_like(xp)
                rhs = jnp.concatenate([jnp.where(lane_lo, xp, zero),
                                       jnp.where(lane_lo, zero, xp)], axis=0)
                ys.append(_dot(jnp.concatenate(ws, axis=1), rhs))
            y_g = y_g + jnp.concatenate(ys, axis=1)
            y_g = y_g + dskip_ref[:, cols] * xs[:, cols].astype(F32)
            y_g = y_g * zs_ref[rows, cols].astype(F32)
            ms = jnp.mean(y_g * y_g, axis=-1, keepdims=True)
            y_g = y_g * lax.rsqrt(ms + EPS) * ng_ref[:, cols]
            ycat_ref[rows, cols] = y_g.astype(BF16)
        for sp in range(SG_GROUPS // 2):
            cols = slice(sp * 2 * SG_GROUP_DIM, (sp + 1) * 2 * SG_GROUP_DIM)
            vp = v_ref[rows, cols]
            zero = jnp.zeros_like(vp)
            rhs = jnp.concatenate([jnp.where(lane2_lo, vp, zero),
                                   jnp.where(lane2_lo, zero, vp)], axis=0)
            s = _dot(sgw_ref[sp], rhs) + sgb_ref[:, cols]
            ycat_ref[rows, d_inner + sp * 2 * SG_GROUP_DIM:d_inner + (sp + 1) * 2 * SG_GROUP_DIM] = (
                gate_ref[rows, cols].astype(F32) * s).astype(BF16)
        return carry

    if per_chunk is None:
        lax.fori_loop(0, n_chunks, chunk, 0)
    else:
        for c in range(n_chunks):
            per_chunk(c)
            chunk(c, 0)
    return _dot(ycat_ref[...], wout_ref[...])


def _mix_out_kernel(h_ref, xbc_ref, dt_ref, zs_ref, gate_ref, v_ref, hinf_ref, hinb_ref,
                    alog_ref, ef_ref, eb_ref, dskip_ref, ng_ref, sgw_ref, sgb_ref, wout_ref,
                    mod_ref, o_ref, ycat_ref):
    out = _mix_tile(xbc_ref, dt_ref, zs_ref, gate_ref, v_ref, hinf_ref, hinb_ref, alog_ref, ef_ref,
                    eb_ref, dskip_ref, ng_ref, sgw_ref, sgb_ref, wout_ref, ycat_ref)
    o_ref[...] = h_ref[...] + _mod_vec(mod_ref, 2) * out


def _mix_mlp_kernel(h_ref, xbc_ref, dt_ref, zs_ref, gate_ref, v_ref, hinf_ref, hinb_ref,
                    alog_ref, ef_ref, eb_ref, dskip_ref, ng_ref, sgw_ref, sgb_ref, wout_ref,
                    moda_ref, modb_ref, mg_ref, mw1_ref, mw2_ref, fg_ref, o_ref, ycat_ref,
                    hmid_ref, *, final_norm):
    s = pl.program_id(0)
    slot = lax.rem(s + 1, 2)
    n_chunks = dt_ref.shape[0] // CHUNK
    step = mw1_ref.shape[1] // n_chunks

    @pl.when(s == 0)
    def _():
        hmid_ref[...] = jnp.zeros_like(hmid_ref)

    hm = hmid_ref[slot]
    um = _modulate(hm, mg_ref[...], _mod_vec(modb_ref, 3), _mod_vec(modb_ref, 4)).astype(BF16)
    parts = []

    def mlp_block(c):
        r = jnp.maximum(_dot(um, mw1_ref[:, c * step:(c + 1) * step]), 0.0)
        parts.append(_dot((r * r).astype(BF16), mw2_ref[c * step:(c + 1) * step, :]))

    out = _mix_tile(xbc_ref, dt_ref, zs_ref, gate_ref, v_ref, hinf_ref, hinb_ref, alog_ref, ef_ref,
                    eb_ref, dskip_ref, ng_ref, sgw_ref, sgb_ref, wout_ref, ycat_ref,
                    per_chunk=mlp_block)
    mlp = parts[0]
    for part in parts[1:]:
        mlp = mlp + part
    res = hm + _mod_vec(modb_ref, 5) * mlp
    if final_norm:
        ms = jnp.mean(res * res, axis=-1, keepdims=True)
        res = res * lax.rsqrt(ms + EPS) * fg_ref[...]
    o_ref[...] = res
    hmid_ref[lax.rem(s, 2)] = h_ref[...] + _mod_vec(moda_ref, 2) * out


def _mix_out_call(h, xbc, dt, zs, gate, v, hin_f, hin_b, alog_p, e_f, e_b, dskip, norm_g,
                  sg_w, sg_b_exp, w_out, mod, mod_row, tq):
    bsz, seq, d = h.shape
    d_inner = e_f.shape[-1]
    gw = d_inner // SSD_GROUPS
    cpb = tq // CHUNK
    const = lambda shape: pl.BlockSpec(shape, lambda b, i: (0,) * len(shape))
    tile = lambda n: pl.BlockSpec((None, tq, n), lambda b, i: (b, i, 0))
    hin_spec = pl.BlockSpec((None, cpb, SSD_GROUPS, SSD_STATE, gw), lambda b, i: (b, i, 0, 0, 0))
    return pl.pallas_call(
        _mix_out_kernel,
        out_shape=jax.ShapeDtypeStruct((bsz, seq, d), F32),
        grid=(bsz, seq // tq),
        in_specs=[
            tile(d), tile(xbc.shape[-1]), tile(LANES), tile(d_inner), tile(d), tile(d),
            hin_spec, hin_spec,
            const((1, LANES)), const(e_f.shape), const(e_b.shape),
            const((1, d_inner)), const((1, d_inner)),
            const(sg_w.shape), const(sg_b_exp.shape), _resident(w_out.shape),
            pl.BlockSpec((None, 1, mod.shape[-1]), lambda b, i: (mod_row(b), 0, 0)),
        ],
        out_specs=tile(d),
        scratch_shapes=[pltpu.VMEM((tq, w_out.shape[0]), BF16)],
        compiler_params=_cparams("parallel", "parallel"),
        name="hy_mix_out",
    )(h, xbc, dt, zs, gate, v, hin_f, hin_b, alog_p, e_f, e_b, dskip, norm_g.reshape(1, d_inner),
      sg_w, sg_b_exp, w_out, mod)


def _mix_mlp_call(h, xbc, dt, zs, gate, v, hin_f, hin_b, alog_p, e_f, e_b, dskip, norm_g,
                  sg_w, sg_b_exp, w_out, mod, mod_row, mg, mw1, mw2, final_g, final_norm, tq):
    bsz, seq, d = h.shape
    d_inner = e_f.shape[-1]
    gw = d_inner // SSD_GROUPS
    cpb = tq // CHUNK
    n_tiles = seq // tq
    total = bsz * n_tiles
    cur = lambda s: jnp.minimum(s, total - 1)
    done = lambda s: jnp.maximum(s - 1, 0)
    const = lambda shape: pl.BlockSpec(shape, lambda s: (0,) * len(shape))
    tile = lambda n: pl.BlockSpec((None, tq, n), lambda s: (cur(s) // n_tiles, cur(s) % n_tiles, 0))
    hin_spec = pl.BlockSpec((None, cpb, SSD_GROUPS, SSD_STATE, gw),
                            lambda s: (cur(s) // n_tiles, cur(s) % n_tiles, 0, 0, 0))
    mod_spec = lambda t: pl.BlockSpec((None, 1, mod.shape[-1]),
                                      lambda s: (mod_row(t(s) // n_tiles), 0, 0))
    return pl.pallas_call(
        functools.partial(_mix_mlp_kernel, final_norm=final_norm),
        out_shape=jax.ShapeDtypeStruct((bsz, seq, d), F32),
        grid=(total + 1,),
        in_specs=[
            tile(d), tile(xbc.shape[-1]), tile(LANES), tile(d_inner), tile(d), tile(d),
            hin_spec, hin_spec,
            const((1, LANES)), const(e_f.shape), const(e_b.shape),
            const((1, d_inner)), const((1, d_inner)),
            const(sg_w.shape), const(sg_b_exp.shape), _resident(w_out.shape),
            mod_spec(cur), mod_spec(done),
            const((1, d)), _resident(mw1.shape), _resident(mw2.shape), const((1, d)),
        ],
        out_specs=pl.BlockSpec((None, tq, d), lambda s: (done(s) // n_tiles, done(s) % n_tiles, 0)),
        scratch_shapes=[pltpu.VMEM((tq, w_out.shape[0]), BF16), pltpu.VMEM((2, tq, d), F32)],
        compiler_params=_cparams("arbitrary"),
        name="mix_mlp_final" if final_norm else "mix_mlp",
    )(h, xbc, dt, zs, gate, v, hin_f, hin_b, alog_p, e_f, e_b, dskip, norm_g.reshape(1, d_inner),
      sg_w, sg_b_exp, w_out, mod, mod, mg.reshape(1, d), mw1, mw2, final_g.reshape(1, d))


def _conformer_kernel(hp_ref, h_ref, hn_ref, mod_ref, g_ref, w1_ref, b1_ref, wd_ref, bd_ref,
                      lng_ref, lnb_ref, w2_ref, b2_ref, o_ref):
    tm, d = h_ref.shape
    cw = w2_ref.shape[0]
    ext = _ext_rows(hp_ref, h_ref, hn_ref)
    u = _modulate(ext, g_ref[...], _mod_vec(mod_ref, 0), _mod_vec(mod_ref, 1)).astype(BF16)
    valid = _ext_valid(tm)
    accs = []
    for c0 in range(0, cw, CONF_COLS):
        cols = slice(c0, c0 + CONF_COLS)
        gcols = slice(cw + c0, cw + c0 + CONF_COLS)
        a = (_dot(u, w1_ref[:, cols]) + b1_ref[:, cols]) * _sigmoid(
            _dot(u, w1_ref[:, gcols]) + b1_ref[:, gcols])
        a = jnp.where(valid, a, 0.0)
        accs.append(_dwconv_rows(a, wd_ref, bd_ref[:, cols], HALO - CONF_CONV // 2, CONF_CONV, tm,
                                 cols))
    acc = jnp.concatenate(accs, axis=1)
    y = _silu(_layernorm(acc, lng_ref[...], lnb_ref[...])).astype(BF16)
    out = _dot(y, w2_ref[...]) + b2_ref[...]
    o_ref[...] = h_ref[...] + _mod_vec(mod_ref, 2) * out


def _conformer_call(h, mod, mod_row, norm_g, w1, b1, wd, bd, ln_g, ln_b, w2, b2, tm):
    bsz, seq, d = h.shape
    n_tiles = seq // tm
    cw = w2.shape[0]
    hv = h.reshape(bsz, seq // HALO, HALO, d)
    prev, nxt = _halo_specs(tm, n_tiles, d)
    const = lambda shape: pl.BlockSpec(shape, lambda b, i: (0,) * len(shape))
    tile = pl.BlockSpec((None, tm, d), lambda b, i: (b, i, 0))
    return pl.pallas_call(
        _conformer_kernel,
        out_shape=jax.ShapeDtypeStruct((bsz, seq, d), F32),
        grid=(bsz, n_tiles),
        in_specs=[
            prev, tile, nxt,
            pl.BlockSpec((None, 1, mod.shape[-1]), lambda b, i: (mod_row(b), 0, 0)),
            const((1, d)), _resident(w1.shape), const((1, 2 * cw)), const(wd.shape),
            const((1, cw)), const((1, cw)), const((1, cw)), _resident(w2.shape), const((1, d)),
        ],
        out_specs=tile,
        compiler_params=_cparams("parallel", "parallel"),
        name="conformer",
    )(hv, h, hv, mod, norm_g.reshape(1, d), w1, b1.reshape(1, 2 * cw), wd, bd.reshape(1, cw),
      ln_g.reshape(1, cw), ln_b.reshape(1, cw), w2, b2.reshape(1, d))


def _mlp_kernel(h_ref, mod_ref, g_ref, w1_ref, w2_ref, fg_ref, o_ref, a_ref, *, final_norm):
    hf = h_ref[...]
    u = _modulate(hf, g_ref[...], _mod_vec(mod_ref, 3), _mod_vec(mod_ref, 4)).astype(BF16)
    hid = w1_ref.shape[1]
    step = min(hid, 1024)
    for k in range(0, hid, step):
        r = jnp.maximum(_dot(u, w1_ref[:, k:k + step]), 0.0)
        a_ref[:, k:k + step] = (r * r).astype(BF16)
    out = hf + _mod_vec(mod_ref, 5) * _dot(a_ref[...], w2_ref[...])
    if final_norm:
        ms = jnp.mean(out * out, axis=-1, keepdims=True)
        out = out * lax.rsqrt(ms + EPS) * fg_ref[...]
    o_ref[...] = out


def _mlp_call(h, mod, mod_row, norm_g, w1, w2, final_g, final_norm, tm):
    bsz, seq, d = h.shape
    const = lambda shape: pl.BlockSpec(shape, lambda b, i: (0,) * len(shape))
    tile = pl.BlockSpec((None, tm, d), lambda b, i: (b, i, 0))
    return pl.pallas_call(
        functools.partial(_mlp_kernel, final_norm=final_norm),
        out_shape=jax.ShapeDtypeStruct((bsz, seq, d), F32),
        grid=(bsz, seq // tm),
        in_specs=[
            tile,
            pl.BlockSpec((None, 1, mod.shape[-1]), lambda b, i: (mod_row(b), 0, 0)),
            const((1, d)), _resident(w1.shape), _resident(w2.shape), const((1, d)),
        ],
        out_specs=tile,
        scratch_shapes=[pltpu.VMEM((tm, w1.shape[1]), BF16)],
        compiler_params=_cparams("parallel", "parallel"),
        name="mlp_final" if final_norm else "mlp",
    )(h, mod, norm_g.reshape(1, d), w1, w2, final_g.reshape(1, d))


def _conf_mlp_kernel(hp_ref, h_ref, hn_ref, moda_ref, modb_ref, g_ref, w1_ref, b1_ref, wd_ref,
                     bd_ref, lng_ref, lnb_ref, w2_ref, b2_ref, mg_ref, mw1_ref, mw2_ref, fg_ref,
                     o_ref, hmid_ref, *, n_tiles, final_norm):
    s = pl.program_id(0)
    tm, d = h_ref.shape
    cw = w2_ref.shape[0]
    slot = lax.rem(s + 1, 2)

    @pl.when(s == 0)
    def _():
        hmid_ref[...] = jnp.zeros_like(hmid_ref)

    hm = hmid_ref[slot]
    um = _modulate(hm, mg_ref[...], _mod_vec(modb_ref, 3), _mod_vec(modb_ref, 4)).astype(BF16)
    ext = _ext_rows(hp_ref, h_ref, hn_ref)
    uc = _modulate(ext, g_ref[...], _mod_vec(moda_ref, 0), _mod_vec(moda_ref, 1)).astype(BF16)
    i = lax.rem(jnp.minimum(s, pl.num_programs(0) - 2), n_tiles)
    row = lax.broadcasted_iota(jnp.int32, (tm + 2 * HALO, 1), 0)
    valid = jnp.logical_and(jnp.logical_or(row >= HALO, i > 0),
                            jnp.logical_or(row < HALO + tm, i < n_tiles - 1))

    hid = mw1_ref.shape[1]
    n_blk = cw // CONF_COLS
    step = hid // n_blk
    accs = []
    mlp = None
    for j in range(n_blk):
        cols = slice(j * CONF_COLS, (j + 1) * CONF_COLS)
        gcols = slice(cw + j * CONF_COLS, cw + (j + 1) * CONF_COLS)
        r = jnp.maximum(_dot(um, mw1_ref[:, j * step:(j + 1) * step]), 0.0)
        part = _dot((r * r).astype(BF16), mw2_ref[j * step:(j + 1) * step, :])
        mlp = part if mlp is None else mlp + part
        a = (_dot(uc, w1_ref[:, cols]) + b1_ref[:, cols]) * _sigmoid(
            _dot(uc, w1_ref[:, gcols]) + b1_ref[:, gcols])
        a = jnp.where(valid, a, 0.0)
        accs.append(_dwconv_rows(a, wd_ref, bd_ref[:, cols], HALO - CONF_CONV // 2, CONF_CONV, tm,
                                 cols))

    out = hm + _mod_vec(modb_ref, 5) * mlp
    if final_norm:
        ms = jnp.mean(out * out, axis=-1, keepdims=True)
        out = out * lax.rsqrt(ms + EPS) * fg_ref[...]
    o_ref[...] = out

    acc = jnp.concatenate(accs, axis=1)
    y = _silu(_layernorm(acc, lng_ref[...], lnb_ref[...])).astype(BF16)
    outc = _dot(y, w2_ref[...]) + b2_ref[...]
    hmid_ref[lax.rem(s, 2)] = h_ref[...] + _mod_vec(moda_ref, 2) * outc


def _conf_mlp_call(h, mod, mod_row, cg, w1, b1, wd, bd, ln_g, ln_b, w2, b2, mg, mw1, mw2, final_g,
                   final_norm, tm):
    bsz, seq, d = h.shape
    n_tiles = seq // tm
    total = bsz * n_tiles
    cw = w2.shape[0]
    per = tm // HALO
    hv = h.reshape(bsz, seq // HALO, HALO, d)
    cur = lambda s: jnp.minimum(s, total - 1)
    done = lambda s: jnp.maximum(s - 1, 0)
    const = lambda shape: pl.BlockSpec(shape, lambda s: (0,) * len(shape))
    mod_spec = lambda t: pl.BlockSpec((None, 1, mod.shape[-1]),
                                      lambda s: (mod_row(t(s) // n_tiles), 0, 0))
    return pl.pallas_call(
        functools.partial(_conf_mlp_kernel, n_tiles=n_tiles, final_norm=final_norm),
        out_shape=jax.ShapeDtypeStruct((bsz, seq, d), F32),
        grid=(total + 1,),
        in_specs=[
            pl.BlockSpec((None, None, HALO, d), lambda s: (
                cur(s) // n_tiles, jnp.maximum(cur(s) % n_tiles * per - 1, 0), 0, 0)),
            pl.BlockSpec((None, tm, d), lambda s: (cur(s) // n_tiles, cur(s) % n_tiles, 0)),
            pl.BlockSpec((None, None, HALO, d), lambda s: (
                cur(s) // n_tiles, jnp.minimum((cur(s) % n_tiles + 1) * per, n_tiles * per - 1),
                0, 0)),
            mod_spec(cur), mod_spec(done),
            const((1, d)), _resident(w1.shape), const((1, 2 * cw)), const(wd.shape),
            const((1, cw)), const((1, cw)), const((1, cw)), _resident(w2.shape), const((1, d)),
            const((1, d)), _resident(mw1.shape), _resident(mw2.shape), const((1, d)),
        ],
        out_specs=pl.BlockSpec((None, tm, d), lambda s: (done(s) // n_tiles, done(s) % n_tiles, 0)),
        scratch_shapes=[pltpu.VMEM((2, tm, d), F32)],
        compiler_params=_cparams("arbitrary"),
        name="conf_mlp_final" if final_norm else "conf_mlp",
    )(hv, h, hv, mod, mod, cg.reshape(1, d), w1, b1.reshape(1, 2 * cw), wd, bd.reshape(1, cw),
      ln_g.reshape(1, cw), ln_b.reshape(1, cw), w2, b2.reshape(1, d), mg.reshape(1, d), mw1, mw2,
      final_g.reshape(1, d))


def _tile(seq, want):
    return min(seq, want)


def _hybrid_mixer(h, mod, mod_row, norm_g, p, h0_f, h0_b, need_out, mlp=None):
    seq = h.shape[1]
    zs, xbc, dt, gate, v = _inproj_call(
        h, mod, mod_row, norm_g, p["w_in"], p["conv_w"], p["conv_b"], p["dt_bias"],
        p["ln_g"], p["ln_b"], _tile(seq, TOKEN_TILE))
    ts = _tile(seq, SCAN_TILE)
    tq = _tile(seq, TOKEN_TILE)
    hin_f, hfin_f, hin_b, hfin_b = _ssd_state_call(xbc, dt, p["alog"], p["e_f"], p["e_b"], h0_f,
                                                   h0_b, ts)
    out = None
    args = (h, xbc, dt, zs, gate, v, hin_f, hin_b, p["alog"], p["e_f"], p["e_b"], p["dskip"],
            p["norm_g"], p["sg_w"], p["sg_b"], p["w_out"], mod, mod_row)
    if mlp is not None:
        out = _mix_mlp_call(*args, *mlp, tq)
    elif need_out:
        out = _mix_out_call(*args, tq)
    return out, hfin_f, hfin_b


def _hybrid_params(j, hy_w_in, ssd_conv_w, ssd_conv_b, ssd_dt_bias, ssd_a_log, ssd_d, ssd_norm_g,
                   sg_ln_g, sg_ln_b, sg_w, sg_b, hy_w_out):
    d_inner = SSD_HEADS * SSD_HEAD_DIM
    n_xbc = d_inner + 2 * SSD_GROUPS * SSD_STATE
    w = hy_w_in[j]
    o2 = d_inner + n_xbc
    o3 = o2 + 2 * SSD_HEADS
    pad = jnp.zeros((w.shape[0], LANES - 2 * SSD_HEADS), w.dtype)
    w_in_r = jnp.concatenate([w[:, :o2], w[:, o3:], w[:, o2:o3], pad], axis=1).astype(BF16)
    lane_pad = lambda t: jnp.pad(t.reshape(1, -1), ((0, 0), (0, LANES - t.size)))
    head = jnp.arange(d_inner, dtype=jnp.int32) // SSD_HEAD_DIM
    row = jnp.arange(LANES, dtype=jnp.int32)[:, None]
    sg_width = SG_GROUPS * SG_GROUP_DIM
    sg_b_exp = jnp.broadcast_to(sg_b[j].T[:, :, None],
                                (CHUNK, SG_GROUPS, SG_GROUP_DIM)).reshape(CHUNK, sg_width)
    sg_pairs = sg_w[j].reshape(SG_GROUPS // 2, 2, CHUNK, CHUNK).transpose(0, 2, 1, 3).reshape(
        SG_GROUPS // 2, CHUNK, 2 * CHUNK)
    return dict(
        w_in=w_in_r, conv_w=ssd_conv_w[j], conv_b=ssd_conv_b[j],
        dt_bias=lane_pad(ssd_dt_bias[j]), alog=lane_pad(ssd_a_log[j]),
        e_f=(row == head[None, :]).astype(BF16),
        e_b=(row == head[None, :] + SSD_HEADS).astype(BF16),
        dskip=jnp.repeat(ssd_d[j], SSD_HEAD_DIM).reshape(1, d_inner),
        norm_g=ssd_norm_g[j], ln_g=sg_ln_g[j], ln_b=sg_ln_b[j],
        sg_w=sg_pairs.astype(BF16), sg_b=sg_b_exp, w_out=hy_w_out[j].astype(BF16))


def kernel(x, c, ctx, c_ctx, ada_w, ada_b, norm_mix_g, norm_mlp_g, mlp_w1, mlp_w2, hy_w_in,
           ssd_conv_w, ssd_conv_b, ssd_dt_bias, ssd_a_log, ssd_d, ssd_norm_g, sg_ln_g, sg_ln_b,
           sg_w, sg_b, hy_w_out, cf_w_pw1, cf_b_pw1, cf_w_dw, cf_b_dw, cf_ln_g, cf_ln_b, cf_w_pw2,
           cf_b_pw2, final_norm_g):
    bsz, seq, d = x.shape
    depth = ada_w.shape[0]
    assert bsz < MOD_ROWS and seq % 512 == 0 and ctx.shape[1] % (2 * CHUNK) == 0
    cs = jnp.concatenate([c, c_ctx[None, :], jnp.zeros((MOD_ROWS - bsz - 1, d), c.dtype)], axis=0)
    mod = _ada_call(cs, ada_w, ada_b)

    gw = SSD_HEADS * SSD_HEAD_DIM // SSD_GROUPS
    zeros = jnp.zeros((bsz, SSD_GROUPS, SSD_STATE, gw), F32)
    h, hc = x, ctx
    for i in range(depth):
        ctx_needed = i < depth - 1
        last = i == depth - 1
        lat_row = lambda b, i=i: i * MOD_ROWS + b
        ctx_row = lambda b, i=i: i * MOD_ROWS + bsz
        j = i // 2
        if i % 2 == 0:
            p = _hybrid_params(j, hy_w_in, ssd_conv_w, ssd_conv_b, ssd_dt_bias, ssd_a_log, ssd_d,
                               ssd_norm_g, sg_ln_g, sg_ln_b, sg_w, sg_b, hy_w_out)
            hc_new, h0_f, h0_b = _hybrid_mixer(hc, mod, ctx_row, norm_mix_g[i], p, zeros, zeros,
                                               ctx_needed)
            h, _, _ = _hybrid_mixer(h, mod, lat_row, norm_mix_g[i], p, h0_f, h0_b, True,
                                    mlp=(norm_mlp_g[i], mlp_w1[i].astype(BF16),
                                         mlp_w2[i].astype(BF16), final_norm_g, last))
            if ctx_needed:
                hc = hc_new
        else:
            w1 = cf_w_pw1[j].astype(BF16)
            w2 = cf_w_pw2[j].astype(BF16)
            args = (w1, cf_b_pw1[j], cf_w_dw[j], cf_b_dw[j], cf_ln_g[j], cf_ln_b[j], w2,
                    cf_b_pw2[j])
            h = _conf_mlp_call(h, mod, lat_row, norm_mix_g[i], *args, norm_mlp_g[i],
                               mlp_w1[i].astype(BF16), mlp_w2[i].astype(BF16), final_norm_g, last,
                               _tile(seq, TOKEN_TILE))
            if ctx_needed:
                hc = _conformer_call(hc, mod, ctx_row, norm_mix_g[i], *args,
                                     _tile(hc.shape[1], TOKEN_TILE))
        if ctx_needed:
            hc = _mlp_call(hc, mod, ctx_row, norm_mlp_g[i], mlp_w1[i].astype(BF16),
                           mlp_w2[i].astype(BF16), final_norm_g, False,
                           _tile(hc.shape[1], MLP_TILE))
    return h
```

```python
import functools

import jax
import jax.numpy as jnp
from jax import lax
from jax.experimental import pallas as pl
from jax.experimental.pallas import tpu as pltpu

F32 = jnp.float32
BF16 = jnp.bfloat16

SSD_HEADS = 16
SSD_HEAD_DIM = 64
SSD_GROUPS = 2
SSD_STATE = 128
SSD_CONV = 5
CHUNK = 128
SG_GROUPS = 8
SG_GROUP_DIM = 128
CONF_CONV = 31
EPS = 1e-6
LOG2E = 1.4426950408889634
N_MOD = 6
MOD_ROWS = 8
HALO = 16
LANES = 128
VMEM_LIMIT = 60 * 1024 * 1024
TOKEN_TILE = 512
MLP_TILE = 1024
CONF_COLS = 256
SCAN_TILE = 1024


def _cparams(*sem):
    return pltpu.CompilerParams(dimension_semantics=sem, vmem_limit_bytes=VMEM_LIMIT)


def _dot(a, b):
    return jnp.dot(a, b, preferred_element_type=F32)


def _sigmoid(x):
    return 1.0 / (1.0 + jnp.exp(-x))


def _silu(x):
    return x * _sigmoid(x)


def _gelu_tanh(x):
    c = 0.7978845608028654
    return x * (0.5 + 0.5 * jnp.tanh(x * (c + (c * 0.044715) * (x * x))))


def _softplus(x):
    return jnp.maximum(x, 0.0) + jnp.log1p(jnp.exp(-jnp.abs(x)))


def _modulate(hf, g, shift, scale):
    ms = jnp.mean(hf * hf, axis=-1, keepdims=True)
    return (hf * lax.rsqrt(ms + EPS)) * (g * (1.0 + scale)) + shift


def _layernorm(x, g, b):
    mu = jnp.mean(x, axis=-1, keepdims=True)
    xc = x - mu
    var = jnp.mean(xc * xc, axis=-1, keepdims=True)
    return xc * lax.rsqrt(var + EPS) * g + b


def _mod_vec(mod_ref, k):
    d = mod_ref.shape[-1] // N_MOD
    return mod_ref[:, k * d:(k + 1) * d]


def _cumsum_rows(x, reverse):
    n = x.shape[0]
    row = lax.broadcasted_iota(jnp.int32, x.shape, 0)
    s = 1
    while s < n:
        if reverse:
            x = x + jnp.where(row < n - s, pltpu.roll(x, n - s, axis=0), 0.0)
        else:
            x = x + jnp.where(row >= s, pltpu.roll(x, s, axis=0), 0.0)
        s *= 2
    return x


def _expand_heads(x, e, terms):
    out = None
    r = x
    for _ in range(terms):
        p = r.astype(BF16)
        t = _dot(p, e)
        out = t if out is None else out + t
        r = r - p.astype(F32)
    return out


def _dwconv_rows(ext, w_ref, bias, start, taps, tm, cols=slice(None)):
    n = ext.shape[0]
    acc = None
    for s in range(8):
        ks = [k for k in range(taps) if (start + k) % 8 == s]
        if not ks:
            continue
        rolled = ext if s == 0 else pltpu.roll(ext, n - s, axis=0)
        for k in ks:
            base = start + k - s
            term = w_ref[k:k + 1, cols] * rolled[base:base + tm]
            acc = term if acc is None else acc + term
    return acc + bias


def _resident(shape):
    return pl.BlockSpec(shape, lambda *_: (0,) * len(shape), pipeline_mode=pl.Buffered(1))


def _ada_kernel(cs_ref, w_ref, b_ref, o_ref):
    s = _silu(cs_ref[...]).astype(BF16)
    o_ref[...] = _dot(s, w_ref[...].astype(BF16)) + b_ref[...]


def _ada_call(cs, ada_w, ada_b):
    depth, d, n = ada_w.shape
    tn = n // 4
    out = pl.pallas_call(
        _ada_kernel,
        out_shape=jax.ShapeDtypeStruct((depth, MOD_ROWS, n), F32),
        grid=(depth, n // tn),
        in_specs=[
            pl.BlockSpec((MOD_ROWS, d), lambda l, j: (0, 0)),
            pl.BlockSpec((None, d, tn), lambda l, j: (l, 0, j)),
            pl.BlockSpec((None, 1, tn), lambda l, j: (l, 0, j)),
        ],
        out_specs=pl.BlockSpec((None, MOD_ROWS, tn), lambda l, j: (l, 0, j)),
        compiler_params=_cparams("parallel", "parallel"),
        name="ada_mod",
    )(cs, ada_w, ada_b.reshape(depth, 1, n))
    return out.reshape(depth * MOD_ROWS, 1, n)


def _halo_specs(tm, n_tiles, d):
    per = tm // HALO
    last = n_tiles * per - 1
    prev = pl.BlockSpec((None, None, HALO, d),
                        lambda b, i: (b, jnp.maximum(i * per - 1, 0), 0, 0))
    nxt = pl.BlockSpec((None, None, HALO, d),
                       lambda b, i: (b, jnp.minimum((i + 1) * per, last), 0, 0))
    return prev, nxt


def _ext_rows(hp_ref, h_ref, hn_ref):
    return jnp.concatenate([hp_ref[...], h_ref[...], hn_ref[...]], axis=0)


def _ext_valid(tm):
    i = pl.program_id(1)
    n = pl.num_programs(1)
    row = lax.broadcasted_iota(jnp.int32, (tm + 2 * HALO, 1), 0)
    return jnp.logical_and(jnp.logical_or(row >= HALO, i > 0),
                           jnp.logical_or(row < HALO + tm, i < n - 1))


def _inproj_kernel(hp_ref, h_ref, hn_ref, mod_ref, g_ref, w_ref, cw_ref, cb_ref, dtb_ref,
                   lng_ref, lnb_ref, zs_ref, xbc_ref, dt_ref, gate_ref, v_ref):
    tm, d = h_ref.shape
    n_xbc = xbc_ref.shape[-1]
    o_xbc = d
    o_uv = o_xbc + n_xbc
    o_dt = o_uv + 2 * d
    ext = _ext_rows(hp_ref, h_ref, hn_ref)
    u = _modulate(ext, g_ref[...], _mod_vec(mod_ref, 0), _mod_vec(mod_ref, 1))
    u = jnp.where(_ext_valid(tm), u, 0.0).astype(BF16)
    um = u[HALO:HALO + tm]

    xe = _dot(u, w_ref[:, o_xbc:o_uv])
    acc = _dwconv_rows(xe, cw_ref, cb_ref[...], HALO - SSD_CONV // 2, SSD_CONV, tm)
    xbc_ref[...] = _silu(acc).astype(BF16)

    zs_ref[...] = _silu(_dot(um, w_ref[:, 0:d])).astype(BF16)

    lane = lax.broadcasted_iota(jnp.int32, (1, LANES), 1)
    dt = _softplus(_dot(um, w_ref[:, o_dt:o_dt + LANES]) + dtb_ref[...])
    dt_ref[...] = jnp.where(lane < 2 * SSD_HEADS, dt, 0.0)

    gate_ref[...] = _gelu_tanh(_dot(um, w_ref[:, o_uv:o_uv + d])).astype(BF16)
    v = _gelu_tanh(_dot(um, w_ref[:, o_uv + d:o_uv + 2 * d]))
    v_ref[...] = _layernorm(v, lng_ref[...], lnb_ref[...]).astype(BF16)


def _inproj_call(h, mod, mod_row, norm_g, w_in_r, conv_w, conv_b, dt_bias_p, ln_g, ln_b, tm):
    bsz, seq, d = h.shape
    n_tiles = seq // tm
    n_xbc = conv_w.shape[-1]
    hv = h.reshape(bsz, seq // HALO, HALO, d)
    prev, nxt = _halo_specs(tm, n_tiles, d)
    const = lambda shape: pl.BlockSpec(shape, lambda b, i: (0,) * len(shape))
    tile = lambda n: pl.BlockSpec((None, tm, n), lambda b, i: (b, i, 0))
    out_shape = (
        jax.ShapeDtypeStruct((bsz, seq, d), BF16),
        jax.ShapeDtypeStruct((bsz, seq, n_xbc), BF16),
        jax.ShapeDtypeStruct((bsz, seq, LANES), F32),
        jax.ShapeDtypeStruct((bsz, seq, d), BF16),
        jax.ShapeDtypeStruct((bsz, seq, d), BF16),
    )
    return pl.pallas_call(
        _inproj_kernel,
        out_shape=out_shape,
        grid=(bsz, n_tiles),
        in_specs=[
            prev, tile(d), nxt,
            pl.BlockSpec((None, 1, mod.shape[-1]), lambda b, i: (mod_row(b), 0, 0)),
            const((1, d)),
            _resident(w_in_r.shape),
            const(conv_w.shape), const((1, n_xbc)), const((1, LANES)),
            const((1, d)), const((1, d)),
        ],
        out_specs=(tile(d), tile(n_xbc), tile(LANES), tile(d), tile(d)),
        compiler_params=_cparams("parallel", "parallel"),
        name="hy_inproj",
    )(hv, h, hv, mod, norm_g.reshape(1, d), w_in_r, conv_w, conv_b.reshape(1, n_xbc),
      dt_bias_p, ln_g.reshape(1, d), ln_b.reshape(1, d))


def _ssd_scan_block(xbc_ref, dt_ref, alog_ref, e_ref, hin_ref, hs_ref, xw_ref, reverse):
    n_chunks = hin_ref.shape[0]
    d_inner = e_ref.shape[-1]
    gw = d_inner // SSD_GROUPS
    a = -jnp.exp(alog_ref[...])
    e = e_ref[...]
    row8 = lax.broadcasted_iota(jnp.int32, (8, LANES), 0)
    ws = []
    edges = jnp.zeros((8, LANES), F32)
    for c in range(n_chunks):
        dt = dt_ref[c * CHUNK:(c + 1) * CHUNK, :]
        cs = _cumsum_rows(dt * a, reverse)
        edge = cs[0:1, :] if reverse else cs[CHUNK - 1:CHUNK, :]
        ws.append(jnp.exp(edge - cs) * dt)
        edges = jnp.where(row8 == c, edge, edges)
    w = _expand_heads(jnp.concatenate(ws, axis=0), e, 1)
    decay = _expand_heads(jnp.exp(edges), e, 3)
    xw_ref[...] = (xbc_ref[:, 0:d_inner].astype(F32) * w).astype(BF16)
    for t in range(n_chunks):
        c = n_chunks - 1 - t if reverse else t
        rows = slice(c * CHUNK, (c + 1) * CHUNK)
        for g in range(SSD_GROUPS):
            bm = xbc_ref[rows, d_inner + g * SSD_STATE:d_inner + (g + 1) * SSD_STATE]
            s = lax.dot_general(bm, xw_ref[rows, g * gw:(g + 1) * gw], (((0,), (0,)), ((), ())),
                                preferred_element_type=F32)
            hcur = hs_ref[g]
            hin_ref[c, g] = hcur.astype(BF16)
            hs_ref[g] = hcur * decay[c:c + 1, g * gw:(g + 1) * gw] + s


def _ssd_state_kernel(xf_ref, dtf_ref, xb_ref, dtb_ref, alog_ref, ef_ref, eb_ref, h0f_ref, h0b_ref,
                      hinf_ref, hfinf_ref, hinb_ref, hfinb_ref, hsf_ref, xwf_ref, hsb_ref, xwb_ref):
    i = pl.program_id(1)

    @pl.when(i == 0)
    def _():
        hsf_ref[...] = h0f_ref[...]
        hsb_ref[...] = h0b_ref[...]

    _ssd_scan_block(xf_ref, dtf_ref, alog_ref, ef_ref, hinf_ref, hsf_ref, xwf_ref, False)
    _ssd_scan_block(xb_ref, dtb_ref, alog_ref, eb_ref, hinb_ref, hsb_ref, xwb_ref, True)

    @pl.when(i == pl.num_programs(1) - 1)
    def _():
        hfinf_ref[...] = hsf_ref[...]
        hfinb_ref[...] = hsb_ref[...]


def _ssd_state_call(xbc, dt, alog_p, e_f, e_b, h0_f, h0_b, tq):
    bsz, seq, _ = xbc.shape
    d_inner = e_f.shape[-1]
    gw = d_inner // SSD_GROUPS
    n_blk = seq // tq
    cpb = tq // CHUNK
    fwd = lambda i: i
    bwd = lambda i: n_blk - 1 - i
    n_xb = d_inner + SSD_GROUPS * SSD_STATE
    tile = lambda n, blk: pl.BlockSpec((None, tq, n), lambda b, i: (b, blk(i), 0))
    const = lambda shape: pl.BlockSpec(shape, lambda b, i: (0,) * len(shape))
    state = pl.BlockSpec((None, SSD_GROUPS, SSD_STATE, gw), lambda b, i: (b, 0, 0, 0))
    hin = lambda blk: pl.BlockSpec((None, cpb, SSD_GROUPS, SSD_STATE, gw),
                                   lambda b, i: (b, blk(i), 0, 0, 0))
    hin_shape = jax.ShapeDtypeStruct((bsz, seq // CHUNK, SSD_GROUPS, SSD_STATE, gw), BF16)
    hfin_shape = jax.ShapeDtypeStruct((bsz, SSD_GROUPS, SSD_STATE, gw), F32)
    return pl.pallas_call(
        _ssd_state_kernel,
        out_shape=(hin_shape, hfin_shape, hin_shape, hfin_shape),
        grid=(bsz, n_blk),
        in_specs=[
            tile(n_xb, fwd), tile(LANES, fwd), tile(n_xb, bwd), tile(LANES, bwd),
            const((1, LANES)), const(e_f.shape), const(e_b.shape), state, state,
        ],
        out_specs=(hin(fwd), state, hin(bwd), state),
        scratch_shapes=[pltpu.VMEM((SSD_GROUPS, SSD_STATE, gw), F32),
                        pltpu.VMEM((tq, d_inner), BF16),
                        pltpu.VMEM((SSD_GROUPS, SSD_STATE, gw), F32),
                        pltpu.VMEM((tq, d_inner), BF16)],
        compiler_params=_cparams("parallel", "arbitrary"),
        name="ssd_state",
    )(xbc, dt, xbc, dt, alog_p, e_f, e_b, h0_f, h0_b)


def _mix_tile(xbc_ref, dt_ref, zs_ref, gate_ref, v_ref, hinf_ref, hinb_ref, alog_ref, ef_ref,
              eb_ref, dskip_ref, ng_ref, sgw_ref, sgb_ref, wout_ref, ycat_ref, per_chunk=None):
    tq = dt_ref.shape[0]
    n_chunks = tq // CHUNK
    d_inner = ef_ref.shape[-1]
    gw = d_inner // SSD_GROUPS
    hpg = SSD_HEADS // SSD_GROUPS
    a2 = -jnp.exp(alog_ref[...]) * LOG2E
    lane = lax.broadcasted_iota(jnp.int32, (CHUNK, LANES), 1)
    ii = lax.broadcasted_iota(jnp.int32, (CHUNK, CHUNK), 0)
    jj = lax.broadcasted_iota(jnp.int32, (CHUNK, CHUNK), 1)
    low = ii >= jj
    diag = ii == jj
    lane_lo = lane < SSD_HEAD_DIM
    lane2 = lax.broadcasted_iota(jnp.int32, (CHUNK, 2 * LANES), 1)
    lane2_lo = lane2 < LANES

    def chunk(c, carry):
        r0 = c * CHUNK if isinstance(c, int) else pl.multiple_of(c * CHUNK, CHUNK)
        rows = pl.ds(r0, CHUNK)
        dt = dt_ref[rows, :]
        da = dt * a2
        cs = jnp.where(lane < SSD_HEADS, _cumsum_rows(da, False), _cumsum_rows(da, True))
        rw_t = (cs - jnp.log(dt) * LOG2E).T
        xs = xbc_ref[rows, 0:d_inner]
        dt_t = dt.T
        for g in range(SSD_GROUPS):
            bm = xbc_ref[rows, d_inner + g * SSD_STATE:d_inner + (g + 1) * SSD_STATE]
            o_c = d_inner + (SSD_GROUPS + g) * SSD_STATE
            cm = xbc_ref[rows, o_c:o_c + SSD_STATE]
            scores = lax.dot_general(cm, bm, (((1,), (1,)), ((), ())),
                                     preferred_element_type=F32)
            cols = slice(g * gw, (g + 1) * gw)
            yoff_f = _dot(cm, hinf_ref[c, g])
            yoff_b = _dot(cm, hinb_ref[c, g])
            ys = []
            for pair in range(hpg // 2):
                ws, efs, ebs = [], [], []
                for hh in range(2):
                    hd = g * hpg + pair * 2 + hh
                    hb = SSD_HEADS + hd
                    col_f = jnp.broadcast_to(cs[:, hd:hd + 1], (CHUNK, CHUNK))
                    col_b = jnp.broadcast_to(cs[:, hb:hb + 1], (CHUNK, CHUNK))
                    seg = jnp.where(low, col_f - rw_t[hd:hd + 1, :], col_b - rw_t[hb:hb + 1, :])
                    decay = jnp.exp2(seg) + jnp.where(diag, dt_t[hb:hb + 1, :], 0.0)
                    ws.append((scores * decay).astype(BF16))
                    efs.append(jnp.exp2(col_f))
                    ebs.append(jnp.exp2(col_b))
                p0 = (g * hpg + pair * 2) * SSD_HEAD_DIM
                xp = xs[:, p0:p0 + 2 * SSD_HEAD_DIM]
                zero = jnp.zeros_like(xp)
                rhs = jnp.concatenate([jnp.where(lane_lo, xp, zero),
                                       jnp.where(lane_lo, zero, xp)], axis=0)
                pc = slice(pair * 2 * SSD_HEAD_DIM, (pair + 1) * 2 * SSD_HEAD_DIM)
                ys.append(_dot(jnp.concatenate(ws, axis=1), rhs)
                          + yoff_f[:, pc] * jnp.where(lane_lo, efs[0], efs[1])
                          + yoff_b[:, pc] * jnp.where(lane_lo, ebs[0], ebs[1]))
            y_g = jnp.concatenate(ys, axis=1)
            y_g = y_g + dskip_ref[:, cols] * xs[:, cols].astype(F32)
            y_g = y_g * zs_ref[rows, cols].astype(F32)
            ms = jnp.mean(y_g * y_g, axis=-1, keepdims=True)
            y_g = y_g * lax.rsqrt(ms + EPS) * ng_ref[:, cols]
            ycat_ref[rows, cols] = y_g.astype(BF16)
        for sp in range(SG_GROUPS // 2):
            cols = slice(sp * 2 * SG_GROUP_DIM, (sp + 1) * 2 * SG_GROUP_DIM)
            vp = v_ref[rows, cols]
            zero = jnp.zeros_like(vp)
            rhs = jnp.concatenate([jnp.where(lane2_lo, vp, zero),
                                   jnp.where(lane2_lo, zero, vp)], axis=0)
            s = _dot(sgw_ref[sp], rhs) + sgb_ref[:, cols]
            ycat_ref[rows, d_inner + sp * 2 * SG_GROUP_DIM:d_inner + (sp + 1) * 2 * SG_GROUP_DIM] = (
                gate_ref[rows, cols].astype(F32) * s).astype(BF16)
        return carry

    if per_chunk is None:
        lax.fori_loop(0, n_chunks, chunk, 0)
    else:
        for c in range(n_chunks):
            per_chunk(c)
            chunk(c, 0)
    return _dot(ycat_ref[...], wout_ref[...])


def _mix_out_kernel(h_ref, xbc_ref, dt_ref, zs_ref, gate_ref, v_ref, hinf_ref, hinb_ref,
                    alog_ref, ef_ref, eb_ref, dskip_ref, ng_ref, sgw_ref, sgb_ref, wout_ref,
                    mod_ref, o_ref, ycat_ref):
    out = _mix_tile(xbc_ref, dt_ref, zs_ref, gate_ref, v_ref, hinf_ref, hinb_ref, alog_ref, ef_ref,
                    eb_ref, dskip_ref, ng_ref, sgw_ref, sgb_ref, wout_ref, ycat_ref)
    o_ref[...] = h_ref[...] + _mod_vec(mod_ref, 2) * out


def _mix_mlp_kernel(h_ref, xbc_ref, dt_ref, zs_ref, gate_ref, v_ref, hinf_ref, hinb_ref,
                    alog_ref, ef_ref, eb_ref, dskip_ref, ng_ref, sgw_ref, sgb_ref, wout_ref,
                    moda_ref, modb_ref, mg_ref, mw1_ref, mw2_ref, fg_ref, o_ref, ycat_ref,
                    hmid_ref, *, final_norm):
    s = pl.program_id(0)
    slot = lax.rem(s + 1, 2)
    n_chunks = dt_ref.shape[0] // CHUNK
    step = mw1_ref.shape[1] // n_chunks

    @pl.when(s == 0)
    def _():
        hmid_ref[...] = jnp.zeros_like(hmid_ref)

    hm = hmid_ref[slot]
    um = _modulate(hm, mg_ref[...], _mod_vec(modb_ref, 3), _mod_vec(modb_ref, 4)).astype(BF16)
    parts = []

    def mlp_block(c):
        r = jnp.maximum(_dot(um, mw1_ref[:, c * step:(c + 1) * step]), 0.0)
        parts.append(_dot((r * r).astype(BF16), mw2_ref[c * step:(c + 1) * step, :]))

    out = _mix_tile(xbc_ref, dt_ref, zs_ref, gate_ref, v_ref, hinf_ref, hinb_ref, alog_ref, ef_ref,
                    eb_ref, dskip_ref, ng_ref, sgw_ref, sgb_ref, wout_ref, ycat_ref,
                    per_chunk=mlp_block)
    mlp = parts[0]
    for part in parts[1:]:
        mlp = mlp + part
    res = hm + _mod_vec(modb_ref, 5) * mlp
    if final_norm:
        ms = jnp.mean(res * res, axis=-1, keepdims=True)
        res = res * lax.rsqrt(ms + EPS) * fg_ref[...]
    o_ref[...] = res
    hmid_ref[lax.rem(s, 2)] = h_ref[...] + _mod_vec(moda_ref, 2) * out


def _mix_out_call(h, xbc, dt, zs, gate, v, hin_f, hin_b, alog_p, e_f, e_b, dskip, norm_g,
                  sg_w, sg_b_exp, w_out, mod, mod_row, tq):
    bsz, seq, d = h.shape
    d_inner = e_f.shape[-1]
    gw = d_inner // SSD_GROUPS
    cpb = tq // CHUNK
    const = lambda shape: pl.BlockSpec(shape, lambda b, i: (0,) * len(shape))
    tile = lambda n: pl.BlockSpec((None, tq, n), lambda b, i: (b, i, 0))
    hin_spec = pl.BlockSpec((None, cpb, SSD_GROUPS, SSD_STATE, gw), lambda b, i: (b, i, 0, 0, 0))
    return pl.pallas_call(
        _mix_out_kernel,
        out_shape=jax.ShapeDtypeStruct((bsz, seq, d), F32),
        grid=(bsz, seq // tq),
        in_specs=[
            tile(d), tile(xbc.shape[-1]), tile(LANES), tile(d_inner), tile(d), tile(d),
            hin_spec, hin_spec,
            const((1, LANES)), const(e_f.shape), const(e_b.shape),
            const((1, d_inner)), const((1, d_inner)),
            const(sg_w.shape), const(sg_b_exp.shape), _resident(w_out.shape),
            pl.BlockSpec((None, 1, mod.shape[-1]), lambda b, i: (mod_row(b), 0, 0)),
        ],
        out_specs=tile(d),
        scratch_shapes=[pltpu.VMEM((tq, w_out.shape[0]), BF16)],
        compiler_params=_cparams("parallel", "parallel"),
        name="hy_mix_out",
    )(h, xbc, dt, zs, gate, v, hin_f, hin_b, alog_p, e_f, e_b, dskip, norm_g.reshape(1, d_inner),
      sg_w, sg_b_exp, w_out, mod)


def _mix_mlp_call(h, xbc, dt, zs, gate, v, hin_f, hin_b, alog_p, e_f, e_b, dskip, norm_g,
                  sg_w, sg_b_exp, w_out, mod, mod_row, mg, mw1, mw2, final_g, final_norm, tq):
    bsz, seq, d = h.shape
    d_inner = e_f.shape[-1]
    gw = d_inner // SSD_GROUPS
    cpb = tq // CHUNK
    n_tiles = seq // tq
    total = bsz * n_tiles
    cur = lambda s: jnp.minimum(s, total - 1)
    done = lambda s: jnp.maximum(s - 1, 0)
    const = lambda shape: pl.BlockSpec(shape, lambda s: (0,) * len(shape))
    tile = lambda n: pl.BlockSpec((None, tq, n), lambda s: (cur(s) // n_tiles, cur(s) % n_tiles, 0))
    hin_spec = pl.BlockSpec((None, cpb, SSD_GROUPS, SSD_STATE, gw),
                            lambda s: (cur(s) // n_tiles, cur(s) % n_tiles, 0, 0, 0))
    mod_spec = lambda t: pl.BlockSpec((None, 1, mod.shape[-1]),
                                      lambda s: (mod_row(t(s) // n_tiles), 0, 0))
    return pl.pallas_call(
        functools.partial(_mix_mlp_kernel, final_norm=final_norm),
        out_shape=jax.ShapeDtypeStruct((bsz, seq, d), F32),
        grid=(total + 1,),
        in_specs=[
            tile(d), tile(xbc.shape[-1]), tile(LANES), tile(d_inner), tile(d), tile(d),
            hin_spec, hin_spec,
            const((1, LANES)), const(e_f.shape), const(e_b.shape),
            const((1, d_inner)), const((1, d_inner)),
            const(sg_w.shape), const(sg_b_exp.shape), _resident(w_out.shape),
            mod_spec(cur), mod_spec(done),
            const((1, d)), _resident(mw1.shape), _resident(mw2.shape), const((1, d)),
        ],
        out_specs=pl.BlockSpec((None, tq, d), lambda s: (done(s) // n_tiles, done(s) % n_tiles, 0)),
        scratch_shapes=[pltpu.VMEM((tq, w_out.shape[0]), BF16), pltpu.VMEM((2, tq, d), F32)],
        compiler_params=_cparams("arbitrary"),
        name="mix_mlp_final" if final_norm else "mix_mlp",
    )(h, xbc, dt, zs, gate, v, hin_f, hin_b, alog_p, e_f, e_b, dskip, norm_g.reshape(1, d_inner),
      sg_w, sg_b_exp, w_out, mod, mod, mg.reshape(1, d), mw1, mw2, final_g.reshape(1, d))


def _conformer_kernel(hp_ref, h_ref, hn_ref, mod_ref, g_ref, w1_ref, b1_ref, wd_ref, bd_ref,
                      lng_ref, lnb_ref, w2_ref, b2_ref, o_ref):
    tm, d = h_ref.shape
    cw = w2_ref.shape[0]
    ext = _ext_rows(hp_ref, h_ref, hn_ref)
    u = _modulate(ext, g_ref[...], _mod_vec(mod_ref, 0), _mod_vec(mod_ref, 1)).astype(BF16)
    valid = _ext_valid(tm)
    accs = []
    for c0 in range(0, cw, CONF_COLS):
        cols = slice(c0, c0 + CONF_COLS)
        gcols = slice(cw + c0, cw + c0 + CONF_COLS)
        a = (_dot(u, w1_ref[:, cols]) + b1_ref[:, cols]) * _sigmoid(
            _dot(u, w1_ref[:, gcols]) + b1_ref[:, gcols])
        a = jnp.where(valid, a, 0.0)
        accs.append(_dwconv_rows(a, wd_ref, bd_ref[:, cols], HALO - CONF_CONV // 2, CONF_CONV, tm,
                                 cols))
    acc = jnp.concatenate(accs, axis=1)
    y = _silu(_layernorm(acc, lng_ref[...], lnb_ref[...])).astype(BF16)
    out = _dot(y, w2_ref[...]) + b2_ref[...]
    o_ref[...] = h_ref[...] + _mod_vec(mod_ref, 2) * out


def _conformer_call(h, mod, mod_row, norm_g, w1, b1, wd, bd, ln_g, ln_b, w2, b2, tm):
    bsz, seq, d = h.shape
    n_tiles = seq // tm
    cw = w2.shape[0]
    hv = h.reshape(bsz, seq // HALO, HALO, d)
    prev, nxt = _halo_specs(tm, n_tiles, d)
    const = lambda shape: pl.BlockSpec(shape, lambda b, i: (0,) * len(shape))
    tile = pl.BlockSpec((None, tm, d), lambda b, i: (b, i, 0))
    return pl.pallas_call(
        _conformer_kernel,
        out_shape=jax.ShapeDtypeStruct((bsz, seq, d), F32),
        grid=(bsz, n_tiles),
        in_specs=[
            prev, tile, nxt,
            pl.BlockSpec((None, 1, mod.shape[-1]), lambda b, i: (mod_row(b), 0, 0)),
            const((1, d)), _resident(w1.shape), const((1, 2 * cw)), const(wd.shape),
            const((1, cw)), const((1, cw)), const((1, cw)), _resident(w2.shape), const((1, d)),
        ],
        out_specs=tile,
        compiler_params=_cparams("parallel", "parallel"),
        name="conformer",
    )(hv, h, hv, mod, norm_g.reshape(1, d), w1, b1.reshape(1, 2 * cw), wd, bd.reshape(1, cw),
      ln_g.reshape(1, cw), ln_b.reshape(1, cw), w2, b2.reshape(1, d))


def _mlp_kernel(h_ref, mod_ref, g_ref, w1_ref, w2_ref, fg_ref, o_ref, a_ref, *, final_norm):
    hf = h_ref[...]
    u = _modulate(hf, g_ref[...], _mod_vec(mod_ref, 3), _mod_vec(mod_ref, 4)).astype(BF16)
    hid = w1_ref.shape[1]
    step = min(hid, 1024)
    for k in range(0, hid, step):
        r = jnp.maximum(_dot(u, w1_ref[:, k:k + step]), 0.0)
        a_ref[:, k:k + step] = (r * r).astype(BF16)
    out = hf + _mod_vec(mod_ref, 5) * _dot(a_ref[...], w2_ref[...])
    if final_norm:
        ms = jnp.mean(out * out, axis=-1, keepdims=True)
        out = out * lax.rsqrt(ms + EPS) * fg_ref[...]
    o_ref[...] = out


def _mlp_call(h, mod, mod_row, norm_g, w1, w2, final_g, final_norm, tm):
    bsz, seq, d = h.shape
    const = lambda shape: pl.BlockSpec(shape, lambda b, i: (0,) * len(shape))
    tile = pl.BlockSpec((None, tm, d), lambda b, i: (b, i, 0))
    return pl.pallas_call(
        functools.partial(_mlp_kernel, final_norm=final_norm),
        out_shape=jax.ShapeDtypeStruct((bsz, seq, d), F32),
        grid=(bsz, seq // tm),
        in_specs=[
            tile,
            pl.BlockSpec((None, 1, mod.shape[-1]), lambda b, i: (mod_row(b), 0, 0)),
            const((1, d)), _resident(w1.shape), _resident(w2.shape), const((1, d)),
        ],
        out_specs=tile,
        scratch_shapes=[pltpu.VMEM((tm, w1.shape[1]), BF16)],
        compiler_params=_cparams("parallel", "parallel"),
        name="mlp_final" if final_norm else "mlp",
    )(h, mod, norm_g.reshape(1, d), w1, w2, final_g.reshape(1, d))


def _conf_mlp_kernel(hp_ref, h_ref, hn_ref, moda_ref, modb_ref, g_ref, w1_ref, b1_ref, wd_ref,
                     bd_ref, lng_ref, lnb_ref, w2_ref, b2_ref, mg_ref, mw1_ref, mw2_ref, fg_ref,
                     o_ref, hmid_ref, *, n_tiles, final_norm):
    s = pl.program_id(0)
    tm, d = h_ref.shape
    cw = w2_ref.shape[0]
    slot = lax.rem(s + 1, 2)

    @pl.when(s == 0)
    def _():
        hmid_ref[...] = jnp.zeros_like(hmid_ref)

    hm = hmid_ref[slot]
    um = _modulate(hm, mg_ref[...], _mod_vec(modb_ref, 3), _mod_vec(modb_ref, 4)).astype(BF16)
    ext = _ext_rows(hp_ref, h_ref, hn_ref)
    uc = _modulate(ext, g_ref[...], _mod_vec(moda_ref, 0), _mod_vec(moda_ref, 1)).astype(BF16)
    i = lax.rem(jnp.minimum(s, pl.num_programs(0) - 2), n_tiles)
    row = lax.broadcasted_iota(jnp.int32, (tm + 2 * HALO, 1), 0)
    valid = jnp.logical_and(jnp.logical_or(row >= HALO, i > 0),
                            jnp.logical_or(row < HALO + tm, i < n_tiles - 1))

    hid = mw1_ref.shape[1]
    n_blk = cw // CONF_COLS
    step = hid // n_blk
    accs = []
    mlp = None
    for j in range(n_blk):
        cols = slice(j * CONF_COLS, (j + 1) * CONF_COLS)
        gcols = slice(cw + j * CONF_COLS, cw + (j + 1) * CONF_COLS)
        r = jnp.maximum(_dot(um, mw1_ref[:, j * step:(j + 1) * step]), 0.0)
        part = _dot((r * r).astype(BF16), mw2_ref[j * step:(j + 1) * step, :])
        mlp = part if mlp is None else mlp + part
        a = (_dot(uc, w1_ref[:, cols]) + b1_ref[:, cols]) * _sigmoid(
            _dot(uc, w1_ref[:, gcols]) + b1_ref[:, gcols])
        a = jnp.where(valid, a, 0.0)
        accs.append(_dwconv_rows(a, wd_ref, bd_ref[:, cols], HALO - CONF_CONV // 2, CONF_CONV, tm,
                                 cols))

    out = hm + _mod_vec(modb_ref, 5) * mlp
    if final_norm:
        ms = jnp.mean(out * out, axis=-1, keepdims=True)
        out = out * lax.rsqrt(ms + EPS) * fg_ref[...]
    o_ref[...] = out

    acc = jnp.concatenate(accs, axis=1)
    y = _silu(_layernorm(acc, lng_ref[...], lnb_ref[...])).astype(BF16)
    outc = _dot(y, w2_ref[...]) + b2_ref[...]
    hmid_ref[lax.rem(s, 2)] = h_ref[...] + _mod_vec(moda_ref, 2) * outc


def _conf_mlp_call(h, mod, mod_row, cg, w1, b1, wd, bd, ln_g, ln_b, w2, b2, mg, mw1, mw2, final_g,
                   final_norm, tm):
    bsz, seq, d = h.shape
    n_tiles = seq // tm
    total = bsz * n_tiles
    cw = w2.shape[0]
    per = tm // HALO
    hv = h.reshape(bsz, seq // HALO, HALO, d)
    cur = lambda s: jnp.minimum(s, total - 1)
    done = lambda s: jnp.maximum(s - 1, 0)
    const = lambda shape: pl.BlockSpec(shape, lambda s: (0,) * len(shape))
    mod_spec = lambda t: pl.BlockSpec((None, 1, mod.shape[-1]),
                                      lambda s: (mod_row(t(s) // n_tiles), 0, 0))
    return pl.pallas_call(
        functools.partial(_conf_mlp_kernel, n_tiles=n_tiles, final_norm=final_norm),
        out_shape=jax.ShapeDtypeStruct((bsz, seq, d), F32),
        grid=(total + 1,),
        in_specs=[
            pl.BlockSpec((None, None, HALO, d), lambda s: (
                cur(s) // n_tiles, jnp.maximum(cur(s) % n_tiles * per - 1, 0), 0, 0)),
            pl.BlockSpec((None, tm, d), lambda s: (cur(s) // n_tiles, cur(s) % n_tiles, 0)),
            pl.BlockSpec((None, None, HALO, d), lambda s: (
                cur(s) // n_tiles, jnp.minimum((cur(s) % n_tiles + 1) * per, n_tiles * per - 1),
                0, 0)),
            mod_spec(cur), mod_spec(done),
            const((1, d)), _resident(w1.shape), const((1, 2 * cw)), const(wd.shape),
            const((1, cw)), const((1, cw)), const((1, cw)), _resident(w2.shape), const((1, d)),
            const((1, d)), _resident(mw1.shape), _resident(mw2.shape), const((1, d)),
        ],
        out_specs=pl.BlockSpec((None, tm, d), lambda s: (done(s) // n_tiles, done(s) % n_tiles, 0)),
        scratch_shapes=[pltpu.VMEM((2, tm, d), F32)],
        compiler_params=_cparams("arbitrary"),
        name="conf_mlp_final" if final_norm else "conf_mlp",
    )(hv, h, hv, mod, mod, cg.reshape(1, d), w1, b1.reshape(1, 2 * cw), wd, bd.reshape(1, cw),
      ln_g.reshape(1, cw), ln_b.reshape(1, cw), w2, b2.reshape(1, d), mg.reshape(1, d), mw1, mw2,
      final_g.reshape(1, d))


def _tile(seq, want):
    return min(seq, want)


def _hybrid_mixer(h, mod, mod_row, norm_g, p, h0_f, h0_b, need_out, mlp=None):
    seq = h.shape[1]
    zs, xbc, dt, gate, v = _inproj_call(
        h, mod, mod_row, norm_g, p["w_in"], p["conv_w"], p["conv_b"], p["dt_bias"],
        p["ln_g"], p["ln_b"], _tile(seq, TOKEN_TILE))
    ts = _tile(seq, SCAN_TILE)
    tq = _tile(seq, TOKEN_TILE)
    hin_f, hfin_f, hin_b, hfin_b = _ssd_state_call(xbc, dt, p["alog"], p["e_f"], p["e_b"], h0_f,
                                                   h0_b, ts)
    out = None
    args = (h, xbc, dt, zs, gate, v, hin_f, hin_b, p["alog"], p["e_f"], p["e_b"], p["dskip"],
            p["norm_g"], p["sg_w"], p["sg_b"], p["w_out"], mod, mod_row)
    if mlp is not None:
        out = _mix_mlp_call(*args, *mlp, tq)
    elif need_out:
        out = _mix_out_call(*args, tq)
    return out, hfin_f, hfin_b


def _hybrid_params(j, hy_w_in, ssd_conv_w, ssd_conv_b, ssd_dt_bias, ssd_a_log, ssd_d, ssd_norm_g,
                   sg_ln_g, sg_ln_b, sg_w, sg_b, hy_w_out):
    d_inner = SSD_HEADS * SSD_HEAD_DIM
    n_xbc = d_inner + 2 * SSD_GROUPS * SSD_STATE
    w = hy_w_in[j]
    o2 = d_inner + n_xbc
    o3 = o2 + 2 * SSD_HEADS
    pad = jnp.zeros((w.shape[0], LANES - 2 * SSD_HEADS), w.dtype)
    w_in_r = jnp.concatenate([w[:, :o2], w[:, o3:], w[:, o2:o3], pad], axis=1).astype(BF16)
    lane_pad = lambda t: jnp.pad(t.reshape(1, -1), ((0, 0), (0, LANES - t.size)))
    head = jnp.arange(d_inner, dtype=jnp.int32) // SSD_HEAD_DIM
    row = jnp.arange(LANES, dtype=jnp.int32)[:, None]
    sg_width = SG_GROUPS * SG_GROUP_DIM
    sg_b_exp = jnp.broadcast_to(sg_b[j].T[:, :, None],
                                (CHUNK, SG_GROUPS, SG_GROUP_DIM)).reshape(CHUNK, sg_width)
    sg_pairs = sg_w[j].reshape(SG_GROUPS // 2, 2, CHUNK, CHUNK).transpose(0, 2, 1, 3).reshape(
        SG_GROUPS // 2, CHUNK, 2 * CHUNK)
    return dict(
        w_in=w_in_r, conv_w=ssd_conv_w[j], conv_b=ssd_conv_b[j],
        dt_bias=lane_pad(ssd_dt_bias[j]), alog=lane_pad(ssd_a_log[j]),
        e_f=(row == head[None, :]).astype(BF16),
        e_b=(row == head[None, :] + SSD_HEADS).astype(BF16),
        dskip=jnp.repeat(ssd_d[j], SSD_HEAD_DIM).reshape(1, d_inner),
        norm_g=ssd_norm_g[j], ln_g=sg_ln_g[j], ln_b=sg_ln_b[j],
        sg_w=sg_pairs.astype(BF16), sg_b=sg_b_exp, w_out=hy_w_out[j].astype(BF16))


def kernel(x, c, ctx, c_ctx, ada_w, ada_b, norm_mix_g, norm_mlp_g, mlp_w1, mlp_w2, hy_w_in,
           ssd_conv_w, ssd_conv_b, ssd_dt_bias, ssd_a_log, ssd_d, ssd_norm_g, sg_ln_g, sg_ln_b,
           sg_w, sg_b, hy_w_out, cf_w_pw1, cf_b_pw1, cf_w_dw, cf_b_dw, cf_ln_g, cf_ln_b, cf_w_pw2,
           cf_b_pw2, final_norm_g):
    bsz, seq, d = x.shape
    depth = ada_w.shape[0]
    assert bsz < MOD_ROWS and seq % 512 == 0 and ctx.shape[1] % (2 * CHUNK) == 0
    cs = jnp.concatenate([c, c_ctx[None, :], jnp.zeros((MOD_ROWS - bsz - 1, d), c.dtype)], axis=0)
    mod = _ada_call(cs, ada_w, ada_b)

    gw = SSD_HEADS * SSD_HEAD_DIM // SSD_GROUPS
    zeros = jnp.zeros((bsz, SSD_GROUPS, SSD_STATE, gw), F32)
    h, hc = x, ctx
    for i in range(depth):
        ctx_needed = i < depth - 1
        last = i == depth - 1
        lat_row = lambda b, i=i: i * MOD_ROWS + b
        ctx_row = lambda b, i=i: i * MOD_ROWS + bsz
        j = i // 2
        if i % 2 == 0:
            p = _hybrid_params(j, hy_w_in, ssd_conv_w, ssd_conv_b, ssd_dt_bias, ssd_a_log, ssd_d,
                               ssd_norm_g, sg_ln_g, sg_ln_b, sg_w, sg_b, hy_w_out)
            hc_new, h0_f, h0_b = _hybrid_mixer(hc, mod, ctx_row, norm_mix_g[i], p, zeros, zeros,
                                               ctx_needed)
            h, _, _ = _hybrid_mixer(h, mod, lat_row, norm_mix_g[i], p, h0_f, h0_b, True,
                                    mlp=(norm_mlp_g[i], mlp_w1[i].astype(BF16),
                                         mlp_w2[i].astype(BF16), final_norm_g, last))
            if ctx_needed:
                hc = hc_new
        else:
            w1 = cf_w_pw1[j].astype(BF16)
            w2 = cf_w_pw2[j].astype(BF16)
            args = (w1, cf_b_pw1[j], cf_w_dw[j], cf_b_dw[j], cf_ln_g[j], cf_ln_b[j], w2,
                    cf_b_pw2[j])
            h = _conf_mlp_call(h, mod, lat_row, norm_mix_g[i], *args, norm_mlp_g[i],
                               mlp_w1[i].astype(BF16), mlp_w2[i].astype(BF16), final_norm_g, last,
                               _tile(seq, TOKEN_TILE))
            if ctx_needed:
                hc = _conformer_call(hc, mod, ctx_row, norm_mix_g[i], *args,
                                     _tile(hc.shape[1], TOKEN_TILE))
        if ctx_needed:
            hc = _mlp_call(hc, mod, ctx_row, norm_mlp_g[i], mlp_w1[i].astype(BF16),
                           mlp_w2[i].astype(BF16), final_norm_g, False,
                           _tile(hc.shape[1], MLP_TILE))
    return h
```
